```python
import jax, jax.numpy as jnp
from jax import lax
import numpy as np

D_MODEL = 2048
BATCH = 2
SEQ = 4096
DEPTH = 1

CHUNK = 64
ATTN_HEADS = 8
ATTN_HEAD_DIM = D_MODEL // (2 * ATTN_HEADS)
ATTN_WIDTH = ATTN_HEADS * ATTN_HEAD_DIM
LEFT_CHUNKS = 8
BAND = (LEFT_CHUNKS + 1) * CHUNK
MAX_REL = 256
REL_TABLE = MAX_REL + CHUNK
RET_HEADS = 8
RET_VALUE_DIM = D_MODEL // (2 * RET_HEADS)
RET_KEY_DIM = RET_VALUE_DIM // 2
RET_QK_WIDTH = RET_HEADS * RET_KEY_DIM
RET_V_WIDTH = RET_HEADS * RET_VALUE_DIM
ROPE_BASE = 10000.0
MIX_WIDTH = ATTN_WIDTH + RET_V_WIDTH
IN_WIDTH = 3 * ATTN_WIDTH + 2 * RET_QK_WIDTH + 2 * RET_V_WIDTH
SPLITS = [ATTN_WIDTH, 2 * ATTN_WIDTH, 3 * ATTN_WIDTH,
          3 * ATTN_WIDTH + RET_QK_WIDTH, 3 * ATTN_WIDTH + 2 * RET_QK_WIDTH,
          3 * ATTN_WIDTH + 2 * RET_QK_WIDTH + RET_V_WIDTH]
N_EXPERTS = 256
TOP_K = 8
N_GROUPS = 8
TOPK_GROUPS = 4
EXPERT_DIM = D_MODEL // 8
SHARED_DIM = EXPERT_DIM
ROUTED_SCALE = 2.5
DISPATCH_BLOCK = 64
ALPHA = (2.0 * DEPTH) ** 0.25
BETA = (8.0 * DEPTH) ** -0.25
N_MOD = 6
EPS = 1e-5

kernel_name = "hybrid_band_attn_retention_moe_block"


def layer_norm(x, gain, bias):
    xf = x.astype(jnp.float32)
    mu = jnp.mean(xf, axis=-1, keepdims=True)
    var = jnp.mean(jnp.square(xf - mu), axis=-1, keepdims=True)
    return ((xf - mu) * lax.rsqrt(var + EPS) * gain + bias).astype(x.dtype)


def rms_norm(x, gain):
    xf = x.astype(jnp.float32)
    return (xf * lax.rsqrt(jnp.mean(jnp.square(xf), axis=-1, keepdims=True) + EPS) * gain).astype(x.dtype)


def head_group_norm(x):
    xf = x.astype(jnp.float32)
    mu = jnp.mean(xf, axis=-1, keepdims=True)
    var = jnp.mean(jnp.square(xf - mu), axis=-1, keepdims=True)
    return (xf - mu) * lax.rsqrt(var + EPS)


def rotary(x):
    seq, dk = x.shape[1], x.shape[-1]
    half = dk // 2
    inv = ROPE_BASE ** (-jnp.arange(half, dtype=jnp.float32) / half)
    ang = jnp.arange(seq, dtype=jnp.float32)[:, None] * inv[None, :]
    cos, sin = jnp.cos(ang)[None, :, None, :], jnp.sin(ang)[None, :, None, :]
    x1, x2 = x[..., :half], x[..., half:]
    return jnp.concatenate([x1 * cos - x2 * sin, x2 * cos + x1 * sin], axis=-1)


def band_attention(q, k, v, rel_bias):
    b, s, h, dh = q.shape
    nc = s // CHUNK
    pad = LEFT_CHUNKS * CHUNK
    kp = jnp.pad(k, ((0, 0), (pad, 0), (0, 0), (0, 0))).reshape(b, nc + LEFT_CHUNKS, CHUNK, h, dh)
    vp = jnp.pad(v, ((0, 0), (pad, 0), (0, 0), (0, 0))).reshape(b, nc + LEFT_CHUNKS, CHUNK, h, dh)
    k_band = jnp.concatenate([kp[:, w:w + nc] for w in range(LEFT_CHUNKS + 1)], axis=2)
    v_band = jnp.concatenate([vp[:, w:w + nc] for w in range(LEFT_CHUNKS + 1)], axis=2)
    qc = q.reshape(b, nc, CHUNK, h, dh)
    scores = jnp.einsum('bcihd,bcjhd->bhcij', qc, k_band).astype(jnp.float32) * (dh ** -0.5)
    qi = jnp.arange(CHUNK)[:, None]
    kj = jnp.arange(BAND)[None, :]
    dist = pad + qi - kj
    rel_idx = jnp.clip(dist, -(CHUNK - 1), MAX_REL) + (CHUNK - 1)
    bias = rel_bias[:, rel_idx].astype(jnp.float32)
    key_pos = jnp.arange(nc)[:, None] * CHUNK - pad + kj
    valid = key_pos >= 0
    scores = jnp.where(valid[None, None, :, None, :], scores + bias[None, :, None], -jnp.inf)
    probs = jax.nn.softmax(scores, axis=-1).astype(v.dtype)
    out = jnp.einsum('bhcij,bcjhd->bcihd', probs, v_band)
    return out.reshape(b, s, h * dh)


def retention(q, k, v):
    q, k, v = q.astype(jnp.float32), k.astype(jnp.float32), v.astype(jnp.float32)
    b, s, h, dk = q.shape
    dv = v.shape[-1]
    nc = s // CHUNK
    log_g = jnp.log(1.0 - 2.0 ** (-5.0 - jnp.arange(h, dtype=jnp.float32)))
    n = jnp.arange(CHUNK, dtype=jnp.float32)
    d_intra = jnp.exp(log_g[:, None, None] * jnp.abs(n[:, None] - n[None, :]))
    xi = jnp.exp(log_g[None, :] * (n[:, None] + 1.0))
    zeta = jnp.exp(log_g[None, :] * (CHUNK - 1.0 - n[:, None]))
    g_chunk = jnp.exp(log_g * CHUNK)
    k = k * (dk ** -0.5)
    qc = q.reshape(b, nc, CHUNK, h, dk)
    kc = k.reshape(b, nc, CHUNK, h, dk)
    vc = v.reshape(b, nc, CHUNK, h, dv)
    intra_scores = jnp.einsum('bcnhk,bcmhk->bchnm', qc, kc) * d_intra[None, None]
    intra = jnp.einsum('bchnm,bcmhv->bcnhv', intra_scores, vc)
    updates = jnp.einsum('bcmhk,bcmhv->chkvb', kc * zeta[None, None, :, :, None], vc)

    def step(state, u):
        return g_chunk[:, None, None, None] * state + u, state

    _, r_prev = lax.scan(step, jnp.zeros((h, dk, dv, b), jnp.float32), updates)
    cross = jnp.einsum('bcnhk,chkvb->bcnhv', qc * xi[None, None, :, :, None], r_prev)
    return (intra + cross).reshape(b, s, h, dv)


def token_mixer(h, w_in, rel_bias, attn_gain, ret_gain, w_out):
    b, s, _ = h.shape
    proj = h @ w_in
    qa, ka, va, qb, kb, vb, gb = jnp.split(proj, SPLITS, axis=-1)
    qa = qa.reshape(b, s, ATTN_HEADS, ATTN_HEAD_DIM)
    ka = ka.reshape(b, s, ATTN_HEADS, ATTN_HEAD_DIM)
    va = va.reshape(b, s, ATTN_HEADS, ATTN_HEAD_DIM)
    y_a = rms_norm(band_attention(qa, ka, va, rel_bias), attn_gain)
    qb = rotary(qb.reshape(b, s, RET_HEADS, RET_KEY_DIM))
    kb = rotary(kb.reshape(b, s, RET_HEADS, RET_KEY_DIM))
    vb = vb.reshape(b, s, RET_HEADS, RET_VALUE_DIM)
    ret = head_group_norm(retention(qb, kb, vb)).reshape(b, s, RET_V_WIDTH)
    y_b = (jax.nn.silu(gb.astype(jnp.float32)) * ret * ret_gain).astype(h.dtype)
    return jnp.concatenate([y_a, y_b], axis=-1) @ w_out


def swiglu(x, wg, wu, wd):
    return (jax.nn.silu(x @ wg) * (x @ wu)) @ wd


def moe_ffn(h, w_router, router_bias, w_gate, w_up, w_down, ws_gate, ws_up, ws_down):
    t, d = h.shape
    scores = jax.nn.sigmoid((h @ w_router).astype(jnp.float32))
    biased = scores + router_bias.astype(jnp.float32)
    per_group = N_EXPERTS // N_GROUPS
    grp_score = lax.top_k(biased.reshape(t, N_GROUPS, per_group), 2)[0].sum(-1)
    _, top_grp = lax.top_k(grp_score, TOPK_GROUPS)
    grp_mask = jnp.any(top_grp[:, :, None] == jnp.arange(N_GROUPS)[None, None, :], axis=1)
    masked = jnp.where(jnp.repeat(grp_mask, per_group, axis=1), biased, -jnp.inf)
    _, idx = lax.top_k(masked, TOP_K)
    w = jnp.take_along_axis(scores, idx, axis=1)
    w = w / jnp.sum(w, axis=-1, keepdims=True) * ROUTED_SCALE

    n_assign = t * TOP_K
    flat_e = idx.reshape(-1)
    flat_tok = jnp.repeat(jnp.arange(t, dtype=jnp.int32), TOP_K)
    flat_w = w.reshape(-1)
    order = jnp.argsort(flat_e)
    e_sorted = flat_e[order]
    counts = jnp.bincount(flat_e, length=N_EXPERTS)
    start = jnp.cumsum(counts) - counts
    padded = (counts + DISPATCH_BLOCK - 1) // DISPATCH_BLOCK * DISPATCH_BLOCK
    pend = jnp.cumsum(padded)
    pstart = pend - padded
    dest = pstart[e_sorted] + (jnp.arange(n_assign) - start[e_sorted])
    buf = n_assign + N_EXPERTS * DISPATCH_BLOCK
    n_blocks = buf // DISPATCH_BLOCK
    tok_buf = jnp.full((buf,), t, jnp.int32).at[dest].set(flat_tok[order]).reshape(n_blocks, DISPATCH_BLOCK)
    w_buf = jnp.zeros((buf,), jnp.float32).at[dest].set(flat_w[order]).reshape(n_blocks, DISPATCH_BLOCK)
    blk_expert = jnp.minimum(
        jnp.searchsorted(pend, jnp.arange(n_blocks) * DISPATCH_BLOCK, side='right'), N_EXPERTS - 1)
    h_pad = jnp.concatenate([h, jnp.zeros((1, d), h.dtype)], axis=0)

    def step(acc, blk):
        tok, wb, e = blk
        y = swiglu(h_pad[tok], w_gate[e], w_up[e], w_down[e])
        return acc.at[tok].add((y * wb[:, None]).astype(acc.dtype)), None

    routed, _ = lax.scan(step, jnp.zeros((t + 1, d), h.dtype), (tok_buf, w_buf, blk_expert))
    return routed[:t] + swiglu(h, ws_gate, ws_up, ws_down)


def setup_inputs(seed: int = 0) -> dict:
    key = jax.random.key(seed)
    ks = jax.random.split(key, 21)
    L, D = DEPTH, D_MODEL

    def nrm(k, shape, scale):
        return jax.random.normal(k, shape, jnp.float32) * scale

    col_scale = jnp.concatenate([
        jnp.ones((2 * ATTN_WIDTH,), jnp.float32), jnp.full((ATTN_WIDTH,), BETA, jnp.float32),
        jnp.ones((2 * RET_QK_WIDTH,), jnp.float32), jnp.full((RET_V_WIDTH,), BETA, jnp.float32),
        jnp.ones((RET_V_WIDTH,), jnp.float32)])
    return {
        "x": nrm(ks[0], (BATCH, SEQ, D), 1.0),
        "c": nrm(ks[1], (BATCH, D), 1.0),
        "w_ada": nrm(ks[2], (L, D, N_MOD * D), 0.5 * D ** -0.5),
        "b_ada": nrm(ks[3], (L, N_MOD * D), 0.02),
        "w_in": nrm(ks[4], (L, D, IN_WIDTH), D ** -0.5) * col_scale,
        "rel_bias": nrm(ks[5], (L, ATTN_HEADS, REL_TABLE), 0.5),
        "attn_gain": 1.0 + nrm(ks[6], (L, ATTN_WIDTH), 0.02),
        "ret_gain": 1.0 + nrm(ks[7], (L, RET_V_WIDTH), 0.02),
        "w_out": nrm(ks[8], (L, MIX_WIDTH, D), BETA * MIX_WIDTH ** -0.5),
        "ln1_gain": 1.0 + nrm(ks[9], (L, D), 0.02),
        "ln1_bias": nrm(ks[10], (L, D), 0.02),
        "w_router": nrm(ks[11], (L, D, N_EXPERTS), D ** -0.5),
        "router_bias": nrm(ks[12], (L, N_EXPERTS), 0.01),
        "w_gate": nrm(ks[13], (L, N_EXPERTS, D, EXPERT_DIM), D ** -0.5),
        "w_up": nrm(ks[14], (L, N_EXPERTS, D, EXPERT_DIM), D ** -0.5),
        "w_down": nrm(ks[15], (L, N_EXPERTS, EXPERT_DIM, D), BETA * EXPERT_DIM ** -0.5),
        "ws_gate": nrm(ks[16], (L, D, SHARED_DIM), D ** -0.5),
        "ws_up": nrm(ks[17], (L, D, SHARED_DIM), D ** -0.5),
        "ws_down": nrm(ks[18], (L, SHARED_DIM, D), BETA * SHARED_DIM ** -0.5),
        "ln2_gain": 1.0 + nrm(ks[19], (L, D), 0.02),
        "ln2_bias": nrm(ks[20], (L, D), 0.02),
    }


def reference(x, c, w_ada, b_ada, w_in, rel_bias, attn_gain, ret_gain, w_out, ln1_gain, ln1_bias,
              w_router, router_bias, w_gate, w_up, w_down, ws_gate, ws_up, ws_down, ln2_gain, ln2_bias):
    b, s, d = x.shape
    for l in range(DEPTH):
        mod = jax.nn.silu(c) @ w_ada[l] + b_ada[l]
        sh1, sc1, g1, sh2, sc2, g2 = jnp.split(mod[:, None, :], N_MOD, axis=-1)
        h = x * (1.0 + sc1) + sh1
        y = token_mixer(h, w_in[l], rel_bias[l], attn_gain[l], ret_gain[l], w_out[l])
        x = layer_norm(ALPHA * x + g1 * y, ln1_gain[l], ln1_bias[l])
        h = x * (1.0 + sc2) + sh2
        y = moe_ffn(h.reshape(b * s, d), w_router[l], router_bias[l], w_gate[l], w_up[l], w_down[l],
                    ws_gate[l], ws_up[l], ws_down[l]).reshape(b, s, d)
        x = layer_norm(ALPHA * x + g2 * y, ln2_gain[l], ln2_bias[l])
    return x
```

```python
import functools
import math

import jax
import jax.numpy as jnp
import numpy as np
from jax import lax
from jax.experimental import pallas as pl
from jax.experimental.pallas import tpu as pltpu

F32 = jnp.float32
BF16 = jnp.bfloat16
U32 = jnp.uint32

CHUNK = 64
LEFT_CHUNKS = 8
MAX_REL = 256
ATTN_HEADS = 8
HEAD_DIM = 128
RET_HEADS = 8
RET_KEY_DIM = 64
RET_VALUE_DIM = 128
ROPE_BASE = 10000.0
N_GROUPS = 8
TOPK_GROUPS = 4
TOP_K = 8
ROUTED_SCALE = 2.5
EPS = 1e-5
N_MOD = 6
LANES = 128

SEQ_BLOCK = 256
MOE_BLOCK = 128
COMBINE_TOKENS = 128
NEG_BIG = -1e30
VMEM_LIMIT = 56 * 1024 * 1024


def _sigmoid(v):
    return 1.0 / (1.0 + jnp.exp(-v))


def _dot(a, b):
    return jnp.dot(a, b, preferred_element_type=F32)


def _dot_t(a, b):
    return lax.dot_general(a, b, (((1,), (1,)), ((), ())), preferred_element_type=F32)


def _ada_kernel(c_ref, w_ref, b_ref, o_ref):
    c = c_ref[...]
    s = (c * _sigmoid(c)).astype(BF16)
    o_ref[...] = _dot(s, w_ref[...].astype(BF16)) + b_ref[...]


def _ada(c_pad, w_ada, b_ada):
    d, n = w_ada.shape
    tn = 1024
    return pl.pallas_call(
        _ada_kernel,
        out_shape=jax.ShapeDtypeStruct((8, n), F32),
        grid=(n // tn,),
        in_specs=[pl.BlockSpec((8, d), lambda j: (0, 0)),
                  pl.BlockSpec((d, tn), lambda j: (0, j)),
                  pl.BlockSpec((1, tn), lambda j: (0, j))],
        out_specs=pl.BlockSpec((8, tn), lambda j: (0, j)),
        compiler_params=pltpu.CompilerParams(dimension_semantics=("arbitrary",), vmem_limit_bytes=VMEM_LIMIT),
        name="ada",
    )(c_pad, w_ada, b_ada.reshape(1, n))


def _inproj_kernel(x_ref, mod_ref, w_ref, o_ref, wbf_ref):
    @pl.when(pl.program_id(1) == 0)
    def _():
        wbf_ref[...] = w_ref[...].astype(BF16)

    shift = mod_ref[0, 0:1, :]
    scale = mod_ref[0, 1:2, :]
    h = (x_ref[...] * (1.0 + scale) + shift).astype(BF16)
    o_ref[...] = _dot(h, wbf_ref[...]).astype(o_ref.dtype)


def _inproj(x2, mod, w_in, seq):
    t, d = x2.shape
    n = w_in.shape[1]
    tm, tn = 512, 1024
    per_batch = seq // tm
    return pl.pallas_call(
        _inproj_kernel,
        out_shape=jax.ShapeDtypeStruct((t, n), BF16),
        grid=(n // tn, t // tm),
        in_specs=[pl.BlockSpec((tm, d), lambda j, i: (i, 0)),
                  pl.BlockSpec((1, N_MOD, d), lambda j, i: (i // per_batch, 0, 0)),
                  pl.BlockSpec((d, tn), lambda j, i: (0, j))],
        out_specs=pl.BlockSpec((tm, tn), lambda j, i: (i, j)),
        scratch_shapes=[pltpu.VMEM((d, tn), BF16)],
        compiler_params=pltpu.CompilerParams(dimension_semantics=("arbitrary", "arbitrary"),
                                             vmem_limit_bytes=VMEM_LIMIT),
        name="inproj",
    )(x2, mod, w_in)


def _ret_decay_consts(blk):
    h = np.arange(RET_HEADS, dtype=np.float64)
    log_g = np.log(1.0 - 2.0 ** (-5.0 - h))
    n = np.arange(blk, dtype=np.float64)
    diff = n[:, None] - n[None, :]
    same = (n[:, None] // CHUNK) == (n[None, :] // CHUNK)
    later = (n[:, None] // CHUNK) > (n[None, :] // CHUNK)
    expo = np.where(same, np.abs(diff), diff)
    kscale = RET_KEY_DIM ** -0.5
    decay = np.where(same | later, np.exp(log_g[:, None, None] * expo[None]), 0.0) * kscale
    xi = np.exp(log_g[:, None] * (n[None, :] + 1.0))
    zeta = np.exp(log_g[:, None] * (blk - 1.0 - n[None, :])) * kscale
    g_blk = np.exp(log_g * blk)
    xi = np.broadcast_to(xi[:, :, None], (RET_HEADS, blk, LANES))
    zeta = np.broadcast_to(zeta[:, :, None], (RET_HEADS, blk, LANES))
    return (jnp.asarray(decay, F32), jnp.asarray(xi, F32), jnp.asarray(zeta, F32), [float(v) for v in g_blk])


def _mixer_kernel(g_blk, qa, ka0, ka1, ka2, va0, va1, va2, qb, kb, vb, gb, bias, cos, sin, decay, xi, zeta,
                  again, rgain, out, state, oa):
    i = pl.program_id(1)
    blk = qa.shape[0]

    @pl.when(i == 0)
    def _():
        state[...] = jnp.zeros_like(state)

    ok0 = i >= 2
    ok1 = i >= 1
    scale = HEAD_DIM ** -0.5
    ssq = jnp.zeros((blk, 1), F32)
    for h in range(ATTN_HEADS):
        sl = slice(h * HEAD_DIM, (h + 1) * HEAD_DIM)
        q = qa[:, sl]
        s0 = jnp.where(ok0, _dot_t(q, ka0[:, sl]) * scale + bias[h, :, 0:blk], NEG_BIG)
        s1 = jnp.where(ok1, _dot_t(q, ka1[:, sl]) * scale + bias[h, :, blk:2 * blk], NEG_BIG)
        s2 = _dot_t(q, ka2[:, sl]) * scale + bias[h, :, 2 * blk:3 * blk]
        m = jnp.maximum(jnp.maximum(jnp.max(s0, axis=-1, keepdims=True), jnp.max(s1, axis=-1, keepdims=True)),
                        jnp.max(s2, axis=-1, keepdims=True))
        p0 = jnp.exp(s0 - m)
        p1 = jnp.exp(s1 - m)
        p2 = jnp.exp(s2 - m)
        denom = (jnp.sum(p0, axis=-1, keepdims=True) + jnp.sum(p1, axis=-1, keepdims=True)
                 + jnp.sum(p2, axis=-1, keepdims=True))
        o = _dot(p0.astype(BF16), va0[:, sl]) + _dot(p1.astype(BF16), va1[:, sl]) + _dot(p2.astype(BF16), va2[:, sl])
        o = o * (1.0 / denom)
        oa[:, sl] = o
        ssq = ssq + jnp.sum(o * o, axis=-1, keepdims=True)
    width_a = ATTN_HEADS * HEAD_DIM
    inv_rms = lax.rsqrt(ssq * (1.0 / width_a) + EPS)
    out[:, 0:width_a] = (oa[...] * inv_rms * again[...]).astype(out.dtype)

    cosv = cos[...]
    sinv = sin[...]
    lane = lax.broadcasted_iota(jnp.int32, (blk, LANES), 1)
    first_half = (lane % RET_KEY_DIM) < (RET_KEY_DIM // 2)
    low_head = lane < RET_KEY_DIM

    def rope(v):
        rot = jnp.where(first_half, pltpu.roll(v, LANES - RET_KEY_DIM // 2, 1), pltpu.roll(v, RET_KEY_DIM // 2, 1))
        return v * cosv + rot * sinv

    for j in range(RET_HEADS // 2):
        sl2 = slice(j * LANES, (j + 1) * LANES)
        qr = rope(qb[:, sl2].astype(F32))
        k2 = rope(kb[:, sl2].astype(F32)).astype(BF16)
        for hh in range(2):
            h = 2 * j + hh
            slv = slice(h * RET_VALUE_DIM, (h + 1) * RET_VALUE_DIM)
            qm = jnp.where(low_head if hh == 0 else jnp.logical_not(low_head), qr, 0.0).astype(BF16)
            v = vb[:, slv]
            st = state[h]
            scores = _dot_t(qm, k2) * decay[h]
            ret = _dot(scores.astype(BF16), v) + _dot(qm, st.astype(BF16)) * xi[h]
            zv = (v.astype(F32) * zeta[h]).astype(BF16)
            upd = lax.dot_general(k2, zv, (((0,), (0,)), ((), ())), preferred_element_type=F32)
            state[h] = g_blk[h] * st + upd
            mu = jnp.mean(ret, axis=-1, keepdims=True)
            cen = ret - mu
            var = jnp.mean(cen * cen, axis=-1, keepdims=True)
            gate = gb[:, slv].astype(F32)
            yb = gate * _sigmoid(gate) * (cen * lax.rsqrt(var + EPS)) * rgain[:, slv]
            out[:, width_a + h * RET_VALUE_DIM: width_a + (h + 1) * RET_VALUE_DIM] = yb.astype(out.dtype)


def _attn_bias_table(rel_bias, blk):
    r = np.arange(blk)[:, None]
    c = np.arange(3 * blk)[None, :]
    dist = r + 2 * blk - c
    rel_idx = np.clip(dist, -(CHUNK - 1), MAX_REL) + (CHUNK - 1)
    qc = r // CHUNK + (2 * blk) // CHUNK
    kc = c // CHUNK
    in_band = (kc <= qc) & (kc >= qc - LEFT_CHUNKS)
    tab = rel_bias[:, jnp.asarray(rel_idx)].astype(F32)
    return jnp.where(jnp.asarray(in_band)[None], tab, NEG_BIG)


def _rope_tables(seq):
    half = RET_KEY_DIM // 2
    inv = ROPE_BASE ** (-jnp.arange(half, dtype=F32) / half)
    ang = jnp.arange(seq, dtype=F32)[:, None] * inv[None, :]
    cos, sin = jnp.cos(ang), jnp.sin(ang)
    reps = LANES // RET_KEY_DIM
    cos_t = jnp.tile(jnp.concatenate([cos, cos], axis=-1), (1, reps))
    sin_t = jnp.tile(jnp.concatenate([-sin, sin], axis=-1), (1, reps))
    return cos_t, sin_t


def _mixer(proj, rel_bias, attn_gain, ret_gain, batch, seq):
    t = proj.shape[0]
    blk = SEQ_BLOCK
    assert 2 * blk == LEFT_CHUNKS * CHUNK and seq % blk == 0
    nb = seq // blk
    wa = ATTN_HEADS * HEAD_DIM
    wqk = RET_HEADS * RET_KEY_DIM
    wv = RET_HEADS * RET_VALUE_DIM
    assert wa == wv == 2 * wqk
    bias = _attn_bias_table(rel_bias, blk)
    cos_t, sin_t = _rope_tables(seq)
    decay, xi, zeta, g_blk = _ret_decay_consts(blk)

    def row(b, i):
        return b * nb + i

    def kspec(back, col):
        return pl.BlockSpec((blk, wa), lambda b, i: (row(b, jnp.maximum(i - back, 0)), col))

    const3 = lambda b, i: (0, 0, 0)
    in_specs = [
        pl.BlockSpec((blk, wa), lambda b, i: (row(b, i), 0)),
        kspec(2, 1), kspec(1, 1), kspec(0, 1),
        kspec(2, 2), kspec(1, 2), kspec(0, 2),
        pl.BlockSpec((blk, wqk), lambda b, i: (row(b, i), 3 * wa // wqk)),
        pl.BlockSpec((blk, wqk), lambda b, i: (row(b, i), 3 * wa // wqk + 1)),
        pl.BlockSpec((blk, wv), lambda b, i: (row(b, i), (3 * wa + 2 * wqk) // wv)),
        pl.BlockSpec((blk, wv), lambda b, i: (row(b, i), (3 * wa + 2 * wqk) // wv + 1)),
        pl.BlockSpec((ATTN_HEADS, blk, 3 * blk), const3),
        pl.BlockSpec((blk, LANES), lambda b, i: (i, 0)),
        pl.BlockSpec((blk, LANES), lambda b, i: (i, 0)),
        pl.BlockSpec((RET_HEADS, blk, blk), const3),
        pl.BlockSpec((RET_HEADS, blk, LANES), const3),
        pl.BlockSpec((RET_HEADS, blk, LANES), const3),
        pl.BlockSpec((1, wa), lambda b, i: (0, 0)),
        pl.BlockSpec((1, wv), lambda b, i: (0, 0)),
    ]
    return pl.pallas_call(
        functools.partial(_mixer_kernel, g_blk),
        out_shape=jax.ShapeDtypeStruct((t, wa + wv), BF16),
        grid=(batch, nb),
        in_specs=in_specs,
        out_specs=pl.BlockSpec((blk, wa + wv), lambda b, i: (row(b, i), 0)),
        scratch_shapes=[pltpu.VMEM((RET_HEADS, LANES, RET_VALUE_DIM), F32), pltpu.VMEM((blk, wa), F32)],
        compiler_params=pltpu.CompilerParams(dimension_semantics=("arbitrary", "arbitrary"),
                                             vmem_limit_bytes=VMEM_LIMIT),
        name="mixer",
    )(proj, proj, proj, proj, proj, proj, proj, proj, proj, proj, proj, bias, cos_t, sin_t, decay, xi, zeta,
      attn_gain.reshape(1, wa), ret_gain.reshape(1, wv))


def _layer_norm(z, gain, bias):
    mu = jnp.mean(z, axis=-1, keepdims=True)
    cen = z - mu
    var = jnp.mean(cen * cen, axis=-1, keepdims=True)
    return cen * lax.rsqrt(var + EPS) * gain + bias


def _pack_bf16_pairs(lo, hi):
    lo_bits = pltpu.bitcast(lo.astype(BF16).astype(F32), U32)
    hi_bits = pltpu.bitcast(hi.astype(BF16).astype(F32), U32)
    return (hi_bits & jnp.uint32(0xFFFF0000)) | (lo_bits >> 16)


def _unpack_bf16_pairs(words):
    lo = pltpu.bitcast(words << 16, F32).astype(BF16)
    hi = pltpu.bitcast(words & jnp.uint32(0xFFFF0000), F32).astype(BF16)
    return lo, hi


def _outproj_kernel(alpha, mix_ref, x_ref, mod_ref, wout_ref, g_ref, b_ref, wr_ref, x1_ref, h2p_ref, sc_ref):
    tm, d = x_ref.shape
    y = _dot(mix_ref[...], wout_ref[...])
    gate1 = mod_ref[0, 2:3, :]
    x1 = _layer_norm(alpha * x_ref[...] + gate1 * y, g_ref[...], b_ref[...])
    x1_ref[...] = x1
    h2 = x1 * (1.0 + mod_ref[0, 4:5, :]) + mod_ref[0, 3:4, :]
    sc_ref[...] = _sigmoid(_dot(h2.astype(BF16), wr_ref[...]))
    half = d // 2
    nchunk = half // LANES
    for c in range(nchunk):
        words = _pack_bf16_pairs(h2[:, c * LANES:(c + 1) * LANES], h2[:, half + c * LANES: half + (c + 1) * LANES])
        h2p_ref[pl.ds(c, tm, stride=nchunk), :] = words


def _outproj(mix, x2, mod, w_out_bf, ln_g, ln_b, w_router_bf, alpha, seq):
    t, d = x2.shape
    ne = w_router_bf.shape[1]
    tm = 256
    per_batch = seq // tm
    nchunk = d // 2 // LANES
    return pl.pallas_call(
        functools.partial(_outproj_kernel, alpha),
        out_shape=(jax.ShapeDtypeStruct((t, d), F32),
                   jax.ShapeDtypeStruct((t * nchunk, LANES), U32),
                   jax.ShapeDtypeStruct((t, ne), F32)),
        grid=(t // tm,),
        in_specs=[pl.BlockSpec((tm, d), lambda i: (i, 0)),
                  pl.BlockSpec((tm, d), lambda i: (i, 0)),
                  pl.BlockSpec((1, N_MOD, d), lambda i: (i // per_batch, 0, 0)),
                  pl.BlockSpec((d, d), lambda i: (0, 0)),
                  pl.BlockSpec((1, d), lambda i: (0, 0)),
                  pl.BlockSpec((1, d), lambda i: (0, 0)),
                  pl.BlockSpec((d, ne), lambda i: (0, 0))],
        out_specs=(pl.BlockSpec((tm, d), lambda i: (i, 0)),
                   pl.BlockSpec((tm * nchunk, LANES), lambda i: (i, 0)),
                   pl.BlockSpec((tm, ne), lambda i: (i, 0))),
        compiler_params=pltpu.CompilerParams(dimension_semantics=("arbitrary",), vmem_limit_bytes=VMEM_LIMIT),
        name="outproj",
    )(mix, x2, mod, w_out_bf, ln_g.reshape(1, d), ln_b.reshape(1, d), w_router_bf)


def _route(scores, router_bias):
    t, ne = scores.shape
    per_group = ne // N_GROUPS
    biased = scores + router_bias.astype(F32)
    grp_score = lax.top_k(biased.reshape(t, N_GROUPS, per_group), 2)[0].sum(-1)
    _, top_grp = lax.top_k(grp_score, TOPK_GROUPS)
    grp_mask = jnp.any(top_grp[:, :, None] == jnp.arange(N_GROUPS)[None, None, :], axis=1)
    masked = jnp.where(jnp.repeat(grp_mask, per_group, axis=1), biased, -jnp.inf)
    _, idx = lax.top_k(masked, TOP_K)
    w = jnp.take_along_axis(scores, idx, axis=1)
    w = w / jnp.sum(w, axis=-1, keepdims=True) * ROUTED_SCALE
    return idx, w


def _dispatch_plan(idx, w, ne, bm):
    t, k = idx.shape
    nslot = t * k + ne * bm
    nblk = nslot // bm
    onehot = (idx[:, :, None] == jnp.arange(ne, dtype=idx.dtype)[None, None, :])
    chosen = jnp.sum(onehot, axis=1).astype(jnp.int32)
    before = jnp.cumsum(chosen, axis=0) - chosen
    counts = jnp.sum(chosen, axis=0)
    padded = (counts + bm - 1) // bm * bm
    pend = jnp.cumsum(padded)
    pstart = pend - padded
    rank = jnp.take_along_axis(before, idx, axis=1)
    dest = (pstart[idx] + rank).astype(jnp.int32)
    flat_dest = dest.reshape(-1)
    tok = jnp.repeat(jnp.arange(t, dtype=jnp.int32), k)
    tok_slot = jnp.zeros((nslot,), jnp.int32).at[flat_dest].set(tok)
    w_slot = jnp.zeros((nslot,), F32).at[flat_dest].set(w.reshape(-1))
    blk_expert = jnp.minimum(jnp.searchsorted(pend, jnp.arange(nblk, dtype=jnp.int32) * bm, side='right'),
                             ne - 1).astype(jnp.int32)
    n_used = (pend[-1] // bm).astype(jnp.int32).reshape(1)
    return dest, tok_slot, w_slot, blk_expert, n_used


def _moe_kernel(be_ref, nused_ref, tok_ref, h2p_ref, wcol_ref, wg_ref, wu_ref, wd_ref, y_ref,
                wgb, wub, wdb, xbuf):
    i = pl.program_id(0)
    bm = wcol_ref.shape[0]
    nchunk = xbuf.shape[0] // bm
    dout = wdb.shape[1]
    e = be_ref[i]
    prev = be_ref[jnp.maximum(i - 1, 0)]

    @pl.when(jnp.logical_or(i == 0, prev != e))
    def _():
        wgb[...] = wg_ref[0].astype(BF16)
        wub[...] = wu_ref[0].astype(BF16)
        wdb[...] = wd_ref[0].astype(BF16)

    @pl.when(i < nused_ref[0])
    def _():
        base = i * bm

        def gather(r, carry):
            src = pl.multiple_of(tok_ref[base + r] * nchunk, nchunk)
            xbuf[pl.ds(pl.multiple_of(r * nchunk, nchunk), nchunk), :] = h2p_ref[pl.ds(src, nchunk), :]
            return carry

        lax.fori_loop(0, bm, gather, 0)
        los, his = [], []
        for c in range(nchunk):
            lo, hi = _unpack_bf16_pairs(xbuf[pl.ds(c, bm, stride=nchunk), :])
            los.append(lo)
            his.append(hi)
        xrows = jnp.concatenate(los + his, axis=-1)
        g = _dot(xrows, wgb[...])
        u = _dot(xrows, wub[...])
        wcol = wcol_ref[...]
        wfull = jnp.concatenate([wcol] * (g.shape[1] // LANES), axis=-1)
        act = (g * _sigmoid(g) * u * wfull).astype(BF16)
        y = _dot(act, wdb[...])
        nout = dout // LANES
        for c in range(nout):
            y_ref[pl.ds(c, bm, stride=nout), :] = y[:, c * LANES:(c + 1) * LANES]

    @pl.when(i >= nused_ref[0])
    def _():
        y_ref[...] = jnp.zeros_like(y_ref)


def _moe(h2p, tok_slot, w_slot, blk_expert, n_used, w_gate, w_up, w_down, bm):
    ne, d, de = w_gate.shape
    nslot = tok_slot.shape[0]
    nblk = nslot // bm
    nchunk = d // 2 // LANES
    nout = d // LANES
    wcol = jnp.broadcast_to(w_slot[:, None], (nslot, LANES))
    grid_spec = pltpu.PrefetchScalarGridSpec(
        num_scalar_prefetch=3,
        grid=(nblk,),
        in_specs=[pl.BlockSpec(memory_space=pltpu.VMEM),
                  pl.BlockSpec((bm, LANES), lambda i, be, nu, tk: (i, 0)),
                  pl.BlockSpec((1, d, de), lambda i, be, nu, tk: (be[i], 0, 0)),
                  pl.BlockSpec((1, d, de), lambda i, be, nu, tk: (be[i], 0, 0)),
                  pl.BlockSpec((1, de, d), lambda i, be, nu, tk: (be[i], 0, 0))],
        out_specs=pl.BlockSpec((bm * nout, LANES), lambda i, be, nu, tk: (i, 0)),
        scratch_shapes=[pltpu.VMEM((d, de), BF16), pltpu.VMEM((d, de), BF16), pltpu.VMEM((de, d), BF16),
                        pltpu.VMEM((bm * nchunk, LANES), U32)],
    )
    return pl.pallas_call(
        _moe_kernel,
        out_shape=jax.ShapeDtypeStruct((nslot * nout, LANES), F32),
        grid_spec=grid_spec,
        compiler_params=pltpu.CompilerParams(dimension_semantics=("arbitrary",), vmem_limit_bytes=VMEM_LIMIT),
        name="moe",
    )(blk_expert, n_used, tok_slot, h2p, wcol, w_gate, w_up, w_down)


def _combine_kernel(alpha, pos_ref, y_hbm, x1_ref, mod_ref, wsg_ref, wsu_ref, wsd_ref, g_ref, b_ref, o_ref,
                    ybuf, acc, sem):
    i = pl.program_id(0)
    tt, d = x1_ref.shape
    nout = d // LANES
    rows = tt * nout

    def row_copy(t, k):
        src = pl.multiple_of(pos_ref[(i * tt + t) * TOP_K + k] * nout, nout)
        dst = pl.multiple_of(k * rows + t * nout, nout)
        return pltpu.make_async_copy(y_hbm.at[pl.ds(src, nout), :], ybuf.at[pl.ds(dst, nout), :], sem)

    def issue(t, carry):
        for k in range(TOP_K):
            row_copy(t, k).start()
        return carry

    lax.fori_loop(0, tt, issue, 0)

    x1 = x1_ref[...]
    h2 = (x1 * (1.0 + mod_ref[0, 4:5, :]) + mod_ref[0, 3:4, :]).astype(BF16)
    g = _dot(h2, wsg_ref[...])
    u = _dot(h2, wsu_ref[...])
    shared = _dot((g * _sigmoid(g) * u).astype(BF16), wsd_ref[...])

    pltpu.make_async_copy(y_hbm.at[pl.ds(0, TOP_K * rows), :], ybuf, sem).wait()
    total = ybuf[pl.ds(0, rows), :]
    for k in range(1, TOP_K):
        total = total + ybuf[pl.ds(k * rows, rows), :]
    acc[...] = total
    routed = jnp.concatenate([acc[pl.ds(c, tt, stride=nout), :] for c in range(nout)], axis=-1)
    z = alpha * x1 + mod_ref[0, 5:6, :] * (routed + shared)
    o_ref[...] = _layer_norm(z, g_ref[...], b_ref[...])


def _combine(pos, y_sorted, x1, mod, wsg_bf, wsu_bf, wsd_bf, ln_g, ln_b, alpha, seq):
    t, d = x1.shape
    ds_ = wsg_bf.shape[1]
    tt = COMBINE_TOKENS
    per_batch = seq // tt
    nout = d // LANES
    grid_spec = pltpu.PrefetchScalarGridSpec(
        num_scalar_prefetch=1,
        grid=(t // tt,),
        in_specs=[pl.BlockSpec(memory_space=pl.ANY),
                  pl.BlockSpec((tt, d), lambda i, p: (i, 0)),
                  pl.BlockSpec((1, N_MOD, d), lambda i, p: (i // per_batch, 0, 0)),
                  pl.BlockSpec((d, ds_), lambda i, p: (0, 0)),
                  pl.BlockSpec((d, ds_), lambda i, p: (0, 0)),
                  pl.BlockSpec((ds_, d), lambda i, p: (0, 0)),
                  pl.BlockSpec((1, d), lambda i, p: (0, 0)),
                  pl.BlockSpec((1, d), lambda i, p: (0, 0))],
        out_specs=pl.BlockSpec((tt, d), lambda i, p: (i, 0)),
        scratch_shapes=[pltpu.VMEM((TOP_K * tt * nout, LANES), F32), pltpu.VMEM((tt * nout, LANES), F32),
                        pltpu.SemaphoreType.DMA],
    )
    return pl.pallas_call(
        functools.partial(_combine_kernel, alpha),
        out_shape=jax.ShapeDtypeStruct((t, d), F32),
        grid_spec=grid_spec,
        compiler_params=pltpu.CompilerParams(dimension_semantics=("arbitrary",), vmem_limit_bytes=VMEM_LIMIT),
        name="combine",
    )(pos, y_sorted, x1, mod, wsg_bf, wsu_bf, wsd_bf, ln_g.reshape(1, d), ln_b.reshape(1, d))


def kernel(x, c, w_ada, b_ada, w_in, rel_bias, attn_gain, ret_gain, w_out, ln1_gain, ln1_bias, w_router, router_bias,
           w_gate, w_up, w_down, ws_gate, ws_up, ws_down, ln2_gain, ln2_bias):
    batch, seq, d = x.shape
    depth = w_ada.shape[0]
    alpha = (2.0 * depth) ** 0.25
    ne = w_router.shape[-1]
    xt = x.reshape(batch * seq, d)
    c_pad = jnp.zeros((8, d), F32).at[:batch].set(c)
    for l in range(depth):
        mod = _ada(c_pad, w_ada[l], b_ada[l])[:batch].reshape(batch, N_MOD, d)
        proj = _inproj(xt, mod, w_in[l], seq)
        mix = _mixer(proj, rel_bias[l], attn_gain[l], ret_gain[l], batch, seq)
        x1, h2p, scores = _outproj(mix, xt, mod, w_out[l].astype(BF16), ln1_gain[l], ln1_bias[l],
                                   w_router[l].astype(BF16), alpha, seq)
        idx, w = _route(scores, router_bias[l])
        dest, tok_slot, w_slot, blk_expert, n_used = _dispatch_plan(idx, w, ne, MOE_BLOCK)
        y_sorted = _moe(h2p, tok_slot, w_slot, blk_expert, n_used, w_gate[l], w_up[l], w_down[l], MOE_BLOCK)
        xt = _combine(dest.reshape(-1), y_sorted, x1, mod, ws_gate[l].astype(BF16), ws_up[l].astype(BF16),
                      ws_down[l].astype(BF16), ln2_gain[l], ln2_bias[l], alpha, seq)
    return xt.reshape(batch, seq, d)
```

```python
import functools

import jax
import jax.numpy as jnp
import numpy as np
from jax import lax
from jax.experimental import pallas as pl
from jax.experimental.pallas import tpu as pltpu

F32 = jnp.float32
BF16 = jnp.bfloat16
U32 = jnp.uint32
I32 = jnp.int32

CHUNK = 64
LEFT_CHUNKS = 8
MAX_REL = 256
ATTN_HEADS = 8
HEAD_DIM = 128
RET_HEADS = 8
RET_KEY_DIM = 64
RET_VALUE_DIM = 128
ROPE_BASE = 10000.0
N_GROUPS = 8
TOPK_GROUPS = 4
TOP_K = 8
ROUTED_SCALE = 2.5
EPS = 1e-5
N_MOD = 6
LANES = 128
SUBLANES = 8

SEQ_BLOCK = 256
ROUTE_TOKENS = 256
MOE_BLOCK = 128
COMBINE_TOKENS = 128
NEG_BIG = -1e30
VMEM_LIMIT = 56 * 1024 * 1024


def _sigmoid(v):
    return 1.0 / (1.0 + jnp.exp(-v))


def _dot(a, b):
    return jnp.dot(a, b, preferred_element_type=F32)


def _dot_t(a, b):
    return lax.dot_general(a, b, (((1,), (1,)), ((), ())), preferred_element_type=F32)


def _ada_kernel(c_ref, w_ref, b_ref, o_ref):
    c = c_ref[...]
    s = (c * _sigmoid(c)).astype(BF16)
    o_ref[...] = _dot(s, w_ref[...].astype(BF16)) + b_ref[...]


def _ada(c_pad, w_ada, b_ada):
    d, n = w_ada.shape
    tn = 1024
    return pl.pallas_call(
        _ada_kernel,
        out_shape=jax.ShapeDtypeStruct((8, n), F32),
        grid=(n // tn,),
        in_specs=[pl.BlockSpec((8, d), lambda j: (0, 0)),
                  pl.BlockSpec((d, tn), lambda j: (0, j)),
                  pl.BlockSpec((1, tn), lambda j: (0, j))],
        out_specs=pl.BlockSpec((8, tn), lambda j: (0, j)),
        compiler_params=pltpu.CompilerParams(dimension_semantics=("arbitrary",), vmem_limit_bytes=VMEM_LIMIT),
        name="ada",
    )(c_pad, w_ada, b_ada.reshape(1, n))


def _inproj_kernel(x_ref, mod_ref, w_ref, o_ref, wbf_ref):
    @pl.when(pl.program_id(1) == 0)
    def _():
        wbf_ref[...] = w_ref[...].astype(BF16)

    shift = mod_ref[0, 0:1, :]
    scale = mod_ref[0, 1:2, :]
    h = (x_ref[...] * (1.0 + scale) + shift).astype(BF16)
    o_ref[...] = _dot(h, wbf_ref[...]).astype(o_ref.dtype)


def _inproj(x2, mod, w_in, seq):
    t, d = x2.shape
    n = w_in.shape[1]
    tm, tn = 512, 1024
    per_batch = seq // tm
    return pl.pallas_call(
        _inproj_kernel,
        out_shape=jax.ShapeDtypeStruct((t, n), BF16),
        grid=(n // tn, t // tm),
        in_specs=[pl.BlockSpec((tm, d), lambda j, i: (i, 0)),
                  pl.BlockSpec((1, N_MOD, d), lambda j, i: (i // per_batch, 0, 0)),
                  pl.BlockSpec((d, tn), lambda j, i: (0, j))],
        out_specs=pl.BlockSpec((tm, tn), lambda j, i: (i, j)),
        scratch_shapes=[pltpu.VMEM((d, tn), BF16)],
        compiler_params=pltpu.CompilerParams(dimension_semantics=("arbitrary", "arbitrary"),
                                             vmem_limit_bytes=VMEM_LIMIT),
        name="inproj",
    )(x2, mod, w_in)


def _ret_decay_consts(blk):
    h = np.arange(RET_HEADS, dtype=np.float64)
    log_g = np.log(1.0 - 2.0 ** (-5.0 - h))
    n = np.arange(blk, dtype=np.float64)
    diff = n[:, None] - n[None, :]
    same = (n[:, None] // CHUNK) == (n[None, :] // CHUNK)
    later = (n[:, None] // CHUNK) > (n[None, :] // CHUNK)
    expo = np.where(same, np.abs(diff), diff)
    kscale = RET_KEY_DIM ** -0.5
    decay = np.where(same | later, np.exp(log_g[:, None, None] * expo[None]), 0.0) * kscale
    xi = np.exp(log_g[:, None] * (n[None, :] + 1.0))
    zeta = np.exp(log_g[:, None] * (blk - 1.0 - n[None, :])) * kscale
    g_blk = np.exp(log_g * blk)
    xi = np.broadcast_to(xi[:, :, None], (RET_HEADS, blk, LANES))
    zeta = np.broadcast_to(zeta[:, :, None], (RET_HEADS, blk, LANES))
    return (jnp.asarray(decay, F32), jnp.asarray(xi, F32), jnp.asarray(zeta, F32), [float(v) for v in g_blk])


def _mixer_kernel(g_blk, qa, ka0, ka1, ka2, va0, va1, va2, qb, kb, vb, gb, bias, cos, sin, decay, xi, zeta,
                  again, rgain, out, state, oa):
    i = pl.program_id(1)
    blk = qa.shape[0]

    @pl.when(i == 0)
    def _():
        state[...] = jnp.zeros_like(state)

    ok0 = i >= 2
    ok1 = i >= 1
    scale = HEAD_DIM ** -0.5
    ssq = jnp.zeros((blk, 1), F32)
    for h in range(ATTN_HEADS):
        sl = slice(h * HEAD_DIM, (h + 1) * HEAD_DIM)
        q = qa[:, sl]
        s0 = jnp.where(ok0, _dot_t(q, ka0[:, sl]) * scale + bias[h, :, 0:blk], NEG_BIG)
        s1 = jnp.where(ok1, _dot_t(q, ka1[:, sl]) * scale + bias[h, :, blk:2 * blk], NEG_BIG)
        s2 = _dot_t(q, ka2[:, sl]) * scale + bias[h, :, 2 * blk:3 * blk]
        m = jnp.maximum(jnp.maximum(jnp.max(s0, axis=-1, keepdims=True), jnp.max(s1, axis=-1, keepdims=True)),
                        jnp.max(s2, axis=-1, keepdims=True))
        p0 = jnp.exp(s0 - m)
        p1 = jnp.exp(s1 - m)
        p2 = jnp.exp(s2 - m)
        denom = (jnp.sum(p0, axis=-1, keepdims=True) + jnp.sum(p1, axis=-1, keepdims=True)
                 + jnp.sum(p2, axis=-1, keepdims=True))
        o = _dot(p0.astype(BF16), va0[:, sl]) + _dot(p1.astype(BF16), va1[:, sl]) + _dot(p2.astype(BF16), va2[:, sl])
        o = o * (1.0 / denom)
        oa[:, sl] = o
        ssq = ssq + jnp.sum(o * o, axis=-1, keepdims=True)
    width_a = ATTN_HEADS * HEAD_DIM
    inv_rms = lax.rsqrt(ssq * (1.0 / width_a) + EPS)
    out[:, 0:width_a] = (oa[...] * inv_rms * again[...]).astype(out.dtype)

    cosv = cos[...]
    sinv = sin[...]
    lane = lax.broadcasted_iota(jnp.int32, (blk, LANES), 1)
    first_half = (lane % RET_KEY_DIM) < (RET_KEY_DIM // 2)
    low_head = lane < RET_KEY_DIM

    def rope(v):
        rot = jnp.where(first_half, pltpu.roll(v, LANES - RET_KEY_DIM // 2, 1), pltpu.roll(v, RET_KEY_DIM // 2, 1))
        return v * cosv + rot * sinv

    for j in range(RET_HEADS // 2):
        sl2 = slice(j * LANES, (j + 1) * LANES)
        qr = rope(qb[:, sl2].astype(F32))
        k2 = rope(kb[:, sl2].astype(F32)).astype(BF16)
        for hh in range(2):
            h = 2 * j + hh
            slv = slice(h * RET_VALUE_DIM, (h + 1) * RET_VALUE_DIM)
            qm = jnp.where(low_head if hh == 0 else jnp.logical_not(low_head), qr, 0.0).astype(BF16)
            v = vb[:, slv]
            st = state[h]
            scores = _dot_t(qm, k2) * decay[h]
            ret = _dot(scores.astype(BF16), v) + _dot(qm, st.astype(BF16)) * xi[h]
            zv = (v.astype(F32) * zeta[h]).astype(BF16)
            upd = lax.dot_general(k2, zv, (((0,), (0,)), ((), ())), preferred_element_type=F32)
            state[h] = g_blk[h] * st + upd
            mu = jnp.mean(ret, axis=-1, keepdims=True)
            cen = ret - mu
            var = jnp.mean(cen * cen, axis=-1, keepdims=True)
            gate = gb[:, slv].astype(F32)
            yb = gate * _sigmoid(gate) * (cen * lax.rsqrt(var + EPS)) * rgain[:, slv]
            out[:, width_a + h * RET_VALUE_DIM: width_a + (h + 1) * RET_VALUE_DIM] = yb.astype(out.dtype)


def _attn_bias_table(rel_bias, blk):
    r = np.arange(blk)[:, None]
    c = np.arange(3 * blk)[None, :]
    dist = r + 2 * blk - c
    rel_idx = np.clip(dist, -(CHUNK - 1), MAX_REL) + (CHUNK - 1)
    qc = r // CHUNK + (2 * blk) // CHUNK
    kc = c // CHUNK
    in_band = (kc <= qc) & (kc >= qc - LEFT_CHUNKS)
    tab = rel_bias[:, jnp.asarray(rel_idx)].astype(F32)
    return jnp.where(jnp.asarray(in_band)[None], tab, NEG_BIG)


def _rope_tables(seq):
    half = RET_KEY_DIM // 2
    inv = ROPE_BASE ** (-jnp.arange(half, dtype=F32) / half)
    ang = jnp.arange(seq, dtype=F32)[:, None] * inv[None, :]
    cos, sin = jnp.cos(ang), jnp.sin(ang)
    reps = LANES // RET_KEY_DIM
    cos_t = jnp.tile(jnp.concatenate([cos, cos], axis=-1), (1, reps))
    sin_t = jnp.tile(jnp.concatenate([-sin, sin], axis=-1), (1, reps))
    return cos_t, sin_t


def _mixer(proj, rel_bias, attn_gain, ret_gain, batch, seq):
    t = proj.shape[0]
    blk = SEQ_BLOCK
    assert 2 * blk == LEFT_CHUNKS * CHUNK and seq % blk == 0
    nb = seq // blk
    wa = ATTN_HEADS * HEAD_DIM
    wqk = RET_HEADS * RET_KEY_DIM
    wv = RET_HEADS * RET_VALUE_DIM
    assert wa == wv == 2 * wqk
    bias = _attn_bias_table(rel_bias, blk)
    cos_t, sin_t = _rope_tables(seq)
    decay, xi, zeta, g_blk = _ret_decay_consts(blk)

    def row(b, i):
        return b * nb + i

    def kspec(back, col):
        return pl.BlockSpec((blk, wa), lambda b, i: (row(b, jnp.maximum(i - back, 0)), col))

    const3 = lambda b, i: (0, 0, 0)
    in_specs = [
        pl.BlockSpec((blk, wa), lambda b, i: (row(b, i), 0)),
        kspec(2, 1), kspec(1, 1), kspec(0, 1),
        kspec(2, 2), kspec(1, 2), kspec(0, 2),
        pl.BlockSpec((blk, wqk), lambda b, i: (row(b, i), 3 * wa // wqk)),
        pl.BlockSpec((blk, wqk), lambda b, i: (row(b, i), 3 * wa // wqk + 1)),
        pl.BlockSpec((blk, wv), lambda b, i: (row(b, i), (3 * wa + 2 * wqk) // wv)),
        pl.BlockSpec((blk, wv), lambda b, i: (row(b, i), (3 * wa + 2 * wqk) // wv + 1)),
        pl.BlockSpec((ATTN_HEADS, blk, 3 * blk), const3),
        pl.BlockSpec((blk, LANES), lambda b, i: (i, 0)),
        pl.BlockSpec((blk, LANES), lambda b, i: (i, 0)),
        pl.BlockSpec((RET_HEADS, blk, blk), const3),
        pl.BlockSpec((RET_HEADS, blk, LANES), const3),
        pl.BlockSpec((RET_HEADS, blk, LANES), const3),
        pl.BlockSpec((1, wa), lambda b, i: (0, 0)),
        pl.BlockSpec((1, wv), lambda b, i: (0, 0)),
    ]
    return pl.pallas_call(
        functools.partial(_mixer_kernel, g_blk),
        out_shape=jax.ShapeDtypeStruct((t, wa + wv), BF16),
        grid=(batch, nb),
        in_specs=in_specs,
        out_specs=pl.BlockSpec((blk, wa + wv), lambda b, i: (row(b, i), 0)),
        scratch_shapes=[pltpu.VMEM((RET_HEADS, LANES, RET_VALUE_DIM), F32), pltpu.VMEM((blk, wa), F32)],
        compiler_params=pltpu.CompilerParams(dimension_semantics=("arbitrary", "arbitrary"),
                                             vmem_limit_bytes=VMEM_LIMIT),
        name="mixer",
    )(proj, proj, proj, proj, proj, proj, proj, proj, proj, proj, proj, bias, cos_t, sin_t, decay, xi, zeta,
      attn_gain.reshape(1, wa), ret_gain.reshape(1, wv))


def _layer_norm(z, gain, bias):
    mu = jnp.mean(z, axis=-1, keepdims=True)
    cen = z - mu
    var = jnp.mean(cen * cen, axis=-1, keepdims=True)
    return cen * lax.rsqrt(var + EPS) * gain + bias


def _pack_bf16_pairs(lo, hi):
    return pltpu.pack_elementwise([lo, hi], packed_dtype=BF16)


def _unpack_pairs_f32(words):
    lo = pltpu.unpack_elementwise(words, index=0, packed_dtype=BF16, unpacked_dtype=F32)
    hi = pltpu.unpack_elementwise(words, index=1, packed_dtype=BF16, unpacked_dtype=F32)
    return lo, hi


def _store_packed_rows(ref, val):
    tm, d = val.shape
    half = d // 2
    nchunk = half // LANES
    for c in range(nchunk):
        words = _pack_bf16_pairs(val[:, c * LANES:(c + 1) * LANES], val[:, half + c * LANES: half + (c + 1) * LANES])
        ref[pl.ds(c, tm, stride=nchunk), :] = words


def _route_tile(scores_t, bias_col):
    ne, tm = scores_t.shape
    per_group = ne // N_GROUPS
    biased = scores_t + bias_col
    neg_inf = -jnp.inf
    iota_g = lax.broadcasted_iota(I32, (per_group, tm), 0)
    gscore = []
    for g in range(N_GROUPS):
        b = biased[g * per_group:(g + 1) * per_group, :]
        m1 = jnp.max(b, axis=0, keepdims=True)
        first = jnp.min(jnp.where(b == m1, iota_g, per_group), axis=0, keepdims=True)
        m2 = jnp.max(jnp.where(iota_g == first, neg_inf, b), axis=0, keepdims=True)
        gscore.append(m1 + m2)
    masked_parts = []
    for g in range(N_GROUPS):
        beaten = jnp.zeros((1, tm), I32)
        for o in range(N_GROUPS):
            if o == g:
                continue
            wins = (gscore[o] >= gscore[g]) if o < g else (gscore[o] > gscore[g])
            beaten = beaten + wins.astype(I32)
        keep = beaten < TOPK_GROUPS
        masked_parts.append(jnp.where(keep, biased[g * per_group:(g + 1) * per_group, :], neg_inf))
    masked = jnp.concatenate(masked_parts, axis=0)
    eiota = lax.broadcasted_iota(I32, (ne, tm), 0)
    ids, vals = [], []
    for _ in range(TOP_K):
        m = jnp.max(masked, axis=0, keepdims=True)
        first = jnp.min(jnp.where(masked == m, eiota, ne), axis=0, keepdims=True)
        hit = eiota == first
        vals.append(jnp.sum(jnp.where(hit, scores_t, 0.0), axis=0, keepdims=True))
        ids.append(first)
        masked = jnp.where(hit, neg_inf, masked)
    return ids, vals


def _outproj_kernel(alpha, mix_ref, x_ref, mod_ref, wout_ref, g_ref, b_ref, wrt_ref, rb_ref, tri_ref, ones_ref,
                    x1_ref, h2p_ref, idx_ref, w_ref, rank_ref, cnt_ref, carry):
    i = pl.program_id(0)
    tm, d = x_ref.shape

    @pl.when(i == 0)
    def _():
        carry[...] = jnp.zeros_like(carry)

    y = _dot(mix_ref[...], wout_ref[...])
    gate1 = mod_ref[0, 2:3, :]
    x1 = _layer_norm(alpha * x_ref[...] + gate1 * y, g_ref[...], b_ref[...])
    x1_ref[...] = x1
    h2 = x1 * (1.0 + mod_ref[0, 4:5, :]) + mod_ref[0, 3:4, :]
    _store_packed_rows(h2p_ref, h2)

    scores_t = _sigmoid(_dot_t(wrt_ref[...], h2.astype(BF16)))
    ids, vals = _route_tile(scores_t, rb_ref[:, 0:1])
    ne = scores_t.shape[0]
    eiota = lax.broadcasted_iota(I32, (ne, tm), 0)
    chosen = jnp.zeros((ne, tm), F32)
    for k in range(TOP_K):
        chosen = chosen + (eiota == ids[k]).astype(F32)
    chosen_bf = chosen.astype(BF16)
    before = _dot(chosen_bf, tri_ref[...]) - chosen + jnp.concatenate([carry[...]] * (tm // LANES), axis=-1)
    ranks = [jnp.sum(jnp.where(eiota == ids[k], before, 0.0), axis=0, keepdims=True) for k in range(TOP_K)]
    carry[...] = carry[...] + _dot(chosen_bf, ones_ref[...])
    cnt_ref[...] = carry[...]
    wsel = jnp.concatenate(vals, axis=0)
    w_ref[...] = wsel / jnp.sum(wsel, axis=0, keepdims=True) * ROUTED_SCALE
    idx_ref[...] = jnp.concatenate(ids, axis=0)
    rank_ref[...] = jnp.concatenate(ranks, axis=0).astype(I32)


def _outproj(mix, x2, mod, w_out_bf, ln_g, ln_b, w_router_t_bf, router_bias, alpha, seq):
    t, d = x2.shape
    ne = w_router_t_bf.shape[0]
    tm = ROUTE_TOKENS
    per_batch = seq // tm
    nchunk = d // 2 // LANES
    tri = jnp.asarray(np.triu(np.ones((tm, tm), np.float32)), BF16)
    ones = jnp.ones((tm, LANES), BF16)
    rb = jnp.broadcast_to(router_bias.astype(F32)[:, None], (ne, LANES))
    c2 = lambda i: (0, 0)
    return pl.pallas_call(
        functools.partial(_outproj_kernel, alpha),
        out_shape=(jax.ShapeDtypeStruct((t, d), F32),
                   jax.ShapeDtypeStruct((t * nchunk, LANES), U32),
                   jax.ShapeDtypeStruct((TOP_K, t), I32),
                   jax.ShapeDtypeStruct((TOP_K, t), F32),
                   jax.ShapeDtypeStruct((TOP_K, t), I32),
                   jax.ShapeDtypeStruct((ne, LANES), F32)),
        grid=(t // tm,),
        in_specs=[pl.BlockSpec((tm, d), lambda i: (i, 0)),
                  pl.BlockSpec((tm, d), lambda i: (i, 0)),
                  pl.BlockSpec((1, N_MOD, d), lambda i: (i // per_batch, 0, 0)),
                  pl.BlockSpec((d, d), c2),
                  pl.BlockSpec((1, d), c2),
                  pl.BlockSpec((1, d), c2),
                  pl.BlockSpec((ne, d), c2),
                  pl.BlockSpec((ne, LANES), c2),
                  pl.BlockSpec((tm, tm), c2),
                  pl.BlockSpec((tm, LANES), c2)],
        out_specs=(pl.BlockSpec((tm, d), lambda i: (i, 0)),
                   pl.BlockSpec((tm * nchunk, LANES), lambda i: (i, 0)),
                   pl.BlockSpec((TOP_K, tm), lambda i: (0, i)),
                   pl.BlockSpec((TOP_K, tm), lambda i: (0, i)),
                   pl.BlockSpec((TOP_K, tm), lambda i: (0, i)),
                   pl.BlockSpec((ne, LANES), c2)),
        scratch_shapes=[pltpu.VMEM((ne, LANES), F32)],
        compiler_params=pltpu.CompilerParams(dimension_semantics=("arbitrary",), vmem_limit_bytes=VMEM_LIMIT),
        name="outproj",
    )(mix, x2, mod, w_out_bf, ln_g.reshape(1, d), ln_b.reshape(1, d), w_router_t_bf, rb, tri, ones)


def _slots_kernel(idx_ref, rank_ref, pstart_ref, dest_ref):
    ne = pstart_ref.shape[0]
    tm = idx_ref.shape[1]
    eiota = lax.broadcasted_iota(I32, (ne, tm), 0)
    start = jnp.concatenate([pstart_ref[...]] * (tm // LANES), axis=-1)
    rows = []
    for k in range(TOP_K):
        rows.append(jnp.sum(jnp.where(eiota == idx_ref[k:k + 1, :], start, 0), axis=0, keepdims=True))
    dest_ref[...] = jnp.concatenate(rows, axis=0) + rank_ref[...]


def _slots(idx_t, rank_t, pstart):
    k, t = idx_t.shape
    ne = pstart.shape[0]
    tm = 512
    return pl.pallas_call(
        _slots_kernel,
        out_shape=jax.ShapeDtypeStruct((k, t), I32),
        grid=(t // tm,),
        in_specs=[pl.BlockSpec((k, tm), lambda i: (0, i)),
                  pl.BlockSpec((k, tm), lambda i: (0, i)),
                  pl.BlockSpec((ne, LANES), lambda i: (0, 0))],
        out_specs=pl.BlockSpec((k, tm), lambda i: (0, i)),
        compiler_params=pltpu.CompilerParams(dimension_semantics=("arbitrary",)),
        name="slots",
    )(idx_t, rank_t, jnp.broadcast_to(pstart[:, None], (ne, LANES)))


def _expert_layout(counts, bm, nblk):
    ne = counts.shape[0]
    padded = (counts + bm - 1) // bm * bm
    pend = jnp.cumsum(padded)
    pstart = pend - padded
    blk_start = jnp.arange(nblk, dtype=I32) * bm
    blk_expert = jnp.minimum(jnp.sum((pend[None, :] <= blk_start[:, None]).astype(I32), axis=1), ne - 1)
    onehot = (blk_expert[:, None] == jnp.arange(ne, dtype=I32)[None, :]).astype(I32)
    seg_start = jnp.sum(onehot * pstart[None, :], axis=1)
    seg_count = jnp.sum(onehot * counts[None, :], axis=1)
    nvalid = jnp.clip(seg_count - (blk_start - seg_start), 0, bm).astype(I32)
    return pstart.astype(I32), blk_expert.astype(I32), nvalid


def _moe_kernel(be_ref, nv_ref, dest_ref, h2p_ref, wg_ref, wu_ref, wd_ref, y_ref, wgb, wub, wdb, xbuf, tok_ref):
    i = pl.program_id(0)
    nchunk = SUBLANES
    bm = xbuf.shape[0] // nchunk
    e = be_ref[i]
    prev = be_ref[jnp.maximum(i - 1, 0)]
    nvalid = nv_ref[i]

    @pl.when(i == 0)
    def _():
        ntok = dest_ref.shape[0] // TOP_K
        unroll = 4

        def invert(j, carry):
            for u in range(unroll):
                t = j * unroll + u
                for k in range(TOP_K):
                    tok_ref[dest_ref[k * ntok + t]] = t
            return carry

        lax.fori_loop(0, ntok // unroll, invert, 0)

    @pl.when(jnp.logical_or(i == 0, prev != e))
    def _():
        wgb[...] = wg_ref[0].astype(BF16)
        wub[...] = wu_ref[0].astype(BF16)
        wdb[...] = wd_ref[0].astype(BF16)

    @pl.when(nvalid > 0)
    def _():
        base = i * bm
        last = nvalid - 1

        def gather(j, carry):
            for u in range(SUBLANES):
                r = j * SUBLANES + u
                src = pl.multiple_of(tok_ref[base + jnp.minimum(r, last)] * nchunk, nchunk)
                xbuf[pl.ds(pl.multiple_of(r * nchunk, nchunk), nchunk), :] = h2p_ref[pl.ds(src, nchunk), :]
            return carry

        lax.fori_loop(0, bm // SUBLANES, gather, 0)
        los, his = [], []
        for c in range(nchunk):
            lo, hi = _unpack_pairs_f32(xbuf[pl.ds(c, bm, stride=nchunk), :])
            los.append(lo.astype(BF16))
            his.append(hi.astype(BF16))
        xrows = jnp.concatenate(los + his, axis=-1)
        g = _dot(xrows, wgb[...])
        u = _dot(xrows, wub[...])
        act = (g * _sigmoid(g) * u).astype(BF16)
        _store_packed_rows(y_ref, _dot(act, wdb[...]))

    @pl.when(nvalid == 0)
    def _():
        zeros = jnp.zeros((bm, LANES), F32)
        y_ref[...] = jnp.concatenate([_pack_bf16_pairs(zeros, zeros)] * nchunk, axis=0)


def _moe(h2p, dest, blk_expert, nvalid, w_gate, w_up, w_down, bm):
    ne, d, de = w_gate.shape
    nblk = blk_expert.shape[0]
    nslot = nblk * bm
    nchunk = d // 2 // LANES
    assert nchunk == SUBLANES
    grid_spec = pltpu.PrefetchScalarGridSpec(
        num_scalar_prefetch=3,
        grid=(nblk,),
        in_specs=[pl.BlockSpec(memory_space=pltpu.VMEM),
                  pl.BlockSpec((1, d, de), lambda i, be, nv, tk: (be[i], 0, 0)),
                  pl.BlockSpec((1, d, de), lambda i, be, nv, tk: (be[i], 0, 0)),
                  pl.BlockSpec((1, de, d), lambda i, be, nv, tk: (be[i], 0, 0))],
        out_specs=pl.BlockSpec((bm * nchunk, LANES), lambda i, be, nv, tk: (i, 0)),
        scratch_shapes=[pltpu.VMEM((d, de), BF16), pltpu.VMEM((d, de), BF16), pltpu.VMEM((de, d), BF16),
                        pltpu.VMEM((bm * nchunk, LANES), U32), pltpu.SMEM((nslot,), I32)],
    )
    return pl.pallas_call(
        _moe_kernel,
        out_shape=jax.ShapeDtypeStruct((nslot * nchunk, LANES), U32),
        grid_spec=grid_spec,
        compiler_params=pltpu.CompilerParams(dimension_semantics=("arbitrary",), vmem_limit_bytes=VMEM_LIMIT),
        name="moe",
    )(blk_expert, nvalid, dest, h2p, w_gate, w_up, w_down)


def _combine_kernel(alpha, ntok, pos_ref, y_hbm, w_ref, x1_ref, mod_ref, wsg_ref, wsu_ref, wsd_ref, g_ref, b_ref,
                    o_ref, ybuf, sem):
    i = pl.program_id(0)
    tt, d = x1_ref.shape
    nchunk = SUBLANES
    rows = tt * nchunk

    def row_copy(t, k):
        src = pl.multiple_of(pos_ref[k * ntok + i * tt + t] * nchunk, nchunk)
        dst = pl.multiple_of(k * rows + t * nchunk, nchunk)
        return pltpu.make_async_copy(y_hbm.at[pl.ds(src, nchunk), :], ybuf.at[pl.ds(dst, nchunk), :], sem)

    def issue(t, carry):
        for k in range(TOP_K):
            row_copy(t, k).start()
        return carry

    lax.fori_loop(0, tt, issue, 0)

    x1 = x1_ref[...]
    h2 = (x1 * (1.0 + mod_ref[0, 4:5, :]) + mod_ref[0, 3:4, :]).astype(BF16)
    g = _dot(h2, wsg_ref[...])
    u = _dot(h2, wsu_ref[...])
    shared = _dot((g * _sigmoid(g) * u).astype(BF16), wsd_ref[...])

    pltpu.make_async_copy(y_hbm.at[pl.ds(0, TOP_K * rows), :], ybuf, sem).wait()
    los = [None] * nchunk
    his = [None] * nchunk
    for k in range(TOP_K):
        wk = w_ref[:, k:k + 1]
        for c in range(nchunk):
            lo, hi = _unpack_pairs_f32(ybuf[pl.ds(k * rows + c, tt, stride=nchunk), :])
            los[c] = lo * wk if k == 0 else los[c] + lo * wk
            his[c] = hi * wk if k == 0 else his[c] + hi * wk
    routed = jnp.concatenate(los + his, axis=-1)
    z = alpha * x1 + mod_ref[0, 5:6, :] * (routed + shared)
    o_ref[...] = _layer_norm(z, g_ref[...], b_ref[...])


def _combine(pos, y_sorted, w_tok, x1, mod, wsg_bf, wsu_bf, wsd_bf, ln_g, ln_b, alpha, seq):
    t, d = x1.shape
    ds_ = wsg_bf.shape[1]
    tt = COMBINE_TOKENS
    per_batch = seq // tt
    nchunk = d // 2 // LANES
    assert nchunk == SUBLANES
    grid_spec = pltpu.PrefetchScalarGridSpec(
        num_scalar_prefetch=1,
        grid=(t // tt,),
        in_specs=[pl.BlockSpec(memory_space=pl.ANY),
                  pl.BlockSpec((tt, TOP_K), lambda i, p: (i, 0)),
                  pl.BlockSpec((tt, d), lambda i, p: (i, 0)),
                  pl.BlockSpec((1, N_MOD, d), lambda i, p: (i // per_batch, 0, 0)),
                  pl.BlockSpec((d, ds_), lambda i, p: (0, 0)),
                  pl.BlockSpec((d, ds_), lambda i, p: (0, 0)),
                  pl.BlockSpec((ds_, d), lambda i, p: (0, 0)),
                  pl.BlockSpec((1, d), lambda i, p: (0, 0)),
                  pl.BlockSpec((1, d), lambda i, p: (0, 0))],
        out_specs=pl.BlockSpec((tt, d), lambda i, p: (i, 0)),
        scratch_shapes=[pltpu.VMEM((TOP_K * tt * nchunk, LANES), U32), pltpu.SemaphoreType.DMA],
    )
    return pl.pallas_call(
        functools.partial(_combine_kernel, alpha, t),
        out_shape=jax.ShapeDtypeStruct((t, d), F32),
        grid_spec=grid_spec,
        compiler_params=pltpu.CompilerParams(dimension_semantics=("arbitrary",), vmem_limit_bytes=VMEM_LIMIT),
        name="combine",
    )(pos, y_sorted, w_tok, x1, mod, wsg_bf, wsu_bf, wsd_bf, ln_g.reshape(1, d), ln_b.reshape(1, d))


def kernel(x, c, w_ada, b_ada, w_in, rel_bias, attn_gain, ret_gain, w_out, ln1_gain, ln1_bias, w_router, router_bias,
           w_gate, w_up, w_down, ws_gate, ws_up, ws_down, ln2_gain, ln2_bias):
    batch, seq, d = x.shape
    depth = w_ada.shape[0]
    alpha = (2.0 * depth) ** 0.25
    ne = w_router.shape[-1]
    t = batch * seq
    bm = MOE_BLOCK
    nslot = t * TOP_K + ne * bm
    xt = x.reshape(t, d)
    c_pad = jnp.zeros((8, d), F32).at[:batch].set(c)
    for l in range(depth):
        mod = _ada(c_pad, w_ada[l], b_ada[l])[:batch].reshape(batch, N_MOD, d)
        proj = _inproj(xt, mod, w_in[l], seq)
        mix = _mixer(proj, rel_bias[l], attn_gain[l], ret_gain[l], batch, seq)
        x1, h2p, idx_t, w_t, rank_t, cnt = _outproj(mix, xt, mod, w_out[l].astype(BF16), ln1_gain[l], ln1_bias[l],
                                                    w_router[l].T.astype(BF16), router_bias[l], alpha, seq)
        pstart, blk_expert, nvalid = _expert_layout(cnt[:, 0].astype(I32), bm, nslot // bm)
        dest = _slots(idx_t, rank_t, pstart).reshape(-1)
        y_sorted = _moe(h2p, dest, blk_expert, nvalid, w_gate[l], w_up[l], w_down[l], bm)
        xt = _combine(dest, y_sorted, w_t.T, x1, mod, ws_gate[l].astype(BF16), ws_up[l].astype(BF16),
                      ws_down[l].astype(BF16), ln2_gain[l], ln2_bias[l], alpha, seq)
    return xt.reshape(batch, seq, d)
```

```python
import functools

import jax
import jax.numpy as jnp
import numpy as np
from jax import lax
from jax.experimental import pallas as pl
from jax.experimental.pallas import tpu as pltpu

F32 = jnp.float32
BF16 = jnp.bfloat16
U32 = jnp.uint32
I32 = jnp.int32

CHUNK = 64
LEFT_CHUNKS = 8
MAX_REL = 256
ATTN_HEADS = 8
HEAD_DIM = 128
RET_HEADS = 8
RET_KEY_DIM = 64
RET_VALUE_DIM = 128
ROPE_BASE = 10000.0
N_GROUPS = 8
TOPK_GROUPS = 4
TOP_K = 8
ROUTED_SCALE = 2.5
EPS = 1e-5
N_MOD = 6
LANES = 128
SUBLANES = 8

SEQ_BLOCK = 256
ROUTE_TOKENS = 256
MOE_BLOCK = 128
COMBINE_TOKENS = 128
NEG_BIG = -1e30
VMEM_LIMIT = 56 * 1024 * 1024


def _sigmoid(v):
    return 1.0 / (1.0 + jnp.exp(-v))


def _dot(a, b):
    return jnp.dot(a, b, preferred_element_type=F32)


def _dot_t(a, b):
    return lax.dot_general(a, b, (((1,), (1,)), ((), ())), preferred_element_type=F32)


def _ada_kernel(c_ref, w_ref, b_ref, o_ref):
    c = c_ref[...]
    s = (c * _sigmoid(c)).astype(BF16)
    o_ref[...] = _dot(s, w_ref[...].astype(BF16)) + b_ref[...]


def _ada(c_pad, w_ada, b_ada):
    d, n = w_ada.shape
    tn = 1024
    return pl.pallas_call(
        _ada_kernel,
        out_shape=jax.ShapeDtypeStruct((8, n), F32),
        grid=(n // tn,),
        in_specs=[pl.BlockSpec((8, d), lambda j: (0, 0)),
                  pl.BlockSpec((d, tn), lambda j: (0, j)),
                  pl.BlockSpec((1, tn), lambda j: (0, j))],
        out_specs=pl.BlockSpec((8, tn), lambda j: (0, j)),
        compiler_params=pltpu.CompilerParams(dimension_semantics=("arbitrary",), vmem_limit_bytes=VMEM_LIMIT),
        name="ada",
    )(c_pad, w_ada, b_ada.reshape(1, n))


def _inproj_kernel(x_ref, mod_ref, w_ref, o_ref, wbf_ref):
    @pl.when(pl.program_id(1) == 0)
    def _():
        wbf_ref[...] = w_ref[...].astype(BF16)

    shift = mod_ref[0, 0:1, :]
    scale = mod_ref[0, 1:2, :]
    h = (x_ref[...] * (1.0 + scale) + shift).astype(BF16)
    o_ref[...] = _dot(h, wbf_ref[...]).astype(o_ref.dtype)


def _inproj(x2, mod, w_in, seq):
    t, d = x2.shape
    n = w_in.shape[1]
    tm, tn = 512, 1024
    per_batch = seq // tm
    return pl.pallas_call(
        _inproj_kernel,
        out_shape=jax.ShapeDtypeStruct((t, n), BF16),
        grid=(n // tn, t // tm),
        in_specs=[pl.BlockSpec((tm, d), lambda j, i: (i, 0)),
                  pl.BlockSpec((1, N_MOD, d), lambda j, i: (i // per_batch, 0, 0)),
                  pl.BlockSpec((d, tn), lambda j, i: (0, j))],
        out_specs=pl.BlockSpec((tm, tn), lambda j, i: (i, j)),
        scratch_shapes=[pltpu.VMEM((d, tn), BF16)],
        compiler_params=pltpu.CompilerParams(dimension_semantics=("arbitrary", "arbitrary"),
                                             vmem_limit_bytes=VMEM_LIMIT),
        name="inproj",
    )(x2, mod, w_in)


def _ret_decay_consts(blk):
    h = np.arange(RET_HEADS, dtype=np.float64)
    log_g = np.log(1.0 - 2.0 ** (-5.0 - h))
    n = np.arange(blk, dtype=np.float64)
    diff = n[:, None] - n[None, :]
    same = (n[:, None] // CHUNK) == (n[None, :] // CHUNK)
    later = (n[:, None] // CHUNK) > (n[None, :] // CHUNK)
    expo = np.where(same, np.abs(diff), diff)
    kscale = RET_KEY_DIM ** -0.5
    decay = np.where(same | later, np.exp(log_g[:, None, None] * expo[None]), 0.0) * kscale
    xi = np.exp(log_g[:, None] * (n[None, :] + 1.0))
    zeta = np.exp(log_g[:, None] * (blk - 1.0 - n[None, :])) * kscale
    g_blk = np.exp(log_g * blk)
    xi = np.broadcast_to(xi[:, :, None], (RET_HEADS, blk, LANES))
    zeta = np.broadcast_to(zeta[:, :, None], (RET_HEADS, blk, LANES))
    return (jnp.asarray(decay, F32), jnp.asarray(xi, F32), jnp.asarray(zeta, F32), [float(v) for v in g_blk])


def _mixer_kernel(g_blk, qa, ka0, ka1, ka2, va0, va1, va2, qb, kb, vb, gb, bias, cos, sin, decay, xi, zeta,
                  again, rgain, out, state, oa):
    i = pl.program_id(1)
    blk = qa.shape[0]

    @pl.when(i == 0)
    def _():
        state[...] = jnp.zeros_like(state)

    ok0 = i >= 2
    ok1 = i >= 1
    scale = HEAD_DIM ** -0.5
    ssq = jnp.zeros((blk, 1), F32)
    for h in range(ATTN_HEADS):
        sl = slice(h * HEAD_DIM, (h + 1) * HEAD_DIM)
        q = qa[:, sl]
        s0 = jnp.where(ok0, _dot_t(q, ka0[:, sl]) * scale + bias[h, :, 0:blk], NEG_BIG)
        s1 = jnp.where(ok1, _dot_t(q, ka1[:, sl]) * scale + bias[h, :, blk:2 * blk], NEG_BIG)
        s2 = _dot_t(q, ka2[:, sl]) * scale + bias[h, :, 2 * blk:3 * blk]
        m = jnp.maximum(jnp.maximum(jnp.max(s0, axis=-1, keepdims=True), jnp.max(s1, axis=-1, keepdims=True)),
                        jnp.max(s2, axis=-1, keepdims=True))
        p0 = jnp.exp(s0 - m)
        p1 = jnp.exp(s1 - m)
        p2 = jnp.exp(s2 - m)
        denom = (jnp.sum(p0, axis=-1, keepdims=True) + jnp.sum(p1, axis=-1, keepdims=True)
                 + jnp.sum(p2, axis=-1, keepdims=True))
        o = _dot(p0.astype(BF16), va0[:, sl]) + _dot(p1.astype(BF16), va1[:, sl]) + _dot(p2.astype(BF16), va2[:, sl])
        o = o * (1.0 / denom)
        oa[:, sl] = o
        ssq = ssq + jnp.sum(o * o, axis=-1, keepdims=True)
    width_a = ATTN_HEADS * HEAD_DIM
    inv_rms = lax.rsqrt(ssq * (1.0 / width_a) + EPS)
    out[:, 0:width_a] = (oa[...] * inv_rms * again[...]).astype(out.dtype)

    cosv = cos[...]
    sinv = sin[...]
    lane = lax.broadcasted_iota(jnp.int32, (blk, LANES), 1)
    first_half = (lane % RET_KEY_DIM) < (RET_KEY_DIM // 2)
    low_head = lane < RET_KEY_DIM

    def rope(v):
        rot = jnp.where(first_half, pltpu.roll(v, LANES - RET_KEY_DIM // 2, 1), pltpu.roll(v, RET_KEY_DIM // 2, 1))
        return v * cosv + rot * sinv

    for j in range(RET_HEADS // 2):
        sl2 = slice(j * LANES, (j + 1) * LANES)
        qr = rope(qb[:, sl2].astype(F32))
        k2 = rope(kb[:, sl2].astype(F32)).astype(BF16)
        for hh in range(2):
            h = 2 * j + hh
            slv = slice(h * RET_VALUE_DIM, (h + 1) * RET_VALUE_DIM)
            qm = jnp.where(low_head if hh == 0 else jnp.logical_not(low_head), qr, 0.0).astype(BF16)
            v = vb[:, slv]
            st = state[h]
            scores = _dot_t(qm, k2) * decay[h]
            ret = _dot(scores.astype(BF16), v) + _dot(qm, st.astype(BF16)) * xi[h]
            zv = (v.astype(F32) * zeta[h]).astype(BF16)
            upd = lax.dot_general(k2, zv, (((0,), (0,)), ((), ())), preferred_element_type=F32)
            state[h] = g_blk[h] * st + upd
            mu = jnp.mean(ret, axis=-1, keepdims=True)
            cen = ret - mu
            var = jnp.mean(cen * cen, axis=-1, keepdims=True)
            gate = gb[:, slv].astype(F32)
            yb = gate * _sigmoid(gate) * (cen * lax.rsqrt(var + EPS)) * rgain[:, slv]
            out[:, width_a + h * RET_VALUE_DIM: width_a + (h + 1) * RET_VALUE_DIM] = yb.astype(out.dtype)


def _attn_bias_table(rel_bias, blk):
    r = np.arange(blk)[:, None]
    c = np.arange(3 * blk)[None, :]
    dist = r + 2 * blk - c
    rel_idx = np.clip(dist, -(CHUNK - 1), MAX_REL) + (CHUNK - 1)
    qc = r // CHUNK + (2 * blk) // CHUNK
    kc = c // CHUNK
    in_band = (kc <= qc) & (kc >= qc - LEFT_CHUNKS)
    tab = rel_bias[:, jnp.asarray(rel_idx)].astype(F32)
    return jnp.where(jnp.asarray(in_band)[None], tab, NEG_BIG)


def _rope_tables(seq):
    half = RET_KEY_DIM // 2
    inv = ROPE_BASE ** (-jnp.arange(half, dtype=F32) / half)
    ang = jnp.arange(seq, dtype=F32)[:, None] * inv[None, :]
    cos, sin = jnp.cos(ang), jnp.sin(ang)
    reps = LANES // RET_KEY_DIM
    cos_t = jnp.tile(jnp.concatenate([cos, cos], axis=-1), (1, reps))
    sin_t = jnp.tile(jnp.concatenate([-sin, sin], axis=-1), (1, reps))
    return cos_t, sin_t


def _mixer(proj, rel_bias, attn_gain, ret_gain, batch, seq):
    t = proj.shape[0]
    blk = SEQ_BLOCK
    assert 2 * blk == LEFT_CHUNKS * CHUNK and seq % blk == 0
    nb = seq // blk
    wa = ATTN_HEADS * HEAD_DIM
    wqk = RET_HEADS * RET_KEY_DIM
    wv = RET_HEADS * RET_VALUE_DIM
    assert wa == wv == 2 * wqk
    bias = _attn_bias_table(rel_bias, blk)
    cos_t, sin_t = _rope_tables(seq)
    decay, xi, zeta, g_blk = _ret_decay_consts(blk)

    def row(b, i):
        return b * nb + i

    def kspec(back, col):
        return pl.BlockSpec((blk, wa), lambda b, i: (row(b, jnp.maximum(i - back, 0)), col))

    const3 = lambda b, i: (0, 0, 0)
    in_specs = [
        pl.BlockSpec((blk, wa), lambda b, i: (row(b, i), 0)),
        kspec(2, 1), kspec(1, 1), kspec(0, 1),
        kspec(2, 2), kspec(1, 2), kspec(0, 2),
        pl.BlockSpec((blk, wqk), lambda b, i: (row(b, i), 3 * wa // wqk)),
        pl.BlockSpec((blk, wqk), lambda b, i: (row(b, i), 3 * wa // wqk + 1)),
        pl.BlockSpec((blk, wv), lambda b, i: (row(b, i), (3 * wa + 2 * wqk) // wv)),
        pl.BlockSpec((blk, wv), lambda b, i: (row(b, i), (3 * wa + 2 * wqk) // wv + 1)),
        pl.BlockSpec((ATTN_HEADS, blk, 3 * blk), const3),
        pl.BlockSpec((blk, LANES), lambda b, i: (i, 0)),
        pl.BlockSpec((blk, LANES), lambda b, i: (i, 0)),
        pl.BlockSpec((RET_HEADS, blk, blk), const3),
        pl.BlockSpec((RET_HEADS, blk, LANES), const3),
        pl.BlockSpec((RET_HEADS, blk, LANES), const3),
        pl.BlockSpec((1, wa), lambda b, i: (0, 0)),
        pl.BlockSpec((1, wv), lambda b, i: (0, 0)),
    ]
    return pl.pallas_call(
        functools.partial(_mixer_kernel, g_blk),
        out_shape=jax.ShapeDtypeStruct((t, wa + wv), BF16),
        grid=(batch, nb),
        in_specs=in_specs,
        out_specs=pl.BlockSpec((blk, wa + wv), lambda b, i: (row(b, i), 0)),
        scratch_shapes=[pltpu.VMEM((RET_HEADS, LANES, RET_VALUE_DIM), F32), pltpu.VMEM((blk, wa), F32)],
        compiler_params=pltpu.CompilerParams(dimension_semantics=("arbitrary", "arbitrary"),
                                             vmem_limit_bytes=VMEM_LIMIT),
        name="mixer",
    )(proj, proj, proj, proj, proj, proj, proj, proj, proj, proj, proj, bias, cos_t, sin_t, decay, xi, zeta,
      attn_gain.reshape(1, wa), ret_gain.reshape(1, wv))


def _layer_norm(z, gain, bias):
    mu = jnp.mean(z, axis=-1, keepdims=True)
    cen = z - mu
    var = jnp.mean(cen * cen, axis=-1, keepdims=True)
    return cen * lax.rsqrt(var + EPS) * gain + bias


def _pack_bf16_pairs(lo, hi):
    return pltpu.pack_elementwise([lo, hi], packed_dtype=BF16)


def _unpack_pairs_f32(words):
    lo = pltpu.unpack_elementwise(words, index=0, packed_dtype=BF16, unpacked_dtype=F32)
    hi = pltpu.unpack_elementwise(words, index=1, packed_dtype=BF16, unpacked_dtype=F32)
    return lo, hi


def _store_packed_rows(ref, val):
    tm, d = val.shape
    half = d // 2
    nchunk = half // LANES
    for c in range(nchunk):
        words = _pack_bf16_pairs(val[:, c * LANES:(c + 1) * LANES], val[:, half + c * LANES: half + (c + 1) * LANES])
        ref[pl.ds(c, tm, stride=nchunk), :] = words


def _route_tile(scores_t, bias_col):
    ne, tm = scores_t.shape
    per_group = ne // N_GROUPS
    biased = scores_t + bias_col
    neg_inf = -jnp.inf
    iota_g = lax.broadcasted_iota(I32, (per_group, tm), 0)
    gscore = []
    for g in range(N_GROUPS):
        b = biased[g * per_group:(g + 1) * per_group, :]
        m1 = jnp.max(b, axis=0, keepdims=True)
        first = jnp.min(jnp.where(b == m1, iota_g, per_group), axis=0, keepdims=True)
        m2 = jnp.max(jnp.where(iota_g == first, neg_inf, b), axis=0, keepdims=True)
        gscore.append(m1 + m2)
    masked_parts = []
    for g in range(N_GROUPS):
        beaten = jnp.zeros((1, tm), I32)
        for o in range(N_GROUPS):
            if o == g:
                continue
            wins = (gscore[o] >= gscore[g]) if o < g else (gscore[o] > gscore[g])
            beaten = beaten + wins.astype(I32)
        keep = beaten < TOPK_GROUPS
        masked_parts.append(jnp.where(keep, biased[g * per_group:(g + 1) * per_group, :], neg_inf))
    masked = jnp.concatenate(masked_parts, axis=0)
    eiota = lax.broadcasted_iota(I32, (ne, tm), 0)
    ids, vals = [], []
    for _ in range(TOP_K):
        m = jnp.max(masked, axis=0, keepdims=True)
        first = jnp.min(jnp.where(masked == m, eiota, ne), axis=0, keepdims=True)
        hit = eiota == first
        vals.append(jnp.sum(jnp.where(hit, scores_t, 0.0), axis=0, keepdims=True))
        ids.append(first)
        masked = jnp.where(hit, neg_inf, masked)
    return ids, vals


def _outproj_kernel(alpha, mix_ref, x_ref, mod_ref, wout_ref, g_ref, b_ref, wrt_ref, rb_ref, tri_ref, ones_ref,
                    x1_ref, h2p_ref, idx_ref, w_ref, rank_ref, cnt_ref, carry):
    i = pl.program_id(0)
    tm, d = x_ref.shape

    @pl.when(i == 0)
    def _():
        carry[...] = jnp.zeros_like(carry)

    y = _dot(mix_ref[...], wout_ref[...])
    gate1 = mod_ref[0, 2:3, :]
    x1 = _layer_norm(alpha * x_ref[...] + gate1 * y, g_ref[...], b_ref[...])
    x1_ref[...] = x1
    h2 = x1 * (1.0 + mod_ref[0, 4:5, :]) + mod_ref[0, 3:4, :]
    _store_packed_rows(h2p_ref, h2)

    scores_t = _sigmoid(_dot_t(wrt_ref[...], h2.astype(BF16)))
    ids, vals = _route_tile(scores_t, rb_ref[:, 0:1])
    ne = scores_t.shape[0]
    eiota = lax.broadcasted_iota(I32, (ne, tm), 0)
    chosen = jnp.zeros((ne, tm), F32)
    for k in range(TOP_K):
        chosen = chosen + (eiota == ids[k]).astype(F32)
    chosen_bf = chosen.astype(BF16)
    before = _dot(chosen_bf, tri_ref[...]) - chosen + jnp.concatenate([carry[...]] * (tm // LANES), axis=-1)
    ranks = [jnp.sum(jnp.where(eiota == ids[k], before, 0.0), axis=0, keepdims=True) for k in range(TOP_K)]
    carry[...] = carry[...] + _dot(chosen_bf, ones_ref[...])
    cnt_ref[...] = carry[...]
    wsel = jnp.concatenate(vals, axis=0)
    w_ref[...] = wsel / jnp.sum(wsel, axis=0, keepdims=True) * ROUTED_SCALE
    idx_ref[...] = jnp.concatenate(ids, axis=0)
    rank_ref[...] = jnp.concatenate(ranks, axis=0).astype(I32)


def _outproj(mix, x2, mod, w_out_bf, ln_g, ln_b, w_router_t_bf, router_bias, alpha, seq):
    t, d = x2.shape
    ne = w_router_t_bf.shape[0]
    tm = ROUTE_TOKENS
    per_batch = seq // tm
    nchunk = d // 2 // LANES
    tri = jnp.asarray(np.triu(np.ones((tm, tm), np.float32)), BF16)
    ones = jnp.ones((tm, LANES), BF16)
    rb = jnp.broadcast_to(router_bias.astype(F32)[:, None], (ne, LANES))
    c2 = lambda i: (0, 0)
    return pl.pallas_call(
        functools.partial(_outproj_kernel, alpha),
        out_shape=(jax.ShapeDtypeStruct((t, d), F32),
                   jax.ShapeDtypeStruct((t * nchunk, LANES), U32),
                   jax.ShapeDtypeStruct((TOP_K, t), I32),
                   jax.ShapeDtypeStruct((TOP_K, t), F32),
                   jax.ShapeDtypeStruct((TOP_K, t), I32),
                   jax.ShapeDtypeStruct((ne, LANES), F32)),
        grid=(t // tm,),
        in_specs=[pl.BlockSpec((tm, d), lambda i: (i, 0)),
                  pl.BlockSpec((tm, d), lambda i: (i, 0)),
                  pl.BlockSpec((1, N_MOD, d), lambda i: (i // per_batch, 0, 0)),
                  pl.BlockSpec((d, d), c2),
                  pl.BlockSpec((1, d), c2),
                  pl.BlockSpec((1, d), c2),
                  pl.BlockSpec((ne, d), c2),
                  pl.BlockSpec((ne, LANES), c2),
                  pl.BlockSpec((tm, tm), c2),
                  pl.BlockSpec((tm, LANES), c2)],
        out_specs=(pl.BlockSpec((tm, d), lambda i: (i, 0)),
                   pl.BlockSpec((tm * nchunk, LANES), lambda i: (i, 0)),
                   pl.BlockSpec((TOP_K, tm), lambda i: (0, i)),
                   pl.BlockSpec((TOP_K, tm), lambda i: (0, i)),
                   pl.BlockSpec((TOP_K, tm), lambda i: (0, i)),
                   pl.BlockSpec((ne, LANES), c2)),
        scratch_shapes=[pltpu.VMEM((ne, LANES), F32)],
        compiler_params=pltpu.CompilerParams(dimension_semantics=("arbitrary",), vmem_limit_bytes=VMEM_LIMIT),
        name="outproj",
    )(mix, x2, mod, w_out_bf, ln_g.reshape(1, d), ln_b.reshape(1, d), w_router_t_bf, rb, tri, ones)


def _slots_kernel(idx_ref, rank_ref, pstart_ref, cstart_ref, dest_ref, cdest_ref):
    ne = pstart_ref.shape[0]
    tm = idx_ref.shape[1]
    eiota = lax.broadcasted_iota(I32, (ne, tm), 0)
    reps = tm // LANES
    pstart = jnp.concatenate([pstart_ref[...]] * reps, axis=-1)
    cstart = jnp.concatenate([cstart_ref[...]] * reps, axis=-1)
    prow, crow = [], []
    for k in range(TOP_K):
        hit = eiota == idx_ref[k:k + 1, :]
        prow.append(jnp.sum(jnp.where(hit, pstart, 0), axis=0, keepdims=True))
        crow.append(jnp.sum(jnp.where(hit, cstart, 0), axis=0, keepdims=True))
    dest_ref[...] = jnp.concatenate(prow, axis=0) + rank_ref[...]
    cdest_ref[...] = jnp.concatenate(crow, axis=0) + rank_ref[...]


def _slots(idx_t, rank_t, pstart, cstart):
    k, t = idx_t.shape
    ne = pstart.shape[0]
    tm = 512
    spec = pl.BlockSpec((k, tm), lambda i: (0, i))
    const = pl.BlockSpec((ne, LANES), lambda i: (0, 0))
    return pl.pallas_call(
        _slots_kernel,
        out_shape=(jax.ShapeDtypeStruct((k, t), I32), jax.ShapeDtypeStruct((k, t), I32)),
        grid=(t // tm,),
        in_specs=[spec, spec, const, const],
        out_specs=(spec, spec),
        compiler_params=pltpu.CompilerParams(dimension_semantics=("arbitrary",)),
        name="slots",
    )(idx_t, rank_t, jnp.broadcast_to(pstart[:, None], (ne, LANES)), jnp.broadcast_to(cstart[:, None], (ne, LANES)))


def _invert_kernel(cdest_ref, tok_ref):
    ntok = cdest_ref.shape[0]
    unroll = 8

    def body(j, carry):
        for u in range(unroll):
            t = j * unroll + u
            tok_ref[cdest_ref[t]] = t
        return carry

    lax.fori_loop(0, ntok // unroll, body, 0)


def _invert(cdest_flat, ntok):
    n = cdest_flat.shape[0]
    return pl.pallas_call(
        _invert_kernel,
        out_shape=jax.ShapeDtypeStruct((n,), I32),
        grid=(n // ntok,),
        in_specs=[pl.BlockSpec((ntok,), lambda k: (k,), memory_space=pltpu.SMEM)],
        out_specs=pl.BlockSpec(memory_space=pltpu.SMEM),
        compiler_params=pltpu.CompilerParams(dimension_semantics=("arbitrary",)),
        name="invert",
    )(cdest_flat)


def _expert_layout(counts, bm, nblk):
    ne = counts.shape[0]
    padded = (counts + bm - 1) // bm * bm
    pend = jnp.cumsum(padded)
    pstart = pend - padded
    cstart = jnp.cumsum(counts) - counts
    blk_start = jnp.arange(nblk, dtype=I32) * bm
    blk_expert = jnp.minimum(jnp.sum((pend[None, :] <= blk_start[:, None]).astype(I32), axis=1), ne - 1)
    onehot = (blk_expert[:, None] == jnp.arange(ne, dtype=I32)[None, :]).astype(I32)
    seg_start = jnp.sum(onehot * pstart[None, :], axis=1)
    seg_count = jnp.sum(onehot * counts[None, :], axis=1)
    seg_cstart = jnp.sum(onehot * cstart[None, :], axis=1)
    offset = blk_start - seg_start
    nvalid = jnp.clip(seg_count - offset, 0, bm).astype(I32)
    cbase = jnp.where(nvalid > 0, seg_cstart + offset, 0).astype(I32)
    return pstart.astype(I32), cstart.astype(I32), blk_expert.astype(I32), nvalid, cbase


def _moe_kernel(be_ref, nv_ref, cb_ref, tok_ref, h2p_ref, wg_ref, wu_ref, wd_ref, y_ref, wgb, wub, wdb, xbuf):
    i = pl.program_id(0)
    nchunk = SUBLANES
    bm = xbuf.shape[0] // nchunk
    e = be_ref[i]
    prev = be_ref[jnp.maximum(i - 1, 0)]
    nvalid = nv_ref[i]

    @pl.when(jnp.logical_or(i == 0, prev != e))
    def _():
        wgb[...] = wg_ref[0].astype(BF16)
        wub[...] = wu_ref[0].astype(BF16)
        wdb[...] = wd_ref[0].astype(BF16)

    @pl.when(nvalid > 0)
    def _():
        base = cb_ref[i]
        last = nvalid - 1

        def gather(j, carry):
            for u in range(SUBLANES):
                r = j * SUBLANES + u
                src = pl.multiple_of(tok_ref[base + jnp.minimum(r, last)] * nchunk, nchunk)
                xbuf[pl.ds(pl.multiple_of(r * nchunk, nchunk), nchunk), :] = h2p_ref[pl.ds(src, nchunk), :]
            return carry

        lax.fori_loop(0, bm // SUBLANES, gather, 0)
        los, his = [], []
        for c in range(nchunk):
            lo, hi = _unpack_pairs_f32(xbuf[pl.ds(c, bm, stride=nchunk), :])
            los.append(lo.astype(BF16))
            his.append(hi.astype(BF16))
        xrows = jnp.concatenate(los + his, axis=-1)
        g = _dot(xrows, wgb[...])
        u = _dot(xrows, wub[...])
        act = (g * _sigmoid(g) * u).astype(BF16)
        _store_packed_rows(y_ref, _dot(act, wdb[...]))

    @pl.when(nvalid == 0)
    def _():
        zeros = jnp.zeros((bm, LANES), F32)
        y_ref[...] = jnp.concatenate([_pack_bf16_pairs(zeros, zeros)] * nchunk, axis=0)


def _moe(h2p, tok, blk_expert, nvalid, cbase, w_gate, w_up, w_down, bm):
    ne, d, de = w_gate.shape
    nblk = blk_expert.shape[0]
    nslot = nblk * bm
    nchunk = d // 2 // LANES
    assert nchunk == SUBLANES
    wspec = lambda i, be, nv, cb, tk: (be[i], 0, 0)
    grid_spec = pltpu.PrefetchScalarGridSpec(
        num_scalar_prefetch=4,
        grid=(nblk,),
        in_specs=[pl.BlockSpec(memory_space=pltpu.VMEM),
                  pl.BlockSpec((1, d, de), wspec),
                  pl.BlockSpec((1, d, de), wspec),
                  pl.BlockSpec((1, de, d), wspec)],
        out_specs=pl.BlockSpec((bm * nchunk, LANES), lambda i, be, nv, cb, tk: (i, 0)),
        scratch_shapes=[pltpu.VMEM((d, de), BF16), pltpu.VMEM((d, de), BF16), pltpu.VMEM((de, d), BF16),
                        pltpu.VMEM((bm * nchunk, LANES), U32)],
    )
    return pl.pallas_call(
        _moe_kernel,
        out_shape=jax.ShapeDtypeStruct((nslot * nchunk, LANES), U32),
        grid_spec=grid_spec,
        compiler_params=pltpu.CompilerParams(dimension_semantics=("arbitrary",), vmem_limit_bytes=VMEM_LIMIT),
        name="moe",
    )(blk_expert, nvalid, cbase, tok, h2p, w_gate, w_up, w_down)


def _combine_kernel(alpha, ntok, pos_ref, y_hbm, w_ref, x1_ref, mod_ref, wsg_ref, wsu_ref, wsd_ref, g_ref, b_ref,
                    o_ref, ybuf, sem):
    i = pl.program_id(0)
    tt, d = x1_ref.shape
    nchunk = SUBLANES
    rows = tt * nchunk

    def row_copy(t, k):
        src = pl.multiple_of(pos_ref[k * ntok + i * tt + t] * nchunk, nchunk)
        dst = pl.multiple_of(k * rows + t * nchunk, nchunk)
        return pltpu.make_async_copy(y_hbm.at[pl.ds(src, nchunk), :], ybuf.at[pl.ds(dst, nchunk), :], sem)

    def issue(t, carry):
        for k in range(TOP_K):
            row_copy(t, k).start()
        return carry

    lax.fori_loop(0, tt, issue, 0)

    x1 = x1_ref[...]
    h2 = (x1 * (1.0 + mod_ref[0, 4:5, :]) + mod_ref[0, 3:4, :]).astype(BF16)
    g = _dot(h2, wsg_ref[...])
    u = _dot(h2, wsu_ref[...])
    shared = _dot((g * _sigmoid(g) * u).astype(BF16), wsd_ref[...])

    pltpu.make_async_copy(y_hbm.at[pl.ds(0, TOP_K * rows), :], ybuf, sem).wait()
    los = [None] * nchunk
    his = [None] * nchunk
    for k in range(TOP_K):
        wk = w_ref[:, k:k + 1]
        for c in range(nchunk):
            lo, hi = _unpack_pairs_f32(ybuf[pl.ds(k * rows + c, tt, stride=nchunk), :])
            los[c] = lo * wk if k == 0 else los[c] + lo * wk
            his[c] = hi * wk if k == 0 else his[c] + hi * wk
    routed = jnp.concatenate(los + his, axis=-1)
    z = alpha * x1 + mod_ref[0, 5:6, :] * (routed + shared)
    o_ref[...] = _layer_norm(z, g_ref[...], b_ref[...])


def _combine(pos, y_sorted, w_tok, x1, mod, wsg_bf, wsu_bf, wsd_bf, ln_g, ln_b, alpha, seq):
    t, d = x1.shape
    ds_ = wsg_bf.shape[1]
    tt = COMBINE_TOKENS
    per_batch = seq // tt
    nchunk = d // 2 // LANES
    assert nchunk == SUBLANES
    grid_spec = pltpu.PrefetchScalarGridSpec(
        num_scalar_prefetch=1,
        grid=(t // tt,),
        in_specs=[pl.BlockSpec(memory_space=pl.ANY),
                  pl.BlockSpec((tt, TOP_K), lambda i, p: (i, 0)),
                  pl.BlockSpec((tt, d), lambda i, p: (i, 0)),
                  pl.BlockSpec((1, N_MOD, d), lambda i, p: (i // per_batch, 0, 0)),
                  pl.BlockSpec((d, ds_), lambda i, p: (0, 0)),
                  pl.BlockSpec((d, ds_), lambda i, p: (0, 0)),
                  pl.BlockSpec((ds_, d), lambda i, p: (0, 0)),
                  pl.BlockSpec((1, d), lambda i, p: (0, 0)),
                  pl.BlockSpec((1, d), lambda i, p: (0, 0))],
        out_specs=pl.BlockSpec((tt, d), lambda i, p: (i, 0)),
        scratch_shapes=[pltpu.VMEM((TOP_K * tt * nchunk, LANES), U32), pltpu.SemaphoreType.DMA],
    )
    return pl.pallas_call(
        functools.partial(_combine_kernel, alpha, t),
        out_shape=jax.ShapeDtypeStruct((t, d), F32),
        grid_spec=grid_spec,
        compiler_params=pltpu.CompilerParams(dimension_semantics=("arbitrary",), vmem_limit_bytes=VMEM_LIMIT),
        name="combine",
    )(pos, y_sorted, w_tok, x1, mod, wsg_bf, wsu_bf, wsd_bf, ln_g.reshape(1, d), ln_b.reshape(1, d))


def kernel(x, c, w_ada, b_ada, w_in, rel_bias, attn_gain, ret_gain, w_out, ln1_gain, ln1_bias, w_router, router_bias,
           w_gate, w_up, w_down, ws_gate, ws_up, ws_down, ln2_gain, ln2_bias):
    batch, seq, d = x.shape
    depth = w_ada.shape[0]
    alpha = (2.0 * depth) ** 0.25
    ne = w_router.shape[-1]
    t = batch * seq
    bm = MOE_BLOCK
    nslot = t * TOP_K + ne * bm
    xt = x.reshape(t, d)
    c_pad = jnp.zeros((8, d), F32).at[:batch].set(c)
    for l in range(depth):
        mod = _ada(c_pad, w_ada[l], b_ada[l])[:batch].reshape(batch, N_MOD, d)
        proj = _inproj(xt, mod, w_in[l], seq)
        mix = _mixer(proj, rel_bias[l], attn_gain[l], ret_gain[l], batch, seq)
        x1, h2p, idx_t, w_t, rank_t, cnt = _outproj(mix, xt, mod, w_out[l].astype(BF16), ln1_gain[l], ln1_bias[l],
                                                    w_router[l].T.astype(BF16), router_bias[l], alpha, seq)
        pstart, cstart, blk_expert, nvalid, cbase = _expert_layout(cnt[:, 0].astype(I32), bm, nslot // bm)
        dest, cdest = _slots(idx_t, rank_t, pstart, cstart)
        tok = _invert(cdest.reshape(-1), t)
        y_sorted = _moe(h2p, tok, blk_expert, nvalid, cbase, w_gate[l], w_up[l], w_down[l], bm)
        xt = _combine(dest.reshape(-1), y_sorted, w_t.T, x1, mod, ws_gate[l].astype(BF16), ws_up[l].astype(BF16),
                      ws_down[l].astype(BF16), ln2_gain[l], ln2_bias[l], alpha, seq)
    return xt.reshape(batch, seq, d)
```

```python
import functools

import jax
import jax.numpy as jnp
import numpy as np
from jax import lax
from jax.experimental import pallas as pl
from jax.experimental.pallas import tpu as pltpu

F32 = jnp.float32
BF16 = jnp.bfloat16
U32 = jnp.uint32
I32 = jnp.int32

CHUNK = 64
LEFT_CHUNKS = 8
MAX_REL = 256
ATTN_HEADS = 8
HEAD_DIM = 128
RET_HEADS = 8
RET_KEY_DIM = 64
RET_VALUE_DIM = 128
ROPE_BASE = 10000.0
N_GROUPS = 8
TOPK_GROUPS = 4
TOP_K = 8
ROUTED_SCALE = 2.5
EPS = 1e-5
N_MOD = 6
LANES = 128
SUBLANES = 8

SEQ_BLOCK = 256
ROUTE_TOKENS = 256
MOE_BLOCK = 128
COMBINE_TOKENS = 128
NEG_BIG = -1e30
VMEM_LIMIT = 56 * 1024 * 1024


def _sigmoid(v):
    return 1.0 / (1.0 + jnp.exp(-v))


def _dot(a, b):
    return jnp.dot(a, b, preferred_element_type=F32)


def _dot_t(a, b):
    return lax.dot_general(a, b, (((1,), (1,)), ((), ())), preferred_element_type=F32)


def _ada_kernel(c_ref, w_ref, b_ref, o_ref):
    c = c_ref[...]
    s = (c * _sigmoid(c)).astype(BF16)
    o_ref[...] = _dot(s, w_ref[...].astype(BF16)) + b_ref[...]


def _ada(c_pad, w_ada, b_ada):
    d, n = w_ada.shape
    tn = 1024
    return pl.pallas_call(
        _ada_kernel,
        out_shape=jax.ShapeDtypeStruct((8, n), F32),
        grid=(n // tn,),
        in_specs=[pl.BlockSpec((8, d), lambda j: (0, 0)),
                  pl.BlockSpec((d, tn), lambda j: (0, j)),
                  pl.BlockSpec((1, tn), lambda j: (0, j))],
        out_specs=pl.BlockSpec((8, tn), lambda j: (0, j)),
        compiler_params=pltpu.CompilerParams(dimension_semantics=("arbitrary",), vmem_limit_bytes=VMEM_LIMIT),
        name="ada",
    )(c_pad, w_ada, b_ada.reshape(1, n))


def _inproj_kernel(x_ref, mod_ref, w_ref, o_ref, wbf_ref):
    @pl.when(pl.program_id(1) == 0)
    def _():
        wbf_ref[...] = w_ref[...].astype(BF16)

    shift = mod_ref[0, 0:1, :]
    scale = mod_ref[0, 1:2, :]
    h = (x_ref[...] * (1.0 + scale) + shift).astype(BF16)
    o_ref[...] = _dot(h, wbf_ref[...]).astype(o_ref.dtype)


def _inproj(x2, mod, w_in, seq):
    t, d = x2.shape
    n = w_in.shape[1]
    tm, tn = 512, 1024
    per_batch = seq // tm
    return pl.pallas_call(
        _inproj_kernel,
        out_shape=jax.ShapeDtypeStruct((t, n), BF16),
        grid=(n // tn, t // tm),
        in_specs=[pl.BlockSpec((tm, d), lambda j, i: (i, 0)),
                  pl.BlockSpec((1, N_MOD, d), lambda j, i: (i // per_batch, 0, 0)),
                  pl.BlockSpec((d, tn), lambda j, i: (0, j))],
        out_specs=pl.BlockSpec((tm, tn), lambda j, i: (i, j)),
        scratch_shapes=[pltpu.VMEM((d, tn), BF16)],
        compiler_params=pltpu.CompilerParams(dimension_semantics=("arbitrary", "arbitrary"),
                                             vmem_limit_bytes=VMEM_LIMIT),
        name="inproj",
    )(x2, mod, w_in)


def _ret_decay_consts(blk):
    h = np.arange(RET_HEADS, dtype=np.float64)
    log_g = np.log(1.0 - 2.0 ** (-5.0 - h))
    n = np.arange(blk, dtype=np.float64)
    diff = n[:, None] - n[None, :]
    same = (n[:, None] // CHUNK) == (n[None, :] // CHUNK)
    later = (n[:, None] // CHUNK) > (n[None, :] // CHUNK)
    expo = np.where(same, np.abs(diff), diff)
    kscale = RET_KEY_DIM ** -0.5
    decay = np.where(same | later, np.exp(log_g[:, None, None] * expo[None]), 0.0) * kscale
    xi = np.exp(log_g[:, None] * (n[None, :] + 1.0))
    zeta = np.exp(log_g[:, None] * (blk - 1.0 - n[None, :])) * kscale
    g_blk = np.exp(log_g * blk)
    xi = np.broadcast_to(xi[:, :, None], (RET_HEADS, blk, LANES))
    zeta = np.broadcast_to(zeta[:, :, None], (RET_HEADS, blk, LANES))
    return (jnp.asarray(decay, F32), jnp.asarray(xi, F32), jnp.asarray(zeta, F32), [float(v) for v in g_blk])


def _mixer_kernel(g_blk, qa, ka0, ka1, ka2, va0, va1, va2, qb, kb, vb, gb, bias, cos, sin, decay, xi, zeta,
                  again, rgain, out, state, oa):
    i = pl.program_id(1)
    blk = qa.shape[0]

    @pl.when(i == 0)
    def _():
        state[...] = jnp.zeros_like(state)

    ok0 = i >= 2
    ok1 = i >= 1
    scale = HEAD_DIM ** -0.5
    ssq = jnp.zeros((blk, 1), F32)
    for h in range(ATTN_HEADS):
        sl = slice(h * HEAD_DIM, (h + 1) * HEAD_DIM)
        q = qa[:, sl]
        s0 = jnp.where(ok0, _dot_t(q, ka0[:, sl]) * scale + bias[h, :, 0:blk], NEG_BIG)
        s1 = jnp.where(ok1, _dot_t(q, ka1[:, sl]) * scale + bias[h, :, blk:2 * blk], NEG_BIG)
        s2 = _dot_t(q, ka2[:, sl]) * scale + bias[h, :, 2 * blk:3 * blk]
        m = jnp.maximum(jnp.maximum(jnp.max(s0, axis=-1, keepdims=True), jnp.max(s1, axis=-1, keepdims=True)),
                        jnp.max(s2, axis=-1, keepdims=True))
        p0 = jnp.exp(s0 - m)
        p1 = jnp.exp(s1 - m)
        p2 = jnp.exp(s2 - m)
        denom = (jnp.sum(p0, axis=-1, keepdims=True) + jnp.sum(p1, axis=-1, keepdims=True)
                 + jnp.sum(p2, axis=-1, keepdims=True))
        o = _dot(p0.astype(BF16), va0[:, sl]) + _dot(p1.astype(BF16), va1[:, sl]) + _dot(p2.astype(BF16), va2[:, sl])
        o = o * (1.0 / denom)
        oa[:, sl] = o
        ssq = ssq + jnp.sum(o * o, axis=-1, keepdims=True)
    width_a = ATTN_HEADS * HEAD_DIM
    inv_rms = lax.rsqrt(ssq * (1.0 / width_a) + EPS)
    out[:, 0:width_a] = (oa[...] * inv_rms * again[...]).astype(out.dtype)

    cosv = cos[...]
    sinv = sin[...]
    lane = lax.broadcasted_iota(jnp.int32, (blk, LANES), 1)
    first_half = (lane % RET_KEY_DIM) < (RET_KEY_DIM // 2)
    low_head = lane < RET_KEY_DIM

    def rope(v):
        rot = jnp.where(first_half, pltpu.roll(v, LANES - RET_KEY_DIM // 2, 1), pltpu.roll(v, RET_KEY_DIM // 2, 1))
        return v * cosv + rot * sinv

    for j in range(RET_HEADS // 2):
        sl2 = slice(j * LANES, (j + 1) * LANES)
        qr = rope(qb[:, sl2].astype(F32))
        k2 = rope(kb[:, sl2].astype(F32)).astype(BF16)
        for hh in range(2):
            h = 2 * j + hh
            slv = slice(h * RET_VALUE_DIM, (h + 1) * RET_VALUE_DIM)
            qm = jnp.where(low_head if hh == 0 else jnp.logical_not(low_head), qr, 0.0).astype(BF16)
            v = vb[:, slv]
            st = state[h]
            scores = _dot_t(qm, k2) * decay[h]
            ret = _dot(scores.astype(BF16), v) + _dot(qm, st.astype(BF16)) * xi[h]
            zv = (v.astype(F32) * zeta[h]).astype(BF16)
            upd = lax.dot_general(k2, zv, (((0,), (0,)), ((), ())), preferred_element_type=F32)
            state[h] = g_blk[h] * st + upd
            mu = jnp.mean(ret, axis=-1, keepdims=True)
            cen = ret - mu
            var = jnp.mean(cen * cen, axis=-1, keepdims=True)
            gate = gb[:, slv].astype(F32)
            yb = gate * _sigmoid(gate) * (cen * lax.rsqrt(var + EPS)) * rgain[:, slv]
            out[:, width_a + h * RET_VALUE_DIM: width_a + (h + 1) * RET_VALUE_DIM] = yb.astype(out.dtype)


def _attn_bias_table(rel_bias, blk):
    nk = 3 * blk
    period = nk + blk
    offs = np.concatenate([np.arange(nk), np.zeros((1,), np.int64), np.arange(-(blk - 1), 0)])
    rel_idx = np.clip(2 * blk - offs, -(CHUNK - 1), MAX_REL) + (CHUNK - 1)
    line = rel_bias[:, jnp.asarray(rel_idx)].astype(F32)
    heads = line.shape[0]
    flat = jnp.broadcast_to(line[:, None, :], (heads, blk, period)).reshape(heads, blk * period)
    tab = flat[:, :blk * (period - 1)].reshape(heads, blk, period - 1)[:, :, :nk]
    r = np.arange(blk)[:, None]
    c = np.arange(nk)[None, :]
    qc = r // CHUNK + (2 * blk) // CHUNK
    kc = c // CHUNK
    in_band = (kc <= qc) & (kc >= qc - LEFT_CHUNKS)
    return jnp.where(jnp.asarray(in_band)[None], tab, NEG_BIG)


def _rope_tables(seq):
    half = RET_KEY_DIM // 2
    inv = ROPE_BASE ** (-jnp.arange(half, dtype=F32) / half)
    ang = jnp.arange(seq, dtype=F32)[:, None] * inv[None, :]
    cos, sin = jnp.cos(ang), jnp.sin(ang)
    reps = LANES // RET_KEY_DIM
    cos_t = jnp.tile(jnp.concatenate([cos, cos], axis=-1), (1, reps))
    sin_t = jnp.tile(jnp.concatenate([-sin, sin], axis=-1), (1, reps))
    return cos_t, sin_t


def _mixer(proj, rel_bias, attn_gain, ret_gain, batch, seq):
    t = proj.shape[0]
    blk = SEQ_BLOCK
    assert 2 * blk == LEFT_CHUNKS * CHUNK and seq % blk == 0
    nb = seq // blk
    wa = ATTN_HEADS * HEAD_DIM
    wqk = RET_HEADS * RET_KEY_DIM
    wv = RET_HEADS * RET_VALUE_DIM
    assert wa == wv == 2 * wqk
    bias = _attn_bias_table(rel_bias, blk)
    cos_t, sin_t = _rope_tables(seq)
    decay, xi, zeta, g_blk = _ret_decay_consts(blk)

    def row(b, i):
        return b * nb + i

    def kspec(back, col):
        return pl.BlockSpec((blk, wa), lambda b, i: (row(b, jnp.maximum(i - back, 0)), col))

    const3 = lambda b, i: (0, 0, 0)
    in_specs = [
        pl.BlockSpec((blk, wa), lambda b, i: (row(b, i), 0)),
        kspec(2, 1), kspec(1, 1), kspec(0, 1),
        kspec(2, 2), kspec(1, 2), kspec(0, 2),
        pl.BlockSpec((blk, wqk), lambda b, i: (row(b, i), 3 * wa // wqk)),
        pl.BlockSpec((blk, wqk), lambda b, i: (row(b, i), 3 * wa // wqk + 1)),
        pl.BlockSpec((blk, wv), lambda b, i: (row(b, i), (3 * wa + 2 * wqk) // wv)),
        pl.BlockSpec((blk, wv), lambda b, i: (row(b, i), (3 * wa + 2 * wqk) // wv + 1)),
        pl.BlockSpec((ATTN_HEADS, blk, 3 * blk), const3),
        pl.BlockSpec((blk, LANES), lambda b, i: (i, 0)),
        pl.BlockSpec((blk, LANES), lambda b, i: (i, 0)),
        pl.BlockSpec((RET_HEADS, blk, blk), const3),
        pl.BlockSpec((RET_HEADS, blk, LANES), const3),
        pl.BlockSpec((RET_HEADS, blk, LANES), const3),
        pl.BlockSpec((1, wa), lambda b, i: (0, 0)),
        pl.BlockSpec((1, wv), lambda b, i: (0, 0)),
    ]
    return pl.pallas_call(
        functools.partial(_mixer_kernel, g_blk),
        out_shape=jax.ShapeDtypeStruct((t, wa + wv), BF16),
        grid=(batch, nb),
        in_specs=in_specs,
        out_specs=pl.BlockSpec((blk, wa + wv), lambda b, i: (row(b, i), 0)),
        scratch_shapes=[pltpu.VMEM((RET_HEADS, LANES, RET_VALUE_DIM), F32), pltpu.VMEM((blk, wa), F32)],
        compiler_params=pltpu.CompilerParams(dimension_semantics=("arbitrary", "arbitrary"),
                                             vmem_limit_bytes=VMEM_LIMIT),
        name="mixer",
    )(proj, proj, proj, proj, proj, proj, proj, proj, proj, proj, proj, bias, cos_t, sin_t, decay, xi, zeta,
      attn_gain.reshape(1, wa), ret_gain.reshape(1, wv))


def _layer_norm(z, gain, bias):
    mu = jnp.mean(z, axis=-1, keepdims=True)
    cen = z - mu
    var = jnp.mean(cen * cen, axis=-1, keepdims=True)
    return cen * lax.rsqrt(var + EPS) * gain + bias


def _pack_bf16_pairs(lo, hi):
    return pltpu.pack_elementwise([lo, hi], packed_dtype=BF16)


def _unpack_pairs_f32(words):
    lo = pltpu.unpack_elementwise(words, index=0, packed_dtype=BF16, unpacked_dtype=F32)
    hi = pltpu.unpack_elementwise(words, index=1, packed_dtype=BF16, unpacked_dtype=F32)
    return lo, hi


def _store_packed_rows(ref, val):
    tm, d = val.shape
    half = d // 2
    nchunk = half // LANES
    for c in range(nchunk):
        words = _pack_bf16_pairs(val[:, c * LANES:(c + 1) * LANES], val[:, half + c * LANES: half + (c + 1) * LANES])
        ref[pl.ds(c, tm, stride=nchunk), :] = words


def _route_tile(scores_t, bias_col):
    ne, tm = scores_t.shape
    per_group = ne // N_GROUPS
    biased = scores_t + bias_col
    neg_inf = -jnp.inf
    iota_g = lax.broadcasted_iota(I32, (per_group, tm), 0)
    gscore = []
    for g in range(N_GROUPS):
        b = biased[g * per_group:(g + 1) * per_group, :]
        m1 = jnp.max(b, axis=0, keepdims=True)
        first = jnp.min(jnp.where(b == m1, iota_g, per_group), axis=0, keepdims=True)
        m2 = jnp.max(jnp.where(iota_g == first, neg_inf, b), axis=0, keepdims=True)
        gscore.append(m1 + m2)
    masked_parts = []
    for g in range(N_GROUPS):
        beaten = jnp.zeros((1, tm), I32)
        for o in range(N_GROUPS):
            if o == g:
                continue
            wins = (gscore[o] >= gscore[g]) if o < g else (gscore[o] > gscore[g])
            beaten = beaten + wins.astype(I32)
        keep = beaten < TOPK_GROUPS
        masked_parts.append(jnp.where(keep, biased[g * per_group:(g + 1) * per_group, :], neg_inf))
    masked = jnp.concatenate(masked_parts, axis=0)
    eiota = lax.broadcasted_iota(I32, (ne, tm), 0)
    ids, vals = [], []
    for _ in range(TOP_K):
        m = jnp.max(masked, axis=0, keepdims=True)
        first = jnp.min(jnp.where(masked == m, eiota, ne), axis=0, keepdims=True)
        hit = eiota == first
        vals.append(jnp.sum(jnp.where(hit, scores_t, 0.0), axis=0, keepdims=True))
        ids.append(first)
        masked = jnp.where(hit, neg_inf, masked)
    return ids, vals


def _outproj_kernel(alpha, mix_ref, x_ref, mod_ref, wout_ref, g_ref, b_ref, wrt_ref, rb_ref, tri_ref, ones_ref,
                    x1_ref, h2p_ref, idx_ref, w_ref, rank_ref, cnt_ref, carry):
    i = pl.program_id(0)
    tm, d = x_ref.shape

    @pl.when(i == 0)
    def _():
        carry[...] = jnp.zeros_like(carry)

    y = _dot(mix_ref[...], wout_ref[...])
    gate1 = mod_ref[0, 2:3, :]
    x1 = _layer_norm(alpha * x_ref[...] + gate1 * y, g_ref[...], b_ref[...])
    x1_ref[...] = x1
    h2 = x1 * (1.0 + mod_ref[0, 4:5, :]) + mod_ref[0, 3:4, :]
    _store_packed_rows(h2p_ref, h2)

    scores_t = _sigmoid(_dot_t(wrt_ref[...], h2.astype(BF16)))
    ids, vals = _route_tile(scores_t, rb_ref[:, 0:1])
    ne = scores_t.shape[0]
    eiota = lax.broadcasted_iota(I32, (ne, tm), 0)
    chosen = jnp.zeros((ne, tm), F32)
    for k in range(TOP_K):
        chosen = chosen + (eiota == ids[k]).astype(F32)
    chosen_bf = chosen.astype(BF16)
    before = _dot(chosen_bf, tri_ref[...]) - chosen + jnp.concatenate([carry[...]] * (tm // LANES), axis=-1)
    ranks = [jnp.sum(jnp.where(eiota == ids[k], before, 0.0), axis=0, keepdims=True) for k in range(TOP_K)]
    carry[...] = carry[...] + _dot(chosen_bf, ones_ref[...])
    cnt_ref[...] = carry[...]
    wsel = jnp.concatenate(vals, axis=0)
    w_ref[...] = wsel / jnp.sum(wsel, axis=0, keepdims=True) * ROUTED_SCALE
    idx_ref[...] = jnp.concatenate(ids, axis=0)
    rank_ref[...] = jnp.concatenate(ranks, axis=0).astype(I32)


def _outproj(mix, x2, mod, w_out_bf, ln_g, ln_b, w_router_t_bf, router_bias, alpha, seq):
    t, d = x2.shape
    ne = w_router_t_bf.shape[0]
    tm = ROUTE_TOKENS
    per_batch = seq // tm
    nchunk = d // 2 // LANES
    tri = jnp.asarray(np.triu(np.ones((tm, tm), np.float32)), BF16)
    ones = jnp.ones((tm, LANES), BF16)
    rb = jnp.broadcast_to(router_bias.astype(F32)[:, None], (ne, LANES))
    c2 = lambda i: (0, 0)
    return pl.pallas_call(
        functools.partial(_outproj_kernel, alpha),
        out_shape=(jax.ShapeDtypeStruct((t, d), F32),
                   jax.ShapeDtypeStruct((t * nchunk, LANES), U32),
                   jax.ShapeDtypeStruct((TOP_K, t), I32),
                   jax.ShapeDtypeStruct((TOP_K, t), F32),
                   jax.ShapeDtypeStruct((TOP_K, t), I32),
                   jax.ShapeDtypeStruct((ne, LANES), F32)),
        grid=(t // tm,),
        in_specs=[pl.BlockSpec((tm, d), lambda i: (i, 0)),
                  pl.BlockSpec((tm, d), lambda i: (i, 0)),
                  pl.BlockSpec((1, N_MOD, d), lambda i: (i // per_batch, 0, 0)),
                  pl.BlockSpec((d, d), c2),
                  pl.BlockSpec((1, d), c2),
                  pl.BlockSpec((1, d), c2),
                  pl.BlockSpec((ne, d), c2),
                  pl.BlockSpec((ne, LANES), c2),
                  pl.BlockSpec((tm, tm), c2),
                  pl.BlockSpec((tm, LANES), c2)],
        out_specs=(pl.BlockSpec((tm, d), lambda i: (i, 0)),
                   pl.BlockSpec((tm * nchunk, LANES), lambda i: (i, 0)),
                   pl.BlockSpec((TOP_K, tm), lambda i: (0, i)),
                   pl.BlockSpec((TOP_K, tm), lambda i: (0, i)),
                   pl.BlockSpec((TOP_K, tm), lambda i: (0, i)),
                   pl.BlockSpec((ne, LANES), c2)),
        scratch_shapes=[pltpu.VMEM((ne, LANES), F32)],
        compiler_params=pltpu.CompilerParams(dimension_semantics=("arbitrary",), vmem_limit_bytes=VMEM_LIMIT),
        name="outproj",
    )(mix, x2, mod, w_out_bf, ln_g.reshape(1, d), ln_b.reshape(1, d), w_router_t_bf, rb, tri, ones)


def _slots_kernel(idx_ref, rank_ref, pstart_ref, cstart_ref, dest_ref, cdest_ref):
    ne = pstart_ref.shape[0]
    tm = idx_ref.shape[1]
    eiota = lax.broadcasted_iota(I32, (ne, tm), 0)
    reps = tm // LANES
    pstart = jnp.concatenate([pstart_ref[...]] * reps, axis=-1)
    cstart = jnp.concatenate([cstart_ref[...]] * reps, axis=-1)
    prow, crow = [], []
    for k in range(TOP_K):
        hit = eiota == idx_ref[k:k + 1, :]
        prow.append(jnp.sum(jnp.where(hit, pstart, 0), axis=0, keepdims=True))
        crow.append(jnp.sum(jnp.where(hit, cstart, 0), axis=0, keepdims=True))
    dest_ref[...] = jnp.concatenate(prow, axis=0) + rank_ref[...]
    cdest_ref[...] = jnp.concatenate(crow, axis=0) + rank_ref[...]


def _slots(idx_t, rank_t, pstart, cstart):
    k, t = idx_t.shape
    ne = pstart.shape[0]
    tm = 512
    spec = pl.BlockSpec((k, tm), lambda i: (0, i))
    const = pl.BlockSpec((ne, LANES), lambda i: (0, 0))
    return pl.pallas_call(
        _slots_kernel,
        out_shape=(jax.ShapeDtypeStruct((k, t), I32), jax.ShapeDtypeStruct((k, t), I32)),
        grid=(t // tm,),
        in_specs=[spec, spec, const, const],
        out_specs=(spec, spec),
        compiler_params=pltpu.CompilerParams(dimension_semantics=("arbitrary",)),
        name="slots",
    )(idx_t, rank_t, jnp.broadcast_to(pstart[:, None], (ne, LANES)), jnp.broadcast_to(cstart[:, None], (ne, LANES)))


def _invert_kernel(pad, cdest_ref, tok_ref):
    ntok = cdest_ref.shape[0]
    n = tok_ref.shape[0] - pad
    unroll = 8

    @pl.when(pl.program_id(0) == 0)
    def _():
        for p in range(pad):
            tok_ref[n + p] = 0

    def body(j, carry):
        for u in range(unroll):
            t = j * unroll + u
            tok_ref[cdest_ref[t]] = t * SUBLANES
        return carry

    lax.fori_loop(0, ntok // unroll, body, 0)


def _invert(cdest_flat, ntok, pad):
    n = cdest_flat.shape[0]
    return pl.pallas_call(
        functools.partial(_invert_kernel, pad),
        out_shape=jax.ShapeDtypeStruct((n + pad,), I32),
        grid=(n // ntok,),
        in_specs=[pl.BlockSpec((ntok,), lambda k: (k,), memory_space=pltpu.SMEM)],
        out_specs=pl.BlockSpec(memory_space=pltpu.SMEM),
        compiler_params=pltpu.CompilerParams(dimension_semantics=("arbitrary",)),
        name="invert",
    )(cdest_flat)


def _expert_layout(counts, bm):
    padded = (counts + bm - 1) // bm * bm
    pend = jnp.cumsum(padded)
    pstart = pend - padded
    cstart = jnp.cumsum(counts) - counts
    return pstart.astype(I32), cstart.astype(I32), (padded // bm).astype(I32), (pend[-1:] // bm).astype(I32)


def _moe_kernel(nb_ref, cs_ref, pb_ref, nused_ref, rows_ref, h2p_ref, wg_ref, wu_ref, wd_ref, y_hbm,
                wgb, wub, wdb, xbuf, ystage, sem, pending):
    e = pl.program_id(0)
    nchunk = SUBLANES
    bm = xbuf.shape[0] // nchunk
    nblk_total = y_hbm.shape[0] // (bm * nchunk)
    nblocks = nb_ref[e]

    def out_copy(blk):
        dst = pl.multiple_of(blk * (bm * nchunk), bm * nchunk)
        return pltpu.make_async_copy(ystage, y_hbm.at[pl.ds(dst, bm * nchunk), :], sem)

    @pl.when(e == 0)
    def _():
        pending[0] = 0

    @pl.when(nblocks > 0)
    def _():
        wgb[...] = wg_ref[0].astype(BF16)
        wub[...] = wu_ref[0].astype(BF16)
        wdb[...] = wd_ref[0].astype(BF16)
        first = cs_ref[e]
        first_blk = pb_ref[e]

        def block(j, carry):
            base = first + j * bm
            for r in range(bm):
                src = pl.multiple_of(rows_ref[base + r], nchunk)
                xbuf[r * nchunk:(r + 1) * nchunk, :] = h2p_ref[pl.ds(src, nchunk), :]
            los, his = [], []
            for c in range(nchunk):
                lo, hi = _unpack_pairs_f32(xbuf[pl.ds(c, bm, stride=nchunk), :])
                los.append(lo.astype(BF16))
                his.append(hi.astype(BF16))
            xrows = jnp.concatenate(los + his, axis=-1)
            g = _dot(xrows, wgb[...])
            u = _dot(xrows, wub[...])
            act = (g * _sigmoid(g) * u).astype(BF16)
            y = _dot(act, wdb[...])

            @pl.when(pending[0] == 1)
            def _():
                out_copy(0).wait()

            _store_packed_rows(ystage, y)
            out_copy(first_blk + j).start()
            pending[0] = 1
            return carry

        lax.fori_loop(0, nblocks, block, 0)

    @pl.when(e == pl.num_programs(0) - 1)
    def _():
        @pl.when(pending[0] == 1)
        def _():
            out_copy(0).wait()

        zeros = jnp.zeros((bm, LANES), F32)
        ystage[...] = jnp.concatenate([_pack_bf16_pairs(zeros, zeros)] * nchunk, axis=0)
        nused = nused_ref[0]

        def fill(j, carry):
            out_copy(j).start()
            return carry

        def drain(j, carry):
            out_copy(0).wait()
            return carry

        lax.fori_loop(nused, nblk_total, fill, 0)
        lax.fori_loop(nused, nblk_total, drain, 0)


def _moe(h2p, rows, nblocks, cstart, pblock, nused, w_gate, w_up, w_down, bm, nslot):
    ne, d, de = w_gate.shape
    nchunk = d // 2 // LANES
    assert nchunk == SUBLANES and nslot % bm == 0
    wspec = lambda e, *_: (e, 0, 0)
    grid_spec = pltpu.PrefetchScalarGridSpec(
        num_scalar_prefetch=5,
        grid=(ne,),
        in_specs=[pl.BlockSpec(memory_space=pltpu.VMEM),
                  pl.BlockSpec((1, d, de), wspec),
                  pl.BlockSpec((1, d, de), wspec),
                  pl.BlockSpec((1, de, d), wspec)],
        out_specs=pl.BlockSpec(memory_space=pl.ANY),
        scratch_shapes=[pltpu.VMEM((d, de), BF16), pltpu.VMEM((d, de), BF16), pltpu.VMEM((de, d), BF16),
                        pltpu.VMEM((bm * nchunk, LANES), U32), pltpu.VMEM((bm * nchunk, LANES), U32),
                        pltpu.SemaphoreType.DMA, pltpu.SMEM((1,), I32)],
    )
    return pl.pallas_call(
        _moe_kernel,
        out_shape=jax.ShapeDtypeStruct((nslot * nchunk, LANES), U32),
        grid_spec=grid_spec,
        compiler_params=pltpu.CompilerParams(dimension_semantics=("arbitrary",), vmem_limit_bytes=VMEM_LIMIT),
        name="moe",
    )(nblocks, cstart, pblock, nused, rows, h2p, w_gate, w_up, w_down)


def _combine_kernel(alpha, ntok, pos_ref, y_hbm, w_ref, x1_ref, mod_ref, wsg_ref, wsu_ref, wsd_ref, g_ref, b_ref,
                    o_ref, ybuf, wbc, routed, sem):
    i = pl.program_id(0)
    nsteps = pl.num_programs(0)
    tt, d = x1_ref.shape
    half = d // 2
    nchunk = SUBLANES
    rows = tt * nchunk
    slot_rows = TOP_K * rows
    slot = lax.rem(i, 2)

    def issue(step, into):
        def body(t, carry):
            for k in range(TOP_K):
                src = pl.multiple_of(pos_ref[k * ntok + step * tt + t] * nchunk, nchunk)
                dst = pl.multiple_of(into * slot_rows + k * rows + t * nchunk, nchunk)
                pltpu.make_async_copy(y_hbm.at[pl.ds(src, nchunk), :], ybuf.at[pl.ds(dst, nchunk), :],
                                      sem.at[into]).start()
            return carry

        lax.fori_loop(0, tt, body, 0)

    @pl.when(i == 0)
    def _():
        issue(0, 0)

    @pl.when(i + 1 < nsteps)
    def _():
        issue(i + 1, 1 - slot)

    x1 = x1_ref[...]
    h2 = (x1 * (1.0 + mod_ref[0, 4:5, :]) + mod_ref[0, 3:4, :]).astype(BF16)
    g = _dot(h2, wsg_ref[...])
    u = _dot(h2, wsu_ref[...])
    shared = _dot((g * _sigmoid(g) * u).astype(BF16), wsd_ref[...])
    for k in range(TOP_K):
        wbc[k] = jnp.broadcast_to(w_ref[:, k:k + 1], (tt, LANES))

    for s in range(2):
        @pl.when(slot == s)
        def _():
            pltpu.make_async_copy(y_hbm.at[pl.ds(0, slot_rows), :], ybuf.at[pl.ds(s * slot_rows, slot_rows), :],
                                  sem.at[s]).wait()
            for c in range(nchunk):
                acc_lo = acc_hi = None
                for k in range(TOP_K):
                    lo, hi = _unpack_pairs_f32(ybuf[pl.ds(s * slot_rows + k * rows + c, tt, stride=nchunk), :])
                    wk = wbc[k]
                    acc_lo = lo * wk if k == 0 else acc_lo + lo * wk
                    acc_hi = hi * wk if k == 0 else acc_hi + hi * wk
                routed[:, c * LANES:(c + 1) * LANES] = acc_lo
                routed[:, half + c * LANES: half + (c + 1) * LANES] = acc_hi

    z = alpha * x1 + mod_ref[0, 5:6, :] * (routed[...] + shared)
    o_ref[...] = _layer_norm(z, g_ref[...], b_ref[...])


def _combine(pos, y_sorted, w_tok, x1, mod, wsg_bf, wsu_bf, wsd_bf, ln_g, ln_b, alpha, seq):
    t, d = x1.shape
    ds_ = wsg_bf.shape[1]
    tt = COMBINE_TOKENS
    per_batch = seq // tt
    nchunk = d // 2 // LANES
    assert nchunk == SUBLANES
    grid_spec = pltpu.PrefetchScalarGridSpec(
        num_scalar_prefetch=1,
        grid=(t // tt,),
        in_specs=[pl.BlockSpec(memory_space=pl.ANY),
                  pl.BlockSpec((tt, TOP_K), lambda i, p: (i, 0)),
                  pl.BlockSpec((tt, d), lambda i, p: (i, 0)),
                  pl.BlockSpec((1, N_MOD, d), lambda i, p: (i // per_batch, 0, 0)),
                  pl.BlockSpec((d, ds_), lambda i, p: (0, 0)),
                  pl.BlockSpec((d, ds_), lambda i, p: (0, 0)),
                  pl.BlockSpec((ds_, d), lambda i, p: (0, 0)),
                  pl.BlockSpec((1, d), lambda i, p: (0, 0)),
                  pl.BlockSpec((1, d), lambda i, p: (0, 0))],
        out_specs=pl.BlockSpec((tt, d), lambda i, p: (i, 0)),
        scratch_shapes=[pltpu.VMEM((2 * TOP_K * tt * nchunk, LANES), U32), pltpu.VMEM((TOP_K, tt, LANES), F32),
                        pltpu.VMEM((tt, d), F32), pltpu.SemaphoreType.DMA((2,))],
    )
    return pl.pallas_call(
        functools.partial(_combine_kernel, alpha, t),
        out_shape=jax.ShapeDtypeStruct((t, d), F32),
        grid_spec=grid_spec,
        compiler_params=pltpu.CompilerParams(dimension_semantics=("arbitrary",), vmem_limit_bytes=VMEM_LIMIT),
        name="combine",
    )(pos, y_sorted, w_tok, x1, mod, wsg_bf, wsu_bf, wsd_bf, ln_g.reshape(1, d), ln_b.reshape(1, d))


def kernel(x, c, w_ada, b_ada, w_in, rel_bias, attn_gain, ret_gain, w_out, ln1_gain, ln1_bias, w_router, router_bias,
           w_gate, w_up, w_down, ws_gate, ws_up, ws_down, ln2_gain, ln2_bias):
    batch, seq, d = x.shape
    depth = w_ada.shape[0]
    alpha = (2.0 * depth) ** 0.25
    ne = w_router.shape[-1]
    t = batch * seq
    bm = MOE_BLOCK
    nslot = t * TOP_K + ne * bm
    xt = x.reshape(t, d)
    c_pad = jnp.zeros((8, d), F32).at[:batch].set(c)
    for l in range(depth):
        mod = _ada(c_pad, w_ada[l], b_ada[l])[:batch].reshape(batch, N_MOD, d)
        proj = _inproj(xt, mod, w_in[l], seq)
        mix = _mixer(proj, rel_bias[l], attn_gain[l], ret_gain[l], batch, seq)
        x1, h2p, idx_t, w_t, rank_t, cnt = _outproj(mix, xt, mod, w_out[l].astype(BF16), ln1_gain[l], ln1_bias[l],
                                                    w_router[l].T.astype(BF16), router_bias[l], alpha, seq)
        pstart, cstart, nblocks, nused = _expert_layout(cnt[:, 0].astype(I32), bm)
        dest, cdest = _slots(idx_t, rank_t, pstart, cstart)
        rows = _invert(cdest.reshape(-1), t, bm)
        y_sorted = _moe(h2p, rows, nblocks, cstart, pstart // bm, nused, w_gate[l], w_up[l], w_down[l], bm, nslot)
        xt = _combine(dest.reshape(-1), y_sorted, w_t.T, x1, mod, ws_gate[l].astype(BF16), ws_up[l].astype(BF16),
                      ws_down[l].astype(BF16), ln2_gain[l], ln2_bias[l], alpha, seq)
    return xt.reshape(batch, seq, d)
```

```python
import functools

import jax
import jax.numpy as jnp
import numpy as np
from jax import lax
from jax.experimental import pallas as pl
from jax.experimental.pallas import tpu as pltpu

F32 = jnp.float32
BF16 = jnp.bfloat16
U32 = jnp.uint32
I32 = jnp.int32

CHUNK = 64
LEFT_CHUNKS = 8
MAX_REL = 256
ATTN_HEADS = 8
HEAD_DIM = 128
RET_HEADS = 8
RET_KEY_DIM = 64
RET_VALUE_DIM = 128
ROPE_BASE = 10000.0
N_GROUPS = 8
TOPK_GROUPS = 4
TOP_K = 8
ROUTED_SCALE = 2.5
EPS = 1e-5
N_MOD = 6
LANES = 128
SUBLANES = 8

SEQ_BLOCK = 256
ROUTE_TOKENS = 256
MOE_BLOCK = 128
COMBINE_TOKENS = 128
NEG_BIG = -1e30
VMEM_LIMIT = 56 * 1024 * 1024


def _sigmoid(v):
    return 1.0 / (1.0 + jnp.exp(-v))


def _dot(a, b):
    return jnp.dot(a, b, preferred_element_type=F32)


def _dot_t(a, b):
    return lax.dot_general(a, b, (((1,), (1,)), ((), ())), preferred_element_type=F32)


def _ada_kernel(c_ref, w_ref, b_ref, o_ref):
    c = c_ref[...]
    s = (c * _sigmoid(c)).astype(BF16)
    o_ref[...] = _dot(s, w_ref[...].astype(BF16)) + b_ref[...]


def _ada(c_pad, w_ada, b_ada):
    d, n = w_ada.shape
    tn = 1024
    return pl.pallas_call(
        _ada_kernel,
        out_shape=jax.ShapeDtypeStruct((8, n), F32),
        grid=(n // tn,),
        in_specs=[pl.BlockSpec((8, d), lambda j: (0, 0)),
                  pl.BlockSpec((d, tn), lambda j: (0, j)),
                  pl.BlockSpec((1, tn), lambda j: (0, j))],
        out_specs=pl.BlockSpec((8, tn), lambda j: (0, j)),
        compiler_params=pltpu.CompilerParams(dimension_semantics=("arbitrary",), vmem_limit_bytes=VMEM_LIMIT),
        name="ada",
    )(c_pad, w_ada, b_ada.reshape(1, n))


def _inproj_kernel(x_ref, mod_ref, w_ref, o_ref, wbf_ref):
    @pl.when(pl.program_id(1) == 0)
    def _():
        wbf_ref[...] = w_ref[...].astype(BF16)

    shift = mod_ref[0, 0:1, :]
    scale = mod_ref[0, 1:2, :]
    h = (x_ref[...] * (1.0 + scale) + shift).astype(BF16)
    o_ref[...] = _dot(h, wbf_ref[...]).astype(o_ref.dtype)


def _inproj(x2, mod, w_in, seq):
    t, d = x2.shape
    n = w_in.shape[1]
    tm, tn = 512, 1024
    per_batch = seq // tm
    return pl.pallas_call(
        _inproj_kernel,
        out_shape=jax.ShapeDtypeStruct((t, n), BF16),
        grid=(n // tn, t // tm),
        in_specs=[pl.BlockSpec((tm, d), lambda j, i: (i, 0)),
                  pl.BlockSpec((1, N_MOD, d), lambda j, i: (i // per_batch, 0, 0)),
                  pl.BlockSpec((d, tn), lambda j, i: (0, j))],
        out_specs=pl.BlockSpec((tm, tn), lambda j, i: (i, j)),
        scratch_shapes=[pltpu.VMEM((d, tn), BF16)],
        compiler_params=pltpu.CompilerParams(dimension_semantics=("arbitrary", "arbitrary"),
                                             vmem_limit_bytes=VMEM_LIMIT),
        name="inproj",
    )(x2, mod, w_in)


def _ret_decay_consts(blk):
    h = np.arange(RET_HEADS, dtype=np.float64)
    log_g = np.log(1.0 - 2.0 ** (-5.0 - h))
    n = np.arange(blk, dtype=np.float64)
    diff = n[:, None] - n[None, :]
    same = (n[:, None] // CHUNK) == (n[None, :] // CHUNK)
    later = (n[:, None] // CHUNK) > (n[None, :] // CHUNK)
    expo = np.where(same, np.abs(diff), diff)
    kscale = RET_KEY_DIM ** -0.5
    decay = np.where(same | later, np.exp(log_g[:, None, None] * expo[None]), 0.0) * kscale
    xi = np.exp(log_g[:, None] * (n[None, :] + 1.0))
    zeta = np.exp(log_g[:, None] * (blk - 1.0 - n[None, :])) * kscale
    g_blk = np.exp(log_g * blk)
    xi = np.broadcast_to(xi[:, :, None], (RET_HEADS, blk, LANES))
    zeta = np.broadcast_to(zeta[:, :, None], (RET_HEADS, blk, LANES))
    return (jnp.asarray(decay, F32), jnp.asarray(xi, F32), jnp.asarray(zeta, F32), [float(v) for v in g_blk])


def _mixer_kernel(g_blk, qa, ka0, ka1, ka2, va0, va1, va2, qb, kb, vb, gb, bias, cos, sin, decay, xi, zeta,
                  again, rgain, out, state, oa):
    i = pl.program_id(1)
    blk = qa.shape[0]

    @pl.when(i == 0)
    def _():
        state[...] = jnp.zeros_like(state)

    ok0 = i >= 2
    ok1 = i >= 1
    scale = HEAD_DIM ** -0.5
    ssq = jnp.zeros((blk, 1), F32)
    for h in range(ATTN_HEADS):
        sl = slice(h * HEAD_DIM, (h + 1) * HEAD_DIM)
        q = qa[:, sl]
        s0 = jnp.where(ok0, _dot_t(q, ka0[:, sl]) * scale + bias[h, :, 0:blk], NEG_BIG)
        s1 = jnp.where(ok1, _dot_t(q, ka1[:, sl]) * scale + bias[h, :, blk:2 * blk], NEG_BIG)
        s2 = _dot_t(q, ka2[:, sl]) * scale + bias[h, :, 2 * blk:3 * blk]
        m = jnp.maximum(jnp.maximum(jnp.max(s0, axis=-1, keepdims=True), jnp.max(s1, axis=-1, keepdims=True)),
                        jnp.max(s2, axis=-1, keepdims=True))
        p0 = jnp.exp(s0 - m)
        p1 = jnp.exp(s1 - m)
        p2 = jnp.exp(s2 - m)
        denom = (jnp.sum(p0, axis=-1, keepdims=True) + jnp.sum(p1, axis=-1, keepdims=True)
                 + jnp.sum(p2, axis=-1, keepdims=True))
        o = _dot(p0.astype(BF16), va0[:, sl]) + _dot(p1.astype(BF16), va1[:, sl]) + _dot(p2.astype(BF16), va2[:, sl])
        o = o * (1.0 / denom)
        oa[:, sl] = o
        ssq = ssq + jnp.sum(o * o, axis=-1, keepdims=True)
    width_a = ATTN_HEADS * HEAD_DIM
    inv_rms = lax.rsqrt(ssq * (1.0 / width_a) + EPS)
    out[:, 0:width_a] = (oa[...] * inv_rms * again[...]).astype(out.dtype)

    cosv = cos[...]
    sinv = sin[...]
    lane = lax.broadcasted_iota(jnp.int32, (blk, LANES), 1)
    first_half = (lane % RET_KEY_DIM) < (RET_KEY_DIM // 2)
    low_head = lane < RET_KEY_DIM

    def rope(v):
        rot = jnp.where(first_half, pltpu.roll(v, LANES - RET_KEY_DIM // 2, 1), pltpu.roll(v, RET_KEY_DIM // 2, 1))
        return v * cosv + rot * sinv

    for j in range(RET_HEADS // 2):
        sl2 = slice(j * LANES, (j + 1) * LANES)
        qr = rope(qb[:, sl2].astype(F32))
        k2 = rope(kb[:, sl2].astype(F32)).astype(BF16)
        for hh in range(2):
            h = 2 * j + hh
            slv = slice(h * RET_VALUE_DIM, (h + 1) * RET_VALUE_DIM)
            qm = jnp.where(low_head if hh == 0 else jnp.logical_not(low_head), qr, 0.0).astype(BF16)
            v = vb[:, slv]
            st = state[h]
            scores = _dot_t(qm, k2) * decay[h]
            ret = _dot(scores.astype(BF16), v) + _dot(qm, st.astype(BF16)) * xi[h]
            zv = (v.astype(F32) * zeta[h]).astype(BF16)
            upd = lax.dot_general(k2, zv, (((0,), (0,)), ((), ())), preferred_element_type=F32)
            state[h] = g_blk[h] * st + upd
            mu = jnp.mean(ret, axis=-1, keepdims=True)
            cen = ret - mu
            var = jnp.mean(cen * cen, axis=-1, keepdims=True)
            gate = gb[:, slv].astype(F32)
            yb = gate * _sigmoid(gate) * (cen * lax.rsqrt(var + EPS)) * rgain[:, slv]
            out[:, width_a + h * RET_VALUE_DIM: width_a + (h + 1) * RET_VALUE_DIM] = yb.astype(out.dtype)


def _attn_bias_table(rel_bias, blk):
    nk = 3 * blk
    period = nk + blk
    offs = np.concatenate([np.arange(nk), np.zeros((1,), np.int64), np.arange(-(blk - 1), 0)])
    rel_idx = np.clip(2 * blk - offs, -(CHUNK - 1), MAX_REL) + (CHUNK - 1)
    line = rel_bias[:, jnp.asarray(rel_idx)].astype(F32)
    heads = line.shape[0]
    flat = jnp.broadcast_to(line[:, None, :], (heads, blk, period)).reshape(heads, blk * period)
    tab = flat[:, :blk * (period - 1)].reshape(heads, blk, period - 1)[:, :, :nk]
    r = np.arange(blk)[:, None]
    c = np.arange(nk)[None, :]
    qc = r // CHUNK + (2 * blk) // CHUNK
    kc = c // CHUNK
    in_band = (kc <= qc) & (kc >= qc - LEFT_CHUNKS)
    return jnp.where(jnp.asarray(in_band)[None], tab, NEG_BIG)


def _rope_tables(seq):
    half = RET_KEY_DIM // 2
    inv = ROPE_BASE ** (-jnp.arange(half, dtype=F32) / half)
    ang = jnp.arange(seq, dtype=F32)[:, None] * inv[None, :]
    cos, sin = jnp.cos(ang), jnp.sin(ang)
    reps = LANES // RET_KEY_DIM
    cos_t = jnp.tile(jnp.concatenate([cos, cos], axis=-1), (1, reps))
    sin_t = jnp.tile(jnp.concatenate([-sin, sin], axis=-1), (1, reps))
    return cos_t, sin_t


def _mixer(proj, rel_bias, attn_gain, ret_gain, batch, seq):
    t = proj.shape[0]
    blk = SEQ_BLOCK
    assert 2 * blk == LEFT_CHUNKS * CHUNK and seq % blk == 0
    nb = seq // blk
    wa = ATTN_HEADS * HEAD_DIM
    wqk = RET_HEADS * RET_KEY_DIM
    wv = RET_HEADS * RET_VALUE_DIM
    assert wa == wv == 2 * wqk
    bias = _attn_bias_table(rel_bias, blk)
    cos_t, sin_t = _rope_tables(seq)
    decay, xi, zeta, g_blk = _ret_decay_consts(blk)

    def row(b, i):
        return b * nb + i

    def kspec(back, col):
        return pl.BlockSpec((blk, wa), lambda b, i: (row(b, jnp.maximum(i - back, 0)), col))

    const3 = lambda b, i: (0, 0, 0)
    in_specs = [
        pl.BlockSpec((blk, wa), lambda b, i: (row(b, i), 0)),
        kspec(2, 1), kspec(1, 1), kspec(0, 1),
        kspec(2, 2), kspec(1, 2), kspec(0, 2),
        pl.BlockSpec((blk, wqk), lambda b, i: (row(b, i), 3 * wa // wqk)),
        pl.BlockSpec((blk, wqk), lambda b, i: (row(b, i), 3 * wa // wqk + 1)),
        pl.BlockSpec((blk, wv), lambda b, i: (row(b, i), (3 * wa + 2 * wqk) // wv)),
        pl.BlockSpec((blk, wv), lambda b, i: (row(b, i), (3 * wa + 2 * wqk) // wv + 1)),
        pl.BlockSpec((ATTN_HEADS, blk, 3 * blk), const3),
        pl.BlockSpec((blk, LANES), lambda b, i: (i, 0)),
        pl.BlockSpec((blk, LANES), lambda b, i: (i, 0)),
        pl.BlockSpec((RET_HEADS, blk, blk), const3),
        pl.BlockSpec((RET_HEADS, blk, LANES), const3),
        pl.BlockSpec((RET_HEADS, blk, LANES), const3),
        pl.BlockSpec((1, wa), lambda b, i: (0, 0)),
        pl.BlockSpec((1, wv), lambda b, i: (0, 0)),
    ]
    return pl.pallas_call(
        functools.partial(_mixer_kernel, g_blk),
        out_shape=jax.ShapeDtypeStruct((t, wa + wv), BF16),
        grid=(batch, nb),
        in_specs=in_specs,
        out_specs=pl.BlockSpec((blk, wa + wv), lambda b, i: (row(b, i), 0)),
        scratch_shapes=[pltpu.VMEM((RET_HEADS, LANES, RET_VALUE_DIM), F32), pltpu.VMEM((blk, wa), F32)],
        compiler_params=pltpu.CompilerParams(dimension_semantics=("arbitrary", "arbitrary"),
                                             vmem_limit_bytes=VMEM_LIMIT),
        name="mixer",
    )(proj, proj, proj, proj, proj, proj, proj, proj, proj, proj, proj, bias, cos_t, sin_t, decay, xi, zeta,
      attn_gain.reshape(1, wa), ret_gain.reshape(1, wv))


def _layer_norm(z, gain, bias):
    mu = jnp.mean(z, axis=-1, keepdims=True)
    cen = z - mu
    var = jnp.mean(cen * cen, axis=-1, keepdims=True)
    return cen * lax.rsqrt(var + EPS) * gain + bias


def _pack_bf16_pairs(lo, hi):
    return pltpu.pack_elementwise([lo, hi], packed_dtype=BF16)


def _unpack_pairs_f32(words):
    lo = pltpu.unpack_elementwise(words, index=0, packed_dtype=BF16, unpacked_dtype=F32)
    hi = pltpu.unpack_elementwise(words, index=1, packed_dtype=BF16, unpacked_dtype=F32)
    return lo, hi


def _store_packed_rows(ref, val):
    tm, d = val.shape
    half = d // 2
    nchunk = half // LANES
    for c in range(nchunk):
        words = _pack_bf16_pairs(val[:, c * LANES:(c + 1) * LANES], val[:, half + c * LANES: half + (c + 1) * LANES])
        ref[pl.ds(c, tm, stride=nchunk), :] = words


def _route_tile(scores_t, bias_col):
    ne, tm = scores_t.shape
    per_group = ne // N_GROUPS
    biased = scores_t + bias_col
    neg_inf = -jnp.inf
    iota_g = lax.broadcasted_iota(I32, (per_group, tm), 0)
    gscore = []
    for g in range(N_GROUPS):
        b = biased[g * per_group:(g + 1) * per_group, :]
        m1 = jnp.max(b, axis=0, keepdims=True)
        first = jnp.min(jnp.where(b == m1, iota_g, per_group), axis=0, keepdims=True)
        m2 = jnp.max(jnp.where(iota_g == first, neg_inf, b), axis=0, keepdims=True)
        gscore.append(m1 + m2)
    masked_parts = []
    for g in range(N_GROUPS):
        beaten = jnp.zeros((1, tm), I32)
        for o in range(N_GROUPS):
            if o == g:
                continue
            wins = (gscore[o] >= gscore[g]) if o < g else (gscore[o] > gscore[g])
            beaten = beaten + wins.astype(I32)
        keep = beaten < TOPK_GROUPS
        masked_parts.append(jnp.where(keep, biased[g * per_group:(g + 1) * per_group, :], neg_inf))
    masked = jnp.concatenate(masked_parts, axis=0)
    eiota = lax.broadcasted_iota(I32, (ne, tm), 0)
    ids, vals = [], []
    for _ in range(TOP_K):
        m = jnp.max(masked, axis=0, keepdims=True)
        first = jnp.min(jnp.where(masked == m, eiota, ne), axis=0, keepdims=True)
        hit = eiota == first
        vals.append(jnp.sum(jnp.where(hit, scores_t, 0.0), axis=0, keepdims=True))
        ids.append(first)
        masked = jnp.where(hit, neg_inf, masked)
    return ids, vals


def _outproj_kernel(alpha, mix_ref, x_ref, mod_ref, wout_ref, g_ref, b_ref, wrt_ref, rb_ref, tri_ref, ones_ref,
                    x1_ref, h2p_ref, idx_ref, w_ref, rank_ref, cnt_ref, carry):
    i = pl.program_id(0)
    tm, d = x_ref.shape

    @pl.when(i == 0)
    def _():
        carry[...] = jnp.zeros_like(carry)

    y = _dot(mix_ref[...], wout_ref[...])
    gate1 = mod_ref[0, 2:3, :]
    x1 = _layer_norm(alpha * x_ref[...] + gate1 * y, g_ref[...], b_ref[...])
    x1_ref[...] = x1
    h2 = x1 * (1.0 + mod_ref[0, 4:5, :]) + mod_ref[0, 3:4, :]
    _store_packed_rows(h2p_ref, h2)

    scores_t = _sigmoid(_dot_t(wrt_ref[...], h2.astype(BF16)))
    ids, vals = _route_tile(scores_t, rb_ref[:, 0:1])
    ne = scores_t.shape[0]
    eiota = lax.broadcasted_iota(I32, (ne, tm), 0)
    chosen = jnp.zeros((ne, tm), F32)
    for k in range(TOP_K):
        chosen = chosen + (eiota == ids[k]).astype(F32)
    chosen_bf = chosen.astype(BF16)
    before = _dot(chosen_bf, tri_ref[...]) - chosen + jnp.concatenate([carry[...]] * (tm // LANES), axis=-1)
    ranks = [jnp.sum(jnp.where(eiota == ids[k], before, 0.0), axis=0, keepdims=True) for k in range(TOP_K)]
    carry[...] = carry[...] + _dot(chosen_bf, ones_ref[...])
    cnt_ref[...] = carry[...]
    wsel = jnp.concatenate(vals, axis=0)
    w_ref[...] = wsel / jnp.sum(wsel, axis=0, keepdims=True) * ROUTED_SCALE
    idx_ref[...] = jnp.concatenate(ids, axis=0)
    rank_ref[...] = jnp.concatenate(ranks, axis=0).astype(I32)


def _outproj(mix, x2, mod, w_out_bf, ln_g, ln_b, w_router_t_bf, router_bias, alpha, seq):
    t, d = x2.shape
    ne = w_router_t_bf.shape[0]
    tm = ROUTE_TOKENS
    per_batch = seq // tm
    nchunk = d // 2 // LANES
    tri = jnp.asarray(np.triu(np.ones((tm, tm), np.float32)), BF16)
    ones = jnp.ones((tm, LANES), BF16)
    rb = jnp.broadcast_to(router_bias.astype(F32)[:, None], (ne, LANES))
    c2 = lambda i: (0, 0)
    return pl.pallas_call(
        functools.partial(_outproj_kernel, alpha),
        out_shape=(jax.ShapeDtypeStruct((t, d), F32),
                   jax.ShapeDtypeStruct((t * nchunk, LANES), U32),
                   jax.ShapeDtypeStruct((TOP_K, t), I32),
                   jax.ShapeDtypeStruct((TOP_K, t), F32),
                   jax.ShapeDtypeStruct((TOP_K, t), I32),
                   jax.ShapeDtypeStruct((ne, LANES), F32)),
        grid=(t // tm,),
        in_specs=[pl.BlockSpec((tm, d), lambda i: (i, 0)),
                  pl.BlockSpec((tm, d), lambda i: (i, 0)),
                  pl.BlockSpec((1, N_MOD, d), lambda i: (i // per_batch, 0, 0)),
                  pl.BlockSpec((d, d), c2),
                  pl.BlockSpec((1, d), c2),
                  pl.BlockSpec((1, d), c2),
                  pl.BlockSpec((ne, d), c2),
                  pl.BlockSpec((ne, LANES), c2),
                  pl.BlockSpec((tm, tm), c2),
                  pl.BlockSpec((tm, LANES), c2)],
        out_specs=(pl.BlockSpec((tm, d), lambda i: (i, 0)),
                   pl.BlockSpec((tm * nchunk, LANES), lambda i: (i, 0)),
                   pl.BlockSpec((TOP_K, tm), lambda i: (0, i)),
                   pl.BlockSpec((TOP_K, tm), lambda i: (0, i)),
                   pl.BlockSpec((TOP_K, tm), lambda i: (0, i)),
                   pl.BlockSpec((ne, LANES), c2)),
        scratch_shapes=[pltpu.VMEM((ne, LANES), F32)],
        compiler_params=pltpu.CompilerParams(dimension_semantics=("arbitrary",), vmem_limit_bytes=VMEM_LIMIT),
        name="outproj",
    )(mix, x2, mod, w_out_bf, ln_g.reshape(1, d), ln_b.reshape(1, d), w_router_t_bf, rb, tri, ones)


def _slots_kernel(idx_ref, rank_ref, pstart_ref, cstart_ref, dest_ref, cdest_ref):
    ne = pstart_ref.shape[0]
    tm = idx_ref.shape[1]
    eiota = lax.broadcasted_iota(I32, (ne, tm), 0)
    reps = tm // LANES
    pstart = jnp.concatenate([pstart_ref[...]] * reps, axis=-1)
    cstart = jnp.concatenate([cstart_ref[...]] * reps, axis=-1)
    prow, crow = [], []
    for k in range(TOP_K):
        hit = eiota == idx_ref[k:k + 1, :]
        prow.append(jnp.sum(jnp.where(hit, pstart, 0), axis=0, keepdims=True))
        crow.append(jnp.sum(jnp.where(hit, cstart, 0), axis=0, keepdims=True))
    dest_ref[...] = jnp.concatenate(prow, axis=0) + rank_ref[...]
    cdest_ref[...] = jnp.concatenate(crow, axis=0) + rank_ref[...]


def _slots(idx_t, rank_t, pstart, cstart):
    k, t = idx_t.shape
    ne = pstart.shape[0]
    tm = 512
    spec = pl.BlockSpec((k, tm), lambda i: (0, i))
    const = pl.BlockSpec((ne, LANES), lambda i: (0, 0))
    return pl.pallas_call(
        _slots_kernel,
        out_shape=(jax.ShapeDtypeStruct((k, t), I32), jax.ShapeDtypeStruct((k, t), I32)),
        grid=(t // tm,),
        in_specs=[spec, spec, const, const],
        out_specs=(spec, spec),
        compiler_params=pltpu.CompilerParams(dimension_semantics=("arbitrary",)),
        name="slots",
    )(idx_t, rank_t, jnp.broadcast_to(pstart[:, None], (ne, LANES)), jnp.broadcast_to(cstart[:, None], (ne, LANES)))


def _invert_kernel(pad, cdest_ref, tok_ref):
    ntok = cdest_ref.shape[0]
    n = tok_ref.shape[0] - pad
    unroll = 8

    @pl.when(pl.program_id(0) == 0)
    def _():
        for p in range(pad):
            tok_ref[n + p] = 0

    def body(j, carry):
        for u in range(unroll):
            t = j * unroll + u
            tok_ref[cdest_ref[t]] = t * SUBLANES
        return carry

    lax.fori_loop(0, ntok // unroll, body, 0)


def _invert(cdest_flat, ntok, pad):
    n = cdest_flat.shape[0]
    return pl.pallas_call(
        functools.partial(_invert_kernel, pad),
        out_shape=jax.ShapeDtypeStruct((n + pad,), I32),
        grid=(n // ntok,),
        in_specs=[pl.BlockSpec((ntok,), lambda k: (k,), memory_space=pltpu.SMEM)],
        out_specs=pl.BlockSpec(memory_space=pltpu.SMEM),
        compiler_params=pltpu.CompilerParams(dimension_semantics=("arbitrary",)),
        name="invert",
    )(cdest_flat)


def _expert_layout(counts, bm, nblk):
    ne = counts.shape[0]
    padded = (counts + bm - 1) // bm * bm
    pend = jnp.cumsum(padded)
    pstart = pend - padded
    cstart = jnp.cumsum(counts) - counts
    blk_start = jnp.arange(nblk + 1, dtype=I32) * bm
    blk_expert = jnp.sum((pend[None, :] <= blk_start[:, None]).astype(I32), axis=1)
    onehot = (blk_expert[:, None] == jnp.arange(ne, dtype=I32)[None, :]).astype(I32)
    cbase = jnp.sum(onehot * (cstart - pstart)[None, :], axis=1) + jnp.where(blk_expert < ne, blk_start, 0)
    return (pstart.astype(I32), cstart.astype(I32), (padded // bm).astype(I32), cbase.astype(I32),
            (pend[-1:] // bm).astype(I32))


def _moe_kernel(nb_ref, pb_ref, cbase_ref, nused_ref, rows_ref, h2p_ref, wg_ref, wu_ref, wd_ref, y_hbm,
                wgb, wub, wdb, xbuf, ystage, sem, count):
    e = pl.program_id(0)
    nchunk = SUBLANES
    bm = xbuf.shape[1] // nchunk
    nbuf = ystage.shape[0]
    nblk_total = y_hbm.shape[0] // (bm * nchunk)
    nblocks = nb_ref[e]

    def out_copy(slot, blk):
        dst = pl.multiple_of(blk * (bm * nchunk), bm * nchunk)
        return pltpu.make_async_copy(ystage.at[slot], y_hbm.at[pl.ds(dst, bm * nchunk), :], sem.at[slot])

    def gather(blk, into):
        base = cbase_ref[blk]
        for r in range(bm):
            src = pl.multiple_of(rows_ref[base + r], nchunk)
            xbuf[into, r * nchunk:(r + 1) * nchunk, :] = h2p_ref[pl.ds(src, nchunk), :]

    @pl.when(e == 0)
    def _():
        count[0] = 0
        gather(0, 0)

    @pl.when(nblocks > 0)
    def _():
        wgb[...] = wg_ref[0].astype(BF16)
        wub[...] = wu_ref[0].astype(BF16)
        wdb[...] = wd_ref[0].astype(BF16)
        first_blk = pb_ref[e]

        def block(j, carry):
            done = count[0]
            blk = first_blk + j
            cur = lax.rem(done, 2)
            slot = lax.rem(done, nbuf)

            @pl.when(done >= nbuf)
            def _():
                out_copy(slot, 0).wait()

            gather(blk + 1, 1 - cur)
            los, his = [], []
            for c in range(nchunk):
                lo, hi = _unpack_pairs_f32(xbuf[cur, pl.ds(c, bm, stride=nchunk), :])
                los.append(lo.astype(BF16))
                his.append(hi.astype(BF16))
            xrows = jnp.concatenate(los + his, axis=-1)
            g = _dot(xrows, wgb[...])
            u = _dot(xrows, wub[...])
            act = (g * _sigmoid(g) * u).astype(BF16)
            _store_packed_rows(ystage.at[slot], _dot(act, wdb[...]))
            out_copy(slot, blk).start()
            count[0] = done + 1
            return carry

        lax.fori_loop(0, nblocks, block, 0)

    @pl.when(e == pl.num_programs(0) - 1)
    def _():
        done = count[0]
        for s in range(nbuf):
            @pl.when(done > s)
            def _():
                out_copy(s, 0).wait()

        zeros = jnp.zeros((bm, LANES), F32)
        ystage[0] = jnp.concatenate([_pack_bf16_pairs(zeros, zeros)] * nchunk, axis=0)
        nused = nused_ref[0]

        def fill(j, carry):
            out_copy(0, j).start()
            return carry

        def drain(j, carry):
            out_copy(0, 0).wait()
            return carry

        lax.fori_loop(nused, nblk_total, fill, 0)
        lax.fori_loop(nused, nblk_total, drain, 0)


def _moe(h2p, rows, nblocks, pblock, cbase, nused, w_gate, w_up, w_down, bm, nslot):
    ne, d, de = w_gate.shape
    nchunk = d // 2 // LANES
    assert nchunk == SUBLANES and nslot % bm == 0
    nbuf = 4
    wspec = lambda e, *_: (e, 0, 0)
    grid_spec = pltpu.PrefetchScalarGridSpec(
        num_scalar_prefetch=5,
        grid=(ne,),
        in_specs=[pl.BlockSpec(memory_space=pltpu.VMEM),
                  pl.BlockSpec((1, d, de), wspec),
                  pl.BlockSpec((1, d, de), wspec),
                  pl.BlockSpec((1, de, d), wspec)],
        out_specs=pl.BlockSpec(memory_space=pl.ANY),
        scratch_shapes=[pltpu.VMEM((d, de), BF16), pltpu.VMEM((d, de), BF16), pltpu.VMEM((de, d), BF16),
                        pltpu.VMEM((2, bm * nchunk, LANES), U32), pltpu.VMEM((nbuf, bm * nchunk, LANES), U32),
                        pltpu.SemaphoreType.DMA((nbuf,)), pltpu.SMEM((1,), I32)],
    )
    return pl.pallas_call(
        _moe_kernel,
        out_shape=jax.ShapeDtypeStruct((nslot * nchunk, LANES), U32),
        grid_spec=grid_spec,
        compiler_params=pltpu.CompilerParams(dimension_semantics=("arbitrary",), vmem_limit_bytes=VMEM_LIMIT),
        name="moe",
    )(nblocks, pblock, cbase, nused, rows, h2p, w_gate, w_up, w_down)


def _combine_kernel(alpha, ntok, pos_ref, y_hbm, w_ref, x1_ref, mod_ref, wsg_ref, wsu_ref, wsd_ref, g_ref, b_ref,
                    o_ref, ybuf, wbc, routed, sem):
    i = pl.program_id(0)
    nsteps = pl.num_programs(0)
    tt, d = x1_ref.shape
    half = d // 2
    nchunk = SUBLANES
    rows = tt * nchunk
    slot_rows = TOP_K * rows
    slot = lax.rem(i, 2)

    def issue(step, into):
        def body(t, carry):
            for k in range(TOP_K):
                src = pl.multiple_of(pos_ref[k * ntok + step * tt + t] * nchunk, nchunk)
                dst = pl.multiple_of(into * slot_rows + k * rows + t * nchunk, nchunk)
                pltpu.make_async_copy(y_hbm.at[pl.ds(src, nchunk), :], ybuf.at[pl.ds(dst, nchunk), :],
                                      sem.at[into]).start()
            return carry

        lax.fori_loop(0, tt, body, 0)

    @pl.when(i == 0)
    def _():
        issue(0, 0)

    @pl.when(i + 1 < nsteps)
    def _():
        issue(i + 1, 1 - slot)

    x1 = x1_ref[...]
    h2 = (x1 * (1.0 + mod_ref[0, 4:5, :]) + mod_ref[0, 3:4, :]).astype(BF16)
    g = _dot(h2, wsg_ref[...])
    u = _dot(h2, wsu_ref[...])
    shared = _dot((g * _sigmoid(g) * u).astype(BF16), wsd_ref[...])
    for k in range(TOP_K):
        wbc[k] = jnp.broadcast_to(w_ref[:, k:k + 1], (tt, LANES))

    for s in range(2):
        @pl.when(slot == s)
        def _():
            pltpu.make_async_copy(y_hbm.at[pl.ds(0, slot_rows), :], ybuf.at[pl.ds(s * slot_rows, slot_rows), :],
                                  sem.at[s]).wait()
            for c in range(nchunk):
                acc_lo = acc_hi = None
                for k in range(TOP_K):
                    lo, hi = _unpack_pairs_f32(ybuf[pl.ds(s * slot_rows + k * rows + c, tt, stride=nchunk), :])
                    wk = wbc[k]
                    acc_lo = lo * wk if k == 0 else acc_lo + lo * wk
                    acc_hi = hi * wk if k == 0 else acc_hi + hi * wk
                routed[:, c * LANES:(c + 1) * LANES] = acc_lo
                routed[:, half + c * LANES: half + (c + 1) * LANES] = acc_hi

    z = alpha * x1 + mod_ref[0, 5:6, :] * (routed[...] + shared)
    o_ref[...] = _layer_norm(z, g_ref[...], b_ref[...])


def _combine(pos, y_sorted, w_tok, x1, mod, wsg_bf, wsu_bf, wsd_bf, ln_g, ln_b, alpha, seq):
    t, d = x1.shape
    ds_ = wsg_bf.shape[1]
    tt = COMBINE_TOKENS
    per_batch = seq // tt
    nchunk = d // 2 // LANES
    assert nchunk == SUBLANES
    grid_spec = pltpu.PrefetchScalarGridSpec(
        num_scalar_prefetch=1,
        grid=(t // tt,),
        in_specs=[pl.BlockSpec(memory_space=pl.ANY),
                  pl.BlockSpec((tt, TOP_K), lambda i, p: (i, 0)),
                  pl.BlockSpec((tt, d), lambda i, p: (i, 0)),
                  pl.BlockSpec((1, N_MOD, d), lambda i, p: (i // per_batch, 0, 0)),
                  pl.BlockSpec((d, ds_), lambda i, p: (0, 0)),
                  pl.BlockSpec((d, ds_), lambda i, p: (0, 0)),
                  pl.BlockSpec((ds_, d), lambda i, p: (0, 0)),
                  pl.BlockSpec((1, d), lambda i, p: (0, 0)),
                  pl.BlockSpec((1, d), lambda i, p: (0, 0))],
        out_specs=pl.BlockSpec((tt, d), lambda i, p: (i, 0)),
        scratch_shapes=[pltpu.VMEM((2 * TOP_K * tt * nchunk, LANES), U32), pltpu.VMEM((TOP_K, tt, LANES), F32),
                        pltpu.VMEM((tt, d), F32), pltpu.SemaphoreType.DMA((2,))],
    )
    return pl.pallas_call(
        functools.partial(_combine_kernel, alpha, t),
        out_shape=jax.ShapeDtypeStruct((t, d), F32),
        grid_spec=grid_spec,
        compiler_params=pltpu.CompilerParams(dimension_semantics=("arbitrary",), vmem_limit_bytes=VMEM_LIMIT),
        name="combine",
    )(pos, y_sorted, w_tok, x1, mod, wsg_bf, wsu_bf, wsd_bf, ln_g.reshape(1, d), ln_b.reshape(1, d))


def kernel(x, c, w_ada, b_ada, w_in, rel_bias, attn_gain, ret_gain, w_out, ln1_gain, ln1_bias, w_router, router_bias,
           w_gate, w_up, w_down, ws_gate, ws_up, ws_down, ln2_gain, ln2_bias):
    batch, seq, d = x.shape
    depth = w_ada.shape[0]
    alpha = (2.0 * depth) ** 0.25
    ne = w_router.shape[-1]
    t = batch * seq
    bm = MOE_BLOCK
    nslot = t * TOP_K + ne * bm
    xt = x.reshape(t, d)
    c_pad = jnp.zeros((8, d), F32).at[:batch].set(c)
    for l in range(depth):
        mod = _ada(c_pad, w_ada[l], b_ada[l])[:batch].reshape(batch, N_MOD, d)
        proj = _inproj(xt, mod, w_in[l], seq)
        mix = _mixer(proj, rel_bias[l], attn_gain[l], ret_gain[l], batch, seq)
        x1, h2p, idx_t, w_t, rank_t, cnt = _outproj(mix, xt, mod, w_out[l].astype(BF16), ln1_gain[l], ln1_bias[l],
                                                    w_router[l].T.astype(BF16), router_bias[l], alpha, seq)
        pstart, cstart, nblocks, cbase, nused = _expert_layout(cnt[:, 0].astype(I32), bm, nslot // bm)
        dest, cdest = _slots(idx_t, rank_t, pstart, cstart)
        rows = _invert(cdest.reshape(-1), t, bm)
        y_sorted = _moe(h2p, rows, nblocks, pstart // bm, cbase, nused, w_gate[l], w_up[l], w_down[l], bm, nslot)
        xt = _combine(dest.reshape(-1), y_sorted, w_t.T, x1, mod, ws_gate[l].astype(BF16), ws_up[l].astype(BF16),
                      ws_down[l].astype(BF16), ln2_gain[l], ln2_bias[l], alpha, seq)
    return xt.reshape(batch, seq, d)
```

```python
import functools

import jax
import jax.numpy as jnp
import numpy as np
from jax import lax
from jax.experimental import pallas as pl
from jax.experimental.pallas import tpu as pltpu

F32 = jnp.float32
BF16 = jnp.bfloat16
U32 = jnp.uint32
I32 = jnp.int32

CHUNK = 64
LEFT_CHUNKS = 8
MAX_REL = 256
ATTN_HEADS = 8
HEAD_DIM = 128
RET_HEADS = 8
RET_KEY_DIM = 64
RET_VALUE_DIM = 128
ROPE_BASE = 10000.0
N_GROUPS = 8
TOPK_GROUPS = 4
TOP_K = 8
ROUTED_SCALE = 2.5
EPS = 1e-5
N_MOD = 6
LANES = 128
SUBLANES = 8

SEQ_BLOCK = 256
ROUTE_TOKENS = 256
MOE_BLOCK = 128
COMBINE_TOKENS = 128
NEG_BIG = -1e30
VMEM_LIMIT = 56 * 1024 * 1024


def _sigmoid(v):
    return 1.0 / (1.0 + jnp.exp(-v))


def _dot(a, b):
    return jnp.dot(a, b, preferred_element_type=F32)


def _dot_t(a, b):
    return lax.dot_general(a, b, (((1,), (1,)), ((), ())), preferred_element_type=F32)


def _ada_kernel(c_ref, w_ref, b_ref, o_ref):
    c = c_ref[...]
    s = (c * _sigmoid(c)).astype(BF16)
    o_ref[...] = _dot(s, w_ref[...].astype(BF16)) + b_ref[...]


def _ada(c_pad, w_ada, b_ada):
    d, n = w_ada.shape
    tn = 1024
    return pl.pallas_call(
        _ada_kernel,
        out_shape=jax.ShapeDtypeStruct((8, n), F32),
        grid=(n // tn,),
        in_specs=[pl.BlockSpec((8, d), lambda j: (0, 0)),
                  pl.BlockSpec((d, tn), lambda j: (0, j)),
                  pl.BlockSpec((1, tn), lambda j: (0, j))],
        out_specs=pl.BlockSpec((8, tn), lambda j: (0, j)),
        compiler_params=pltpu.CompilerParams(dimension_semantics=("arbitrary",), vmem_limit_bytes=VMEM_LIMIT),
        name="ada",
    )(c_pad, w_ada, b_ada.reshape(1, n))


def _inproj_kernel(x_ref, mod_ref, w_ref, o_ref, wbf_ref):
    @pl.when(pl.program_id(1) == 0)
    def _():
        wbf_ref[...] = w_ref[...].astype(BF16)

    shift = mod_ref[0, 0:1, :]
    scale = mod_ref[0, 1:2, :]
    h = (x_ref[...] * (1.0 + scale) + shift).astype(BF16)
    o_ref[...] = _dot(h, wbf_ref[...]).astype(o_ref.dtype)


def _inproj(x2, mod, w_in, seq):
    t, d = x2.shape
    n = w_in.shape[1]
    tm, tn = 512, 1024
    per_batch = seq // tm
    return pl.pallas_call(
        _inproj_kernel,
        out_shape=jax.ShapeDtypeStruct((t, n), BF16),
        grid=(n // tn, t // tm),
        in_specs=[pl.BlockSpec((tm, d), lambda j, i: (i, 0)),
                  pl.BlockSpec((1, N_MOD, d), lambda j, i: (i // per_batch, 0, 0)),
                  pl.BlockSpec((d, tn), lambda j, i: (0, j))],
        out_specs=pl.BlockSpec((tm, tn), lambda j, i: (i, j)),
        scratch_shapes=[pltpu.VMEM((d, tn), BF16)],
        compiler_params=pltpu.CompilerParams(dimension_semantics=("arbitrary", "arbitrary"),
                                             vmem_limit_bytes=VMEM_LIMIT),
        name="inproj",
    )(x2, mod, w_in)


def _ret_decay_consts(blk):
    h = np.arange(RET_HEADS, dtype=np.float64)
    log_g = np.log(1.0 - 2.0 ** (-5.0 - h))
    n = np.arange(blk, dtype=np.float64)
    diff = n[:, None] - n[None, :]
    same = (n[:, None] // CHUNK) == (n[None, :] // CHUNK)
    later = (n[:, None] // CHUNK) > (n[None, :] // CHUNK)
    expo = np.where(same, np.abs(diff), diff)
    kscale = RET_KEY_DIM ** -0.5
    decay = np.where(same | later, np.exp(log_g[:, None, None] * expo[None]), 0.0) * kscale
    xi = np.exp(log_g[:, None] * (n[None, :] + 1.0))
    zeta = np.exp(log_g[:, None] * (blk - 1.0 - n[None, :])) * kscale
    g_blk = np.exp(log_g * blk)
    xi = np.broadcast_to(xi[:, :, None], (RET_HEADS, blk, LANES))
    zeta = np.broadcast_to(zeta[:, :, None], (RET_HEADS, blk, LANES))
    return (jnp.asarray(decay, F32), jnp.asarray(xi, F32), jnp.asarray(zeta, F32), [float(v) for v in g_blk])


def _mixer_kernel(g_blk, qa, ka0, ka1, ka2, va0, va1, va2, qb, kb, vb, gb, bias, cos, sin, decay, xi, zeta,
                  again, rgain, out, state, oa):
    i = pl.program_id(1)
    blk = qa.shape[0]

    @pl.when(i == 0)
    def _():
        state[...] = jnp.zeros_like(state)

    ok0 = i >= 2
    ok1 = i >= 1
    scale = HEAD_DIM ** -0.5
    ssq = jnp.zeros((blk, 1), F32)
    for h in range(ATTN_HEADS):
        sl = slice(h * HEAD_DIM, (h + 1) * HEAD_DIM)
        q = qa[:, sl]
        s0 = jnp.where(ok0, _dot_t(q, ka0[:, sl]) * scale + bias[h, :, 0:blk], NEG_BIG)
        s1 = jnp.where(ok1, _dot_t(q, ka1[:, sl]) * scale + bias[h, :, blk:2 * blk], NEG_BIG)
        s2 = _dot_t(q, ka2[:, sl]) * scale + bias[h, :, 2 * blk:3 * blk]
        m = jnp.maximum(jnp.maximum(jnp.max(s0, axis=-1, keepdims=True), jnp.max(s1, axis=-1, keepdims=True)),
                        jnp.max(s2, axis=-1, keepdims=True))
        p0 = jnp.exp(s0 - m)
        p1 = jnp.exp(s1 - m)
        p2 = jnp.exp(s2 - m)
        denom = (jnp.sum(p0, axis=-1, keepdims=True) + jnp.sum(p1, axis=-1, keepdims=True)
                 + jnp.sum(p2, axis=-1, keepdims=True))
        o = _dot(p0.astype(BF16), va0[:, sl]) + _dot(p1.astype(BF16), va1[:, sl]) + _dot(p2.astype(BF16), va2[:, sl])
        o = o * (1.0 / denom)
        oa[:, sl] = o
        ssq = ssq + jnp.sum(o * o, axis=-1, keepdims=True)
    width_a = ATTN_HEADS * HEAD_DIM
    inv_rms = lax.rsqrt(ssq * (1.0 / width_a) + EPS)
    out[:, 0:width_a] = (oa[...] * inv_rms * again[...]).astype(out.dtype)

    cosv = cos[...]
    sinv = sin[...]
    lane = lax.broadcasted_iota(jnp.int32, (blk, LANES), 1)
    first_half = (lane % RET_KEY_DIM) < (RET_KEY_DIM // 2)
    low_head = lane < RET_KEY_DIM

    def rope(v):
        rot = jnp.where(first_half, pltpu.roll(v, LANES - RET_KEY_DIM // 2, 1), pltpu.roll(v, RET_KEY_DIM // 2, 1))
        return v * cosv + rot * sinv

    for j in range(RET_HEADS // 2):
        sl2 = slice(j * LANES, (j + 1) * LANES)
        qr = rope(qb[:, sl2].astype(F32))
        k2 = rope(kb[:, sl2].astype(F32)).astype(BF16)
        for hh in range(2):
            h = 2 * j + hh
            slv = slice(h * RET_VALUE_DIM, (h + 1) * RET_VALUE_DIM)
            qm = jnp.where(low_head if hh == 0 else jnp.logical_not(low_head), qr, 0.0).astype(BF16)
            v = vb[:, slv]
            st = state[h]
            scores = _dot_t(qm, k2) * decay[h]
            ret = _dot(scores.astype(BF16), v) + _dot(qm, st.astype(BF16)) * xi[h]
            zv = (v.astype(F32) * zeta[h]).astype(BF16)
            upd = lax.dot_general(k2, zv, (((0,), (0,)), ((), ())), preferred_element_type=F32)
            state[h] = g_blk[h] * st + upd
            mu = jnp.mean(ret, axis=-1, keepdims=True)
            cen = ret - mu
            var = jnp.mean(cen * cen, axis=-1, keepdims=True)
            gate = gb[:, slv].astype(F32)
            yb = gate * _sigmoid(gate) * (cen * lax.rsqrt(var + EPS)) * rgain[:, slv]
            out[:, width_a + h * RET_VALUE_DIM: width_a + (h + 1) * RET_VALUE_DIM] = yb.astype(out.dtype)


def _attn_bias_table(rel_bias, blk):
    nk = 3 * blk
    period = nk + blk
    offs = np.concatenate([np.arange(nk), np.zeros((1,), np.int64), np.arange(-(blk - 1), 0)])
    rel_idx = np.clip(2 * blk - offs, -(CHUNK - 1), MAX_REL) + (CHUNK - 1)
    line = rel_bias[:, jnp.asarray(rel_idx)].astype(F32)
    heads = line.shape[0]
    flat = jnp.broadcast_to(line[:, None, :], (heads, blk, period)).reshape(heads, blk * period)
    tab = flat[:, :blk * (period - 1)].reshape(heads, blk, period - 1)[:, :, :nk]
    r = np.arange(blk)[:, None]
    c = np.arange(nk)[None, :]
    qc = r // CHUNK + (2 * blk) // CHUNK
    kc = c // CHUNK
    in_band = (kc <= qc) & (kc >= qc - LEFT_CHUNKS)
    return jnp.where(jnp.asarray(in_band)[None], tab, NEG_BIG)


def _rope_tables(seq):
    half = RET_KEY_DIM // 2
    inv = ROPE_BASE ** (-jnp.arange(half, dtype=F32) / half)
    ang = jnp.arange(seq, dtype=F32)[:, None] * inv[None, :]
    cos, sin = jnp.cos(ang), jnp.sin(ang)
    reps = LANES // RET_KEY_DIM
    cos_t = jnp.tile(jnp.concatenate([cos, cos], axis=-1), (1, reps))
    sin_t = jnp.tile(jnp.concatenate([-sin, sin], axis=-1), (1, reps))
    return cos_t, sin_t


def _mixer(proj, rel_bias, attn_gain, ret_gain, batch, seq):
    t = proj.shape[0]
    blk = SEQ_BLOCK
    assert 2 * blk == LEFT_CHUNKS * CHUNK and seq % blk == 0
    nb = seq // blk
    wa = ATTN_HEADS * HEAD_DIM
    wqk = RET_HEADS * RET_KEY_DIM
    wv = RET_HEADS * RET_VALUE_DIM
    assert wa == wv == 2 * wqk
    bias = _attn_bias_table(rel_bias, blk)
    cos_t, sin_t = _rope_tables(seq)
    decay, xi, zeta, g_blk = _ret_decay_consts(blk)

    def row(b, i):
        return b * nb + i

    def kspec(back, col):
        return pl.BlockSpec((blk, wa), lambda b, i: (row(b, jnp.maximum(i - back, 0)), col))

    const3 = lambda b, i: (0, 0, 0)
    in_specs = [
        pl.BlockSpec((blk, wa), lambda b, i: (row(b, i), 0)),
        kspec(2, 1), kspec(1, 1), kspec(0, 1),
        kspec(2, 2), kspec(1, 2), kspec(0, 2),
        pl.BlockSpec((blk, wqk), lambda b, i: (row(b, i), 3 * wa // wqk)),
        pl.BlockSpec((blk, wqk), lambda b, i: (row(b, i), 3 * wa // wqk + 1)),
        pl.BlockSpec((blk, wv), lambda b, i: (row(b, i), (3 * wa + 2 * wqk) // wv)),
        pl.BlockSpec((blk, wv), lambda b, i: (row(b, i), (3 * wa + 2 * wqk) // wv + 1)),
        pl.BlockSpec((ATTN_HEADS, blk, 3 * blk), const3),
        pl.BlockSpec((blk, LANES), lambda b, i: (i, 0)),
        pl.BlockSpec((blk, LANES), lambda b, i: (i, 0)),
        pl.BlockSpec((RET_HEADS, blk, blk), const3),
        pl.BlockSpec((RET_HEADS, blk, LANES), const3),
        pl.BlockSpec((RET_HEADS, blk, LANES), const3),
        pl.BlockSpec((1, wa), lambda b, i: (0, 0)),
        pl.BlockSpec((1, wv), lambda b, i: (0, 0)),
    ]
    return pl.pallas_call(
        functools.partial(_mixer_kernel, g_blk),
        out_shape=jax.ShapeDtypeStruct((t, wa + wv), BF16),
        grid=(batch, nb),
        in_specs=in_specs,
        out_specs=pl.BlockSpec((blk, wa + wv), lambda b, i: (row(b, i), 0)),
        scratch_shapes=[pltpu.VMEM((RET_HEADS, LANES, RET_VALUE_DIM), F32), pltpu.VMEM((blk, wa), F32)],
        compiler_params=pltpu.CompilerParams(dimension_semantics=("arbitrary", "arbitrary"),
                                             vmem_limit_bytes=VMEM_LIMIT),
        name="mixer",
    )(proj, proj, proj, proj, proj, proj, proj, proj, proj, proj, proj, bias, cos_t, sin_t, decay, xi, zeta,
      attn_gain.reshape(1, wa), ret_gain.reshape(1, wv))


def _layer_norm(z, gain, bias):
    mu = jnp.mean(z, axis=-1, keepdims=True)
    cen = z - mu
    var = jnp.mean(cen * cen, axis=-1, keepdims=True)
    return cen * lax.rsqrt(var + EPS) * gain + bias


def _pack_bf16_pairs(lo, hi):
    return pltpu.pack_elementwise([lo, hi], packed_dtype=BF16)


def _unpack_pairs_f32(words):
    lo = pltpu.unpack_elementwise(words, index=0, packed_dtype=BF16, unpacked_dtype=F32)
    hi = pltpu.unpack_elementwise(words, index=1, packed_dtype=BF16, unpacked_dtype=F32)
    return lo, hi


def _store_packed_rows(ref, val):
    tm, d = val.shape
    half = d // 2
    nchunk = half // LANES
    for c in range(nchunk):
        words = _pack_bf16_pairs(val[:, c * LANES:(c + 1) * LANES], val[:, half + c * LANES: half + (c + 1) * LANES])
        ref[pl.ds(c, tm, stride=nchunk), :] = words


def _route_tile(scores_t, bias_col):
    ne, tm = scores_t.shape
    per_group = ne // N_GROUPS
    biased = scores_t + bias_col
    neg_inf = -jnp.inf
    iota_g = lax.broadcasted_iota(I32, (per_group, tm), 0)
    gscore = []
    for g in range(N_GROUPS):
        b = biased[g * per_group:(g + 1) * per_group, :]
        m1 = jnp.max(b, axis=0, keepdims=True)
        first = jnp.min(jnp.where(b == m1, iota_g, per_group), axis=0, keepdims=True)
        m2 = jnp.max(jnp.where(iota_g == first, neg_inf, b), axis=0, keepdims=True)
        gscore.append(m1 + m2)
    masked_parts = []
    for g in range(N_GROUPS):
        beaten = jnp.zeros((1, tm), I32)
        for o in range(N_GROUPS):
            if o == g:
                continue
            wins = (gscore[o] >= gscore[g]) if o < g else (gscore[o] > gscore[g])
            beaten = beaten + wins.astype(I32)
        keep = beaten < TOPK_GROUPS
        masked_parts.append(jnp.where(keep, biased[g * per_group:(g + 1) * per_group, :], neg_inf))
    masked = jnp.concatenate(masked_parts, axis=0)
    eiota = lax.broadcasted_iota(I32, (ne, tm), 0)
    ids, vals = [], []
    for _ in range(TOP_K):
        m = jnp.max(masked, axis=0, keepdims=True)
        first = jnp.min(jnp.where(masked == m, eiota, ne), axis=0, keepdims=True)
        hit = eiota == first
        vals.append(jnp.sum(jnp.where(hit, scores_t, 0.0), axis=0, keepdims=True))
        ids.append(first)
        masked = jnp.where(hit, neg_inf, masked)
    return ids, vals


def _outproj_kernel(alpha, mix_ref, x_ref, mod_ref, wout_ref, g_ref, b_ref, wrt_ref, rb_ref, tri_ref, ones_ref,
                    x1_ref, h2p_ref, idx_ref, w_ref, rank_ref, cnt_ref, carry):
    i = pl.program_id(0)
    tm, d = x_ref.shape

    @pl.when(i == 0)
    def _():
        carry[...] = jnp.zeros_like(carry)

    y = _dot(mix_ref[...], wout_ref[...])
    gate1 = mod_ref[0, 2:3, :]
    x1 = _layer_norm(alpha * x_ref[...] + gate1 * y, g_ref[...], b_ref[...])
    x1_ref[...] = x1
    h2 = x1 * (1.0 + mod_ref[0, 4:5, :]) + mod_ref[0, 3:4, :]
    _store_packed_rows(h2p_ref, h2)

    scores_t = _sigmoid(_dot_t(wrt_ref[...], h2.astype(BF16)))
    ids, vals = _route_tile(scores_t, rb_ref[:, 0:1])
    ne = scores_t.shape[0]
    eiota = lax.broadcasted_iota(I32, (ne, tm), 0)
    chosen = jnp.zeros((ne, tm), F32)
    for k in range(TOP_K):
        chosen = chosen + (eiota == ids[k]).astype(F32)
    chosen_bf = chosen.astype(BF16)
    before = _dot(chosen_bf, tri_ref[...]) - chosen + jnp.concatenate([carry[...]] * (tm // LANES), axis=-1)
    ranks = [jnp.sum(jnp.where(eiota == ids[k], before, 0.0), axis=0, keepdims=True) for k in range(TOP_K)]
    carry[...] = carry[...] + _dot(chosen_bf, ones_ref[...])
    cnt_ref[...] = carry[...]
    wsel = jnp.concatenate(vals, axis=0)
    w_ref[...] = wsel / jnp.sum(wsel, axis=0, keepdims=True) * ROUTED_SCALE
    idx_ref[...] = jnp.concatenate(ids, axis=0)
    rank_ref[...] = jnp.concatenate(ranks, axis=0).astype(I32)


def _outproj(mix, x2, mod, w_out_bf, ln_g, ln_b, w_router_t_bf, router_bias, alpha, seq):
    t, d = x2.shape
    ne = w_router_t_bf.shape[0]
    tm = ROUTE_TOKENS
    per_batch = seq // tm
    nchunk = d // 2 // LANES
    tri = jnp.asarray(np.triu(np.ones((tm, tm), np.float32)), BF16)
    ones = jnp.ones((tm, LANES), BF16)
    rb = jnp.broadcast_to(router_bias.astype(F32)[:, None], (ne, LANES))
    c2 = lambda i: (0, 0)
    return pl.pallas_call(
        functools.partial(_outproj_kernel, alpha),
        out_shape=(jax.ShapeDtypeStruct((t, d), F32),
                   jax.ShapeDtypeStruct((t * nchunk, LANES), U32),
                   jax.ShapeDtypeStruct((TOP_K, t), I32),
                   jax.ShapeDtypeStruct((TOP_K, t), F32),
                   jax.ShapeDtypeStruct((TOP_K, t), I32),
                   jax.ShapeDtypeStruct((ne, LANES), F32)),
        grid=(t // tm,),
        in_specs=[pl.BlockSpec((tm, d), lambda i: (i, 0)),
                  pl.BlockSpec((tm, d), lambda i: (i, 0)),
                  pl.BlockSpec((1, N_MOD, d), lambda i: (i // per_batch, 0, 0)),
                  pl.BlockSpec((d, d), c2),
                  pl.BlockSpec((1, d), c2),
                  pl.BlockSpec((1, d), c2),
                  pl.BlockSpec((ne, d), c2),
                  pl.BlockSpec((ne, LANES), c2),
                  pl.BlockSpec((tm, tm), c2),
                  pl.BlockSpec((tm, LANES), c2)],
        out_specs=(pl.BlockSpec((tm, d), lambda i: (i, 0)),
                   pl.BlockSpec((tm * nchunk, LANES), lambda i: (i, 0)),
                   pl.BlockSpec((TOP_K, tm), lambda i: (0, i)),
                   pl.BlockSpec((TOP_K, tm), lambda i: (0, i)),
                   pl.BlockSpec((TOP_K, tm), lambda i: (0, i)),
                   pl.BlockSpec((ne, LANES), c2)),
        scratch_shapes=[pltpu.VMEM((ne, LANES), F32)],
        compiler_params=pltpu.CompilerParams(dimension_semantics=("arbitrary",), vmem_limit_bytes=VMEM_LIMIT),
        name="outproj",
    )(mix, x2, mod, w_out_bf, ln_g.reshape(1, d), ln_b.reshape(1, d), w_router_t_bf, rb, tri, ones)


def _slots_kernel(idx_ref, rank_ref, pstart_ref, cstart_ref, dest_ref, cdest_ref):
    ne = pstart_ref.shape[0]
    tm = idx_ref.shape[1]
    eiota = lax.broadcasted_iota(I32, (ne, tm), 0)
    reps = tm // LANES
    pstart = jnp.concatenate([pstart_ref[...]] * reps, axis=-1)
    cstart = jnp.concatenate([cstart_ref[...]] * reps, axis=-1)
    prow, crow = [], []
    for k in range(TOP_K):
        hit = eiota == idx_ref[k:k + 1, :]
        prow.append(jnp.sum(jnp.where(hit, pstart, 0), axis=0, keepdims=True))
        crow.append(jnp.sum(jnp.where(hit, cstart, 0), axis=0, keepdims=True))
    dest_ref[...] = jnp.concatenate(prow, axis=0) + rank_ref[...]
    cdest_ref[...] = jnp.concatenate(crow, axis=0) + rank_ref[...]


def _slots(idx_t, rank_t, pstart, cstart):
    k, t = idx_t.shape
    ne = pstart.shape[0]
    tm = 512
    spec = pl.BlockSpec((k, tm), lambda i: (0, i))
    const = pl.BlockSpec((ne, LANES), lambda i: (0, 0))
    return pl.pallas_call(
        _slots_kernel,
        out_shape=(jax.ShapeDtypeStruct((k, t), I32), jax.ShapeDtypeStruct((k, t), I32)),
        grid=(t // tm,),
        in_specs=[spec, spec, const, const],
        out_specs=(spec, spec),
        compiler_params=pltpu.CompilerParams(dimension_semantics=("arbitrary",)),
        name="slots",
    )(idx_t, rank_t, jnp.broadcast_to(pstart[:, None], (ne, LANES)), jnp.broadcast_to(cstart[:, None], (ne, LANES)))


def _invert_kernel(pad, cdest_ref, tok_ref):
    ntok = cdest_ref.shape[0]
    n = tok_ref.shape[0] - pad
    unroll = 8

    @pl.when(pl.program_id(0) == 0)
    def _():
        for p in range(pad):
            tok_ref[n + p] = 0

    def body(j, carry):
        for u in range(unroll):
            t = j * unroll + u
            tok_ref[cdest_ref[t]] = t * SUBLANES
        return carry

    lax.fori_loop(0, ntok // unroll, body, 0)


def _invert(cdest_flat, ntok, pad):
    n = cdest_flat.shape[0]
    return pl.pallas_call(
        functools.partial(_invert_kernel, pad),
        out_shape=jax.ShapeDtypeStruct((n + pad,), I32),
        grid=(n // ntok,),
        in_specs=[pl.BlockSpec((ntok,), lambda k: (k,), memory_space=pltpu.SMEM)],
        out_specs=pl.BlockSpec(memory_space=pltpu.SMEM),
        compiler_params=pltpu.CompilerParams(dimension_semantics=("arbitrary",)),
        name="invert",
    )(cdest_flat)


def _expert_layout(counts, bm, nblk):
    ne = counts.shape[0]
    padded = (counts + bm - 1) // bm * bm
    pend = jnp.cumsum(padded)
    pstart = pend - padded
    cstart = jnp.cumsum(counts) - counts
    blk_start = jnp.arange(nblk + 1, dtype=I32) * bm
    blk_expert = jnp.sum((pend[None, :] <= blk_start[:, None]).astype(I32), axis=1)
    onehot = (blk_expert[:, None] == jnp.arange(ne, dtype=I32)[None, :]).astype(I32)
    cbase = jnp.sum(onehot * (cstart - pstart)[None, :], axis=1) + jnp.where(blk_expert < ne, blk_start, 0)
    return (pstart.astype(I32), cstart.astype(I32), (padded // bm).astype(I32), cbase.astype(I32),
            (pend[-1:] // bm).astype(I32))


def _moe_kernel(nb_ref, pb_ref, cbase_ref, nused_ref, rows_ref, h2p_ref, wg_ref, wu_ref, wd_ref, y_hbm,
                xbuf, stage2, stage1, sem2, sem1, count):
    e = pl.program_id(0)
    nchunk = SUBLANES
    bm = stage1.shape[1] // nchunk
    nblk_total = y_hbm.shape[0] // (bm * nchunk)
    nblocks = nb_ref[e]

    def out_copy(stage, sem, slot, blk):
        dst = pl.multiple_of(blk * (bm * nchunk), bm * nchunk)
        return pltpu.make_async_copy(stage.at[slot], y_hbm.at[pl.ds(dst, stage.shape[1]), :], sem.at[slot])

    def chunk(stage, sem, which, blk):
        m = stage.shape[1] // nchunk
        done = count[which]
        slot = lax.rem(done, stage.shape[0])

        @pl.when(done >= stage.shape[0])
        def _():
            out_copy(stage, sem, slot, 0).wait()

        base = cbase_ref[blk]
        for r in range(m):
            src = pl.multiple_of(rows_ref[base + r], nchunk)
            xbuf[r * nchunk:(r + 1) * nchunk, :] = h2p_ref[pl.ds(src, nchunk), :]
        los, his = [], []
        for c in range(nchunk):
            lo, hi = _unpack_pairs_f32(xbuf[pl.ds(c, m, stride=nchunk), :])
            los.append(lo.astype(BF16))
            his.append(hi.astype(BF16))
        xrows = jnp.concatenate(los + his, axis=-1)
        g = _dot(xrows, wg_ref[0].astype(BF16))
        u = _dot(xrows, wu_ref[0].astype(BF16))
        act = (g * _sigmoid(g) * u).astype(BF16)
        _store_packed_rows(stage.at[slot], _dot(act, wd_ref[0].astype(BF16)))
        out_copy(stage, sem, slot, blk).start()
        count[which] = done + 1

    @pl.when(e == 0)
    def _():
        count[0] = 0
        count[1] = 0

    first_blk = pb_ref[e]
    npairs = nblocks // 2

    def pair(j, carry):
        chunk(stage2, sem2, 0, first_blk + 2 * j)
        return carry

    lax.fori_loop(0, npairs, pair, 0)

    @pl.when(nblocks - 2 * npairs == 1)
    def _():
        chunk(stage1, sem1, 1, first_blk + 2 * npairs)

    @pl.when(e == pl.num_programs(0) - 1)
    def _():
        for which, (stage, sem) in enumerate(((stage2, sem2), (stage1, sem1))):
            for s in range(stage.shape[0]):
                @pl.when(count[which] > s)
                def _():
                    out_copy(stage, sem, s, 0).wait()

        zeros = jnp.zeros((bm, LANES), F32)
        stage1[0] = jnp.concatenate([_pack_bf16_pairs(zeros, zeros)] * nchunk, axis=0)
        nused = nused_ref[0]

        def fill(j, carry):
            out_copy(stage1, sem1, 0, j).start()
            return carry

        def drain(j, carry):
            out_copy(stage1, sem1, 0, 0).wait()
            return carry

        lax.fori_loop(nused, nblk_total, fill, 0)
        lax.fori_loop(nused, nblk_total, drain, 0)


def _moe(h2p, rows, nblocks, pblock, cbase, nused, w_gate, w_up, w_down, bm, nslot):
    ne, d, de = w_gate.shape
    nchunk = d // 2 // LANES
    assert nchunk == SUBLANES and nslot % bm == 0
    wspec = lambda e, *_: (e, 0, 0)
    grid_spec = pltpu.PrefetchScalarGridSpec(
        num_scalar_prefetch=5,
        grid=(ne,),
        in_specs=[pl.BlockSpec(memory_space=pltpu.VMEM),
                  pl.BlockSpec((1, d, de), wspec),
                  pl.BlockSpec((1, d, de), wspec),
                  pl.BlockSpec((1, de, d), wspec)],
        out_specs=pl.BlockSpec(memory_space=pl.ANY),
        scratch_shapes=[pltpu.VMEM((2 * bm * nchunk, LANES), U32),
                        pltpu.VMEM((4, 2 * bm * nchunk, LANES), U32), pltpu.VMEM((2, bm * nchunk, LANES), U32),
                        pltpu.SemaphoreType.DMA((4,)), pltpu.SemaphoreType.DMA((2,)), pltpu.SMEM((2,), I32)],
    )
    return pl.pallas_call(
        _moe_kernel,
        out_shape=jax.ShapeDtypeStruct((nslot * nchunk, LANES), U32),
        grid_spec=grid_spec,
        compiler_params=pltpu.CompilerParams(dimension_semantics=("arbitrary",), vmem_limit_bytes=VMEM_LIMIT),
        name="moe",
    )(nblocks, pblock, cbase, nused, rows, h2p, w_gate, w_up, w_down)


def _combine_kernel(alpha, ntok, pos_ref, y_hbm, w_ref, x1_ref, mod_ref, wsg_ref, wsu_ref, wsd_ref, g_ref, b_ref,
                    o_ref, ybuf, wbc, routed, sem):
    i = pl.program_id(0)
    nsteps = pl.num_programs(0)
    tt, d = x1_ref.shape
    half = d // 2
    nchunk = SUBLANES
    rows = tt * nchunk
    slot_rows = TOP_K * rows
    slot = lax.rem(i, 2)

    def issue(step, into):
        def body(t, carry):
            for k in range(TOP_K):
                src = pl.multiple_of(pos_ref[k * ntok + step * tt + t] * nchunk, nchunk)
                dst = pl.multiple_of(into * slot_rows + k * rows + t * nchunk, nchunk)
                pltpu.make_async_copy(y_hbm.at[pl.ds(src, nchunk), :], ybuf.at[pl.ds(dst, nchunk), :],
                                      sem.at[into]).start()
            return carry

        lax.fori_loop(0, tt, body, 0)

    @pl.when(i == 0)
    def _():
        issue(0, 0)

    @pl.when(i + 1 < nsteps)
    def _():
        issue(i + 1, 1 - slot)

    x1 = x1_ref[...]
    h2 = (x1 * (1.0 + mod_ref[0, 4:5, :]) + mod_ref[0, 3:4, :]).astype(BF16)
    g = _dot(h2, wsg_ref[...])
    u = _dot(h2, wsu_ref[...])
    shared = _dot((g * _sigmoid(g) * u).astype(BF16), wsd_ref[...])
    for k in range(TOP_K):
        wbc[k] = jnp.broadcast_to(w_ref[:, k:k + 1], (tt, LANES))

    for s in range(2):
        @pl.when(slot == s)
        def _():
            pltpu.make_async_copy(y_hbm.at[pl.ds(0, slot_rows), :], ybuf.at[pl.ds(s * slot_rows, slot_rows), :],
                                  sem.at[s]).wait()
            for c in range(nchunk):
                acc_lo = acc_hi = None
                for k in range(TOP_K):
                    lo, hi = _unpack_pairs_f32(ybuf[pl.ds(s * slot_rows + k * rows + c, tt, stride=nchunk), :])
                    wk = wbc[k]
                    acc_lo = lo * wk if k == 0 else acc_lo + lo * wk
                    acc_hi = hi * wk if k == 0 else acc_hi + hi * wk
                routed[:, c * LANES:(c + 1) * LANES] = acc_lo
                routed[:, half + c * LANES: half + (c + 1) * LANES] = acc_hi

    z = alpha * x1 + mod_ref[0, 5:6, :] * (routed[...] + shared)
    o_ref[...] = _layer_norm(z, g_ref[...], b_ref[...])


def _combine(pos, y_sorted, w_tok, x1, mod, wsg_bf, wsu_bf, wsd_bf, ln_g, ln_b, alpha, seq):
    t, d = x1.shape
    ds_ = wsg_bf.shape[1]
    tt = COMBINE_TOKENS
    per_batch = seq // tt
    nchunk = d // 2 // LANES
    assert nchunk == SUBLANES
    grid_spec = pltpu.PrefetchScalarGridSpec(
        num_scalar_prefetch=1,
        grid=(t // tt,),
        in_specs=[pl.BlockSpec(memory_space=pl.ANY),
                  pl.BlockSpec((tt, TOP_K), lambda i, p: (i, 0)),
                  pl.BlockSpec((tt, d), lambda i, p: (i, 0)),
                  pl.BlockSpec((1, N_MOD, d), lambda i, p: (i // per_batch, 0, 0)),
                  pl.BlockSpec((d, ds_), lambda i, p: (0, 0)),
                  pl.BlockSpec((d, ds_), lambda i, p: (0, 0)),
                  pl.BlockSpec((ds_, d), lambda i, p: (0, 0)),
                  pl.BlockSpec((1, d), lambda i, p: (0, 0)),
                  pl.BlockSpec((1, d), lambda i, p: (0, 0))],
        out_specs=pl.BlockSpec((tt, d), lambda i, p: (i, 0)),
        scratch_shapes=[pltpu.VMEM((2 * TOP_K * tt * nchunk, LANES), U32), pltpu.VMEM((TOP_K, tt, LANES), F32),
                        pltpu.VMEM((tt, d), F32), pltpu.SemaphoreType.DMA((2,))],
    )
    return pl.pallas_call(
        functools.partial(_combine_kernel, alpha, t),
        out_shape=jax.ShapeDtypeStruct((t, d), F32),
        grid_spec=grid_spec,
        compiler_params=pltpu.CompilerParams(dimension_semantics=("arbitrary",), vmem_limit_bytes=VMEM_LIMIT),
        name="combine",
    )(pos, y_sorted, w_tok, x1, mod, wsg_bf, wsu_bf, wsd_bf, ln_g.reshape(1, d), ln_b.reshape(1, d))


def kernel(x, c, w_ada, b_ada, w_in, rel_bias, attn_gain, ret_gain, w_out, ln1_gain, ln1_bias, w_router, router_bias,
           w_gate, w_up, w_down, ws_gate, ws_up, ws_down, ln2_gain, ln2_bias):
    batch, seq, d = x.shape
    depth = w_ada.shape[0]
    alpha = (2.0 * depth) ** 0.25
    ne = w_router.shape[-1]
    t = batch * seq
    bm = MOE_BLOCK
    nslot = t * TOP_K + ne * bm
    xt = x.reshape(t, d)
    c_pad = jnp.zeros((8, d), F32).at[:batch].set(c)
    for l in range(depth):
        mod = _ada(c_pad, w_ada[l], b_ada[l])[:batch].reshape(batch, N_MOD, d)
        proj = _inproj(xt, mod, w_in[l], seq)
        mix = _mixer(proj, rel_bias[l], attn_gain[l], ret_gain[l], batch, seq)
        x1, h2p, idx_t, w_t, rank_t, cnt = _outproj(mix, xt, mod, w_out[l].astype(BF16), ln1_gain[l], ln1_bias[l],
                                                    w_router[l].T.astype(BF16), router_bias[l], alpha, seq)
        pstart, cstart, nblocks, cbase, nused = _expert_layout(cnt[:, 0].astype(I32), bm, nslot // bm)
        dest, cdest = _slots(idx_t, rank_t, pstart, cstart)
        rows = _invert(cdest.reshape(-1), t, 2 * bm)
        y_sorted = _moe(h2p, rows, nblocks, pstart // bm, cbase, nused, w_gate[l], w_up[l], w_down[l], bm, nslot)
        xt = _combine(dest.reshape(-1), y_sorted, w_t.T, x1, mod, ws_gate[l].astype(BF16), ws_up[l].astype(BF16),
                      ws_down[l].astype(BF16), ln2_gain[l], ln2_bias[l], alpha, seq)
    return xt.reshape(batch, seq, d)
```

```python
import functools

import jax
import jax.numpy as jnp
import numpy as np
from jax import lax
from jax.experimental import pallas as pl
from jax.experimental.pallas import tpu as pltpu

F32 = jnp.float32
BF16 = jnp.bfloat16
U32 = jnp.uint32
I32 = jnp.int32

CHUNK = 64
LEFT_CHUNKS = 8
MAX_REL = 256
ATTN_HEADS = 8
HEAD_DIM = 128
RET_HEADS = 8
RET_KEY_DIM = 64
RET_VALUE_DIM = 128
ROPE_BASE = 10000.0
N_GROUPS = 8
TOPK_GROUPS = 4
TOP_K = 8
ROUTED_SCALE = 2.5
EPS = 1e-5
N_MOD = 6
LANES = 128
SUBLANES = 8

SEQ_BLOCK = 256
ROUTE_TOKENS = 512
ROUTE_SUBTILE = 256
MOE_RUN = 256
MOE_RING = 3
COMBINE_TOKENS = 128
NEG_BIG = -1e30
VMEM_LIMIT = 56 * 1024 * 1024


def _sigmoid(v):
    return 1.0 / (1.0 + jnp.exp(-v))


def _dot(a, b):
    return jnp.dot(a, b, preferred_element_type=F32)


def _dot_t(a, b):
    return lax.dot_general(a, b, (((1,), (1,)), ((), ())), preferred_element_type=F32)


def _ada_kernel(c_ref, w_ref, b_ref, o_ref):
    c = c_ref[...]
    s = (c * _sigmoid(c)).astype(BF16)
    o_ref[...] = _dot(s, w_ref[...].astype(BF16)) + b_ref[...]


def _ada(c_pad, w_ada, b_ada):
    d, n = w_ada.shape
    tn = 1024
    return pl.pallas_call(
        _ada_kernel,
        out_shape=jax.ShapeDtypeStruct((8, n), F32),
        grid=(n // tn,),
        in_specs=[pl.BlockSpec((8, d), lambda j: (0, 0)),
                  pl.BlockSpec((d, tn), lambda j: (0, j)),
                  pl.BlockSpec((1, tn), lambda j: (0, j))],
        out_specs=pl.BlockSpec((8, tn), lambda j: (0, j)),
        compiler_params=pltpu.CompilerParams(dimension_semantics=("arbitrary",), vmem_limit_bytes=VMEM_LIMIT),
        name="ada",
    )(c_pad, w_ada, b_ada.reshape(1, n))


def _inproj_kernel(x_ref, mod_ref, w_ref, o_ref, wbf_ref):
    @pl.when(pl.program_id(1) == 0)
    def _():
        wbf_ref[...] = w_ref[...].astype(BF16)

    shift = mod_ref[0, 0:1, :]
    scale = mod_ref[0, 1:2, :]
    h = (x_ref[...] * (1.0 + scale) + shift).astype(BF16)
    o_ref[...] = _dot(h, wbf_ref[...]).astype(o_ref.dtype)


def _inproj(x2, mod, w_in, seq):
    t, d = x2.shape
    n = w_in.shape[1]
    tm, tn = 1024, 1024
    per_batch = seq // tm
    return pl.pallas_call(
        _inproj_kernel,
        out_shape=jax.ShapeDtypeStruct((t, n), BF16),
        grid=(n // tn, t // tm),
        in_specs=[pl.BlockSpec((tm, d), lambda j, i: (i, 0)),
                  pl.BlockSpec((1, N_MOD, d), lambda j, i: (i // per_batch, 0, 0)),
                  pl.BlockSpec((d, tn), lambda j, i: (0, j))],
        out_specs=pl.BlockSpec((tm, tn), lambda j, i: (i, j)),
        scratch_shapes=[pltpu.VMEM((d, tn), BF16)],
        compiler_params=pltpu.CompilerParams(dimension_semantics=("arbitrary", "arbitrary"),
                                             vmem_limit_bytes=VMEM_LIMIT),
        name="inproj",
    )(x2, mod, w_in)


def _ret_decay_consts(blk):
    h = np.arange(RET_HEADS, dtype=np.float64)
    log_g = np.log(1.0 - 2.0 ** (-5.0 - h))
    n = np.arange(blk, dtype=np.float64)
    diff = n[:, None] - n[None, :]
    same = (n[:, None] // CHUNK) == (n[None, :] // CHUNK)
    later = (n[:, None] // CHUNK) > (n[None, :] // CHUNK)
    expo = np.where(same, np.abs(diff), diff)
    kscale = RET_KEY_DIM ** -0.5
    decay = np.where(same | later, np.exp(log_g[:, None, None] * expo[None]), 0.0) * kscale
    xi = np.exp(log_g[:, None] * (n[None, :] + 1.0))
    zeta = np.exp(log_g[:, None] * (blk - 1.0 - n[None, :])) * kscale
    g_blk = np.exp(log_g * blk)
    xi = np.broadcast_to(xi[:, :, None], (RET_HEADS, blk, LANES))
    zeta = np.broadcast_to(zeta[:, :, None], (RET_HEADS, blk, LANES))
    return (jnp.asarray(decay, F32), jnp.asarray(xi, F32), jnp.asarray(zeta, F32), [float(v) for v in g_blk])


def _mixer_kernel(g_blk, qa, ka0, ka1, ka2, va0, va1, va2, qb, kb, vb, gb, bias, cos, sin, decay, xi, zeta,
                  again, rgain, out, state, oa):
    i = pl.program_id(1)
    blk = qa.shape[0]

    @pl.when(i == 0)
    def _():
        state[...] = jnp.zeros_like(state)

    ok0 = i >= 2
    ok1 = i >= 1
    scale = HEAD_DIM ** -0.5
    ssq = jnp.zeros((blk, 1), F32)

    def raw_scores(h):
        sl = slice(h * HEAD_DIM, (h + 1) * HEAD_DIM)
        q = (qa[:, sl].astype(F32) * scale).astype(BF16)
        return _dot_t(q, ka0[:, sl]), _dot_t(q, ka1[:, sl]), _dot_t(q, ka2[:, sl])

    pending = raw_scores(0)
    for h in range(ATTN_HEADS):
        sl = slice(h * HEAD_DIM, (h + 1) * HEAD_DIM)
        r0, r1, r2 = pending
        if h + 1 < ATTN_HEADS:
            pending = raw_scores(h + 1)
        s0 = jnp.where(ok0, r0 + bias[h, :, 0:blk], NEG_BIG)
        s1 = jnp.where(ok1, r1 + bias[h, :, blk:2 * blk], NEG_BIG)
        s2 = r2 + bias[h, :, 2 * blk:3 * blk]
        m = jnp.max(jnp.maximum(jnp.maximum(s0, s1), s2), axis=-1, keepdims=True)
        p0 = jnp.exp(s0 - m)
        p1 = jnp.exp(s1 - m)
        p2 = jnp.exp(s2 - m)
        denom = jnp.sum(p0 + p1 + p2, axis=-1, keepdims=True)
        o = _dot(p0.astype(BF16), va0[:, sl]) + _dot(p1.astype(BF16), va1[:, sl]) + _dot(p2.astype(BF16), va2[:, sl])
        o = o * (1.0 / denom)
        oa[:, sl] = o
        ssq = ssq + jnp.sum(o * o, axis=-1, keepdims=True)
    width_a = ATTN_HEADS * HEAD_DIM
    inv_rms = lax.rsqrt(ssq * (1.0 / width_a) + EPS)
    out[:, 0:width_a] = (oa[...] * inv_rms * again[...]).astype(out.dtype)

    cosv = cos[...]
    sinv = sin[...]
    lane = lax.broadcasted_iota(jnp.int32, (blk, LANES), 1)
    first_half = (lane % RET_KEY_DIM) < (RET_KEY_DIM // 2)
    low_head = lane < RET_KEY_DIM

    def rope(v):
        rot = jnp.where(first_half, pltpu.roll(v, LANES - RET_KEY_DIM // 2, 1), pltpu.roll(v, RET_KEY_DIM // 2, 1))
        return v * cosv + rot * sinv

    for j in range(RET_HEADS // 2):
        sl2 = slice(j * LANES, (j + 1) * LANES)
        qr = rope(qb[:, sl2].astype(F32))
        k2 = rope(kb[:, sl2].astype(F32)).astype(BF16)
        for hh in range(2):
            h = 2 * j + hh
            slv = slice(h * RET_VALUE_DIM, (h + 1) * RET_VALUE_DIM)
            qm = jnp.where(low_head if hh == 0 else jnp.logical_not(low_head), qr, 0.0).astype(BF16)
            v = vb[:, slv]
            st = state[h]
            scores = _dot_t(qm, k2) * decay[h]
            ret = _dot(scores.astype(BF16), v) + _dot(qm, st.astype(BF16)) * xi[h]
            zv = (v.astype(F32) * zeta[h]).astype(BF16)
            upd = lax.dot_general(k2, zv, (((0,), (0,)), ((), ())), preferred_element_type=F32)
            state[h] = g_blk[h] * st + upd
            mu = jnp.mean(ret, axis=-1, keepdims=True)
            cen = ret - mu
            var = jnp.mean(cen * cen, axis=-1, keepdims=True)
            gate = gb[:, slv].astype(F32)
            yb = gate * _sigmoid(gate) * (cen * lax.rsqrt(var + EPS)) * rgain[:, slv]
            out[:, width_a + h * RET_VALUE_DIM: width_a + (h + 1) * RET_VALUE_DIM] = yb.astype(out.dtype)


def _attn_bias_table(rel_bias, blk):
    nk = 3 * blk
    period = nk + blk
    offs = np.concatenate([np.arange(nk), np.zeros((1,), np.int64), np.arange(-(blk - 1), 0)])
    rel_idx = np.clip(2 * blk - offs, -(CHUNK - 1), MAX_REL) + (CHUNK - 1)
    line = rel_bias[:, jnp.asarray(rel_idx)].astype(F32)
    heads = line.shape[0]
    flat = jnp.broadcast_to(line[:, None, :], (heads, blk, period)).reshape(heads, blk * period)
    tab = flat[:, :blk * (period - 1)].reshape(heads, blk, period - 1)[:, :, :nk]
    r = np.arange(blk)[:, None]
    c = np.arange(nk)[None, :]
    qc = r // CHUNK + (2 * blk) // CHUNK
    kc = c // CHUNK
    in_band = (kc <= qc) & (kc >= qc - LEFT_CHUNKS)
    return jnp.where(jnp.asarray(in_band)[None], tab, NEG_BIG)


def _rope_tables(seq):
    half = RET_KEY_DIM // 2
    inv = ROPE_BASE ** (-jnp.arange(half, dtype=F32) / half)
    ang = jnp.arange(seq, dtype=F32)[:, None] * inv[None, :]
    cos, sin = jnp.cos(ang), jnp.sin(ang)
    reps = LANES // RET_KEY_DIM
    cos_t = jnp.tile(jnp.concatenate([cos, cos], axis=-1), (1, reps))
    sin_t = jnp.tile(jnp.concatenate([-sin, sin], axis=-1), (1, reps))
    return cos_t, sin_t


def _mixer(proj, rel_bias, attn_gain, ret_gain, batch, seq):
    t = proj.shape[0]
    blk = SEQ_BLOCK
    assert 2 * blk == LEFT_CHUNKS * CHUNK and seq % blk == 0
    nb = seq // blk
    wa = ATTN_HEADS * HEAD_DIM
    wqk = RET_HEADS * RET_KEY_DIM
    wv = RET_HEADS * RET_VALUE_DIM
    assert wa == wv == 2 * wqk
    bias = _attn_bias_table(rel_bias, blk)
    cos_t, sin_t = _rope_tables(seq)
    decay, xi, zeta, g_blk = _ret_decay_consts(blk)

    def row(b, i):
        return b * nb + i

    def kspec(back, col):
        return pl.BlockSpec((blk, wa), lambda b, i: (row(b, jnp.maximum(i - back, 0)), col))

    const3 = lambda b, i: (0, 0, 0)
    in_specs = [
        pl.BlockSpec((blk, wa), lambda b, i: (row(b, i), 0)),
        kspec(2, 1), kspec(1, 1), kspec(0, 1),
        kspec(2, 2), kspec(1, 2), kspec(0, 2),
        pl.BlockSpec((blk, wqk), lambda b, i: (row(b, i), 3 * wa // wqk)),
        pl.BlockSpec((blk, wqk), lambda b, i: (row(b, i), 3 * wa // wqk + 1)),
        pl.BlockSpec((blk, wv), lambda b, i: (row(b, i), (3 * wa + 2 * wqk) // wv)),
        pl.BlockSpec((blk, wv), lambda b, i: (row(b, i), (3 * wa + 2 * wqk) // wv + 1)),
        pl.BlockSpec((ATTN_HEADS, blk, 3 * blk), const3),
        pl.BlockSpec((blk, LANES), lambda b, i: (i, 0)),
        pl.BlockSpec((blk, LANES), lambda b, i: (i, 0)),
        pl.BlockSpec((RET_HEADS, blk, blk), const3),
        pl.BlockSpec((RET_HEADS, blk, LANES), const3),
        pl.BlockSpec((RET_HEADS, blk, LANES), const3),
        pl.BlockSpec((1, wa), lambda b, i: (0, 0)),
        pl.BlockSpec((1, wv), lambda b, i: (0, 0)),
    ]
    return pl.pallas_call(
        functools.partial(_mixer_kernel, g_blk),
        out_shape=jax.ShapeDtypeStruct((t, wa + wv), BF16),
        grid=(batch, nb),
        in_specs=in_specs,
        out_specs=pl.BlockSpec((blk, wa + wv), lambda b, i: (row(b, i), 0)),
        scratch_shapes=[pltpu.VMEM((RET_HEADS, LANES, RET_VALUE_DIM), F32), pltpu.VMEM((blk, wa), F32)],
        compiler_params=pltpu.CompilerParams(dimension_semantics=("arbitrary", "arbitrary"),
                                             vmem_limit_bytes=VMEM_LIMIT),
        name="mixer",
    )(proj, proj, proj, proj, proj, proj, proj, proj, proj, proj, proj, bias, cos_t, sin_t, decay, xi, zeta,
      attn_gain.reshape(1, wa), ret_gain.reshape(1, wv))


def _layer_norm(z, gain, bias):
    mu = jnp.mean(z, axis=-1, keepdims=True)
    cen = z - mu
    var = jnp.mean(cen * cen, axis=-1, keepdims=True)
    return cen * lax.rsqrt(var + EPS) * gain + bias


def _pack_bf16_pairs(lo, hi):
    return pltpu.pack_elementwise([lo, hi], packed_dtype=BF16)


def _unpack_pairs_f32(words):
    lo = pltpu.unpack_elementwise(words, index=0, packed_dtype=BF16, unpacked_dtype=F32)
    hi = pltpu.unpack_elementwise(words, index=1, packed_dtype=BF16, unpacked_dtype=F32)
    return lo, hi


def _store_packed_rows(ref, val):
    tm, d = val.shape
    half = d // 2
    nchunk = half // LANES
    for c in range(nchunk):
        words = _pack_bf16_pairs(val[:, c * LANES:(c + 1) * LANES], val[:, half + c * LANES: half + (c + 1) * LANES])
        ref[pl.ds(c, tm, stride=nchunk), :] = words


def _route_tile(scores_t, bias_col):
    ne, tm = scores_t.shape
    per_group = ne // N_GROUPS
    biased = scores_t + bias_col
    neg_inf = -jnp.inf
    iota_g = lax.broadcasted_iota(I32, (per_group, tm), 0)
    gscore = []
    for g in range(N_GROUPS):
        b = biased[g * per_group:(g + 1) * per_group, :]
        m1 = jnp.max(b, axis=0, keepdims=True)
        first = jnp.min(jnp.where(b == m1, iota_g, per_group), axis=0, keepdims=True)
        m2 = jnp.max(jnp.where(iota_g == first, neg_inf, b), axis=0, keepdims=True)
        gscore.append(m1 + m2)
    masked_parts = []
    for g in range(N_GROUPS):
        beaten = jnp.zeros((1, tm), I32)
        for o in range(N_GROUPS):
            if o == g:
                continue
            wins = (gscore[o] >= gscore[g]) if o < g else (gscore[o] > gscore[g])
            beaten = beaten + wins.astype(I32)
        keep = beaten < TOPK_GROUPS
        masked_parts.append(jnp.where(keep, biased[g * per_group:(g + 1) * per_group, :], neg_inf))
    masked = jnp.concatenate(masked_parts, axis=0)
    eiota = lax.broadcasted_iota(I32, (ne, tm), 0)
    ids, vals = [], []
    for _ in range(TOP_K):
        m = jnp.max(masked, axis=0, keepdims=True)
        first = jnp.min(jnp.where(masked == m, eiota, ne), axis=0, keepdims=True)
        hit = eiota == first
        vals.append(jnp.sum(jnp.where(hit, scores_t, 0.0), axis=0, keepdims=True))
        ids.append(first)
        masked = jnp.where(hit, neg_inf, masked)
    return ids, vals


def _outproj_kernel(alpha, mix_ref, x_ref, mod_ref, wout_ref, g_ref, b_ref, wrt_ref, rb_ref, tri_ref, ones_ref,
                    x1_ref, h2p_ref, idx_ref, w_ref, rank_ref, cnt_ref, carry):
    i = pl.program_id(0)
    tm, d = x_ref.shape
    nchunk = d // 2 // LANES
    ne = wrt_ref.shape[0]
    sub = ROUTE_SUBTILE

    @pl.when(i == 0)
    def _():
        carry[...] = jnp.zeros_like(carry)

    gate1 = mod_ref[0, 2:3, :]
    scale2 = 1.0 + mod_ref[0, 4:5, :]
    shift2 = mod_ref[0, 3:4, :]
    eiota = lax.broadcasted_iota(I32, (ne, LANES), 0)
    counted = carry[...]
    for part in range(tm // sub):
        rs = slice(part * sub, (part + 1) * sub)
        y = _dot(mix_ref[rs, :], wout_ref[...])
        x1 = _layer_norm(alpha * x_ref[rs, :] + gate1 * y, g_ref[...], b_ref[...])
        x1_ref[rs, :] = x1
        h2 = x1 * scale2 + shift2
        _store_packed_rows(h2p_ref.at[pl.ds(part * sub * nchunk, sub * nchunk), :], h2)
        logits_t = _dot_t(wrt_ref[...], h2.astype(BF16))
        for q in range(sub // LANES):
            ls = slice(part * sub + q * LANES, part * sub + (q + 1) * LANES)
            scores_t = _sigmoid(logits_t[:, q * LANES:(q + 1) * LANES])
            ids, vals = _route_tile(scores_t, rb_ref[:, 0:1])
            chosen = jnp.zeros((ne, LANES), F32)
            for k in range(TOP_K):
                chosen = chosen + (eiota == ids[k]).astype(F32)
            chosen_bf = chosen.astype(BF16)
            before = _dot(chosen_bf, tri_ref[...]) - chosen + counted
            ranks = [jnp.sum(jnp.where(eiota == ids[k], before, 0.0), axis=0, keepdims=True)
                     for k in range(TOP_K)]
            counted = counted + _dot(chosen_bf, ones_ref[...])
            wsel = jnp.concatenate(vals, axis=0)
            w_ref[:, ls] = wsel / jnp.sum(wsel, axis=0, keepdims=True) * ROUTED_SCALE
            idx_ref[:, ls] = jnp.concatenate(ids, axis=0)
            rank_ref[:, ls] = jnp.concatenate(ranks, axis=0).astype(I32)
    carry[...] = counted
    cnt_ref[...] = counted


def _outproj(mix, x2, mod, w_out_bf, ln_g, ln_b, w_router_t_bf, router_bias, alpha, seq):
    t, d = x2.shape
    ne = w_router_t_bf.shape[0]
    tm = ROUTE_TOKENS
    per_batch = seq // tm
    nchunk = d // 2 // LANES
    tri = jnp.asarray(np.triu(np.ones((LANES, LANES), np.float32)), BF16)
    ones = jnp.ones((LANES, LANES), BF16)
    rb = jnp.broadcast_to(router_bias.astype(F32)[:, None], (ne, LANES))
    c2 = lambda i: (0, 0)
    return pl.pallas_call(
        functools.partial(_outproj_kernel, alpha),
        out_shape=(jax.ShapeDtypeStruct((t, d), F32),
                   jax.ShapeDtypeStruct((t * nchunk, LANES), U32),
                   jax.ShapeDtypeStruct((TOP_K, t), I32),
                   jax.ShapeDtypeStruct((TOP_K, t), F32),
                   jax.ShapeDtypeStruct((TOP_K, t), I32),
                   jax.ShapeDtypeStruct((ne, LANES), F32)),
        grid=(t // tm,),
        in_specs=[pl.BlockSpec((tm, d), lambda i: (i, 0)),
                  pl.BlockSpec((tm, d), lambda i: (i, 0)),
                  pl.BlockSpec((1, N_MOD, d), lambda i: (i // per_batch, 0, 0)),
                  pl.BlockSpec((d, d), c2),
                  pl.BlockSpec((1, d), c2),
                  pl.BlockSpec((1, d), c2),
                  pl.BlockSpec((ne, d), c2),
                  pl.BlockSpec((ne, LANES), c2),
                  pl.BlockSpec((LANES, LANES), c2),
                  pl.BlockSpec((LANES, LANES), c2)],
        out_specs=(pl.BlockSpec((tm, d), lambda i: (i, 0)),
                   pl.BlockSpec((tm * nchunk, LANES), lambda i: (i, 0)),
                   pl.BlockSpec((TOP_K, tm), lambda i: (0, i)),
                   pl.BlockSpec((TOP_K, tm), lambda i: (0, i)),
                   pl.BlockSpec((TOP_K, tm), lambda i: (0, i)),
                   pl.BlockSpec((ne, LANES), c2)),
        scratch_shapes=[pltpu.VMEM((ne, LANES), F32)],
        compiler_params=pltpu.CompilerParams(dimension_semantics=("arbitrary",), vmem_limit_bytes=VMEM_LIMIT),
        name="outproj",
    )(mix, x2, mod, w_out_bf, ln_g.reshape(1, d), ln_b.reshape(1, d), w_router_t_bf, rb, tri, ones)


def _slots_kernel(idx_ref, rank_ref, start_ref, pos_ref):
    ne = start_ref.shape[0]
    tm = idx_ref.shape[1]
    eiota = lax.broadcasted_iota(I32, (ne, tm), 0)
    start = jnp.concatenate([start_ref[...]] * (tm // LANES), axis=-1)
    rows = []
    for k in range(TOP_K):
        rows.append(jnp.sum(jnp.where(eiota == idx_ref[k:k + 1, :], start, 0), axis=0, keepdims=True))
    pos_ref[...] = jnp.concatenate(rows, axis=0) + rank_ref[...]


def _slots(idx_t, rank_t, start):
    k, t = idx_t.shape
    ne = start.shape[0]
    tm = 512
    spec = pl.BlockSpec((k, tm), lambda i: (0, i))
    return pl.pallas_call(
        _slots_kernel,
        out_shape=jax.ShapeDtypeStruct((k, t), I32),
        grid=(t // tm,),
        in_specs=[spec, spec, pl.BlockSpec((ne, LANES), lambda i: (0, 0))],
        out_specs=spec,
        compiler_params=pltpu.CompilerParams(dimension_semantics=("arbitrary",)),
        name="slots",
    )(idx_t, rank_t, jnp.broadcast_to(start[:, None], (ne, LANES)))


def _invert_kernel(pos_ref, src_ref, dst_ref):
    ntok = pos_ref.shape[0]
    k = pl.program_id(0)
    unroll = 8

    def body(j, carry):
        for u in range(unroll):
            t = j * unroll + u
            p = pos_ref[t]
            src_ref[p] = t * SUBLANES
            dst_ref[p] = k * ntok + t
        return carry

    lax.fori_loop(0, ntok // unroll, body, 0)


def _invert(pos_flat, ntok):
    n = pos_flat.shape[0]
    smem = pl.BlockSpec(memory_space=pltpu.SMEM)
    return pl.pallas_call(
        _invert_kernel,
        out_shape=(jax.ShapeDtypeStruct((n,), I32), jax.ShapeDtypeStruct((n,), I32)),
        grid=(n // ntok,),
        in_specs=[pl.BlockSpec((ntok,), lambda k: (k,), memory_space=pltpu.SMEM)],
        out_specs=(smem, smem),
        compiler_params=pltpu.CompilerParams(dimension_semantics=("arbitrary",)),
        name="invert",
    )(pos_flat)


def _moe_kernel(cnt_ref, start_ref, src_ref, dst_ref, h2p_ref, wg_ref, wu_ref, wd_ref, out_hbm,
                xbuf, stage, sem, state):
    e = pl.program_id(0)
    nchunk = SUBLANES
    ring = stage.shape[0]
    m = stage.shape[1] // nchunk
    nlist = src_ref.shape[0]
    n = cnt_ref[e]

    def wait_slot(slot):
        pltpu.make_async_copy(stage.at[slot], out_hbm.at[pl.ds(0, m * nchunk), :], sem.at[slot]).wait()

    def send_rows(slot, base, occupied):
        for r in range(m):
            row = jnp.where(r < occupied, dst_ref[jnp.minimum(base + r, nlist - 1)], nlist + slot * m + r)
            pltpu.make_async_copy(stage.at[slot, pl.ds(r * nchunk, nchunk), :],
                                  out_hbm.at[pl.ds(pl.multiple_of(row * nchunk, nchunk), nchunk), :],
                                  sem.at[slot]).start()

    @pl.when(e == 0)
    def _():
        zeros = jnp.zeros((m, LANES), F32)
        packed = jnp.concatenate([_pack_bf16_pairs(zeros, zeros)] * nchunk, axis=0)
        for s in range(ring):
            stage[s] = packed
        for s in range(ring):
            fill = pltpu.make_async_copy(stage.at[s], out_hbm.at[pl.ds((nlist + s * m) * nchunk, m * nchunk), :],
                                         sem.at[s])
            fill.start()
            fill.wait()
        state[0] = 1
        state[1] = 0
        state[2] = 0

    def run(j, carry):
        done = state[0]
        slot = lax.rem(done, ring)
        base = start_ref[e] + j * m

        @pl.when(done >= ring)
        def _():
            wait_slot(slot)

        send_rows(lax.rem(done + ring - 1, ring), state[1], state[2])
        for r in range(m):
            src = pl.multiple_of(src_ref[jnp.minimum(base + r, nlist - 1)], nchunk)
            xbuf[r * nchunk:(r + 1) * nchunk, :] = h2p_ref[pl.ds(src, nchunk), :]
        los, his = [], []
        for c in range(nchunk):
            lo, hi = _unpack_pairs_f32(xbuf[pl.ds(c, m, stride=nchunk), :])
            los.append(lo.astype(BF16))
            his.append(hi.astype(BF16))
        xrows = jnp.concatenate(los + his, axis=-1)
        g = _dot(xrows, wg_ref[0].astype(BF16))
        u = _dot(xrows, wu_ref[0].astype(BF16))
        act = (g * _sigmoid(g) * u).astype(BF16)
        _store_packed_rows(stage.at[slot], _dot(act, wd_ref[0].astype(BF16)))
        state[0] = done + 1
        state[1] = base
        state[2] = jnp.minimum(n - j * m, m)
        return carry

    lax.fori_loop(0, (n + m - 1) // m, run, 0)

    @pl.when(e == pl.num_programs(0) - 1)
    def _():
        done = state[0]
        send_rows(lax.rem(done + ring - 1, ring), state[1], state[2])
        for s in range(ring):
            @pl.when(done > s)
            def _():
                wait_slot(s)


def _moe(h2p, src, dst, counts, starts, w_gate, w_up, w_down):
    ne, d, de = w_gate.shape
    nchunk = d // 2 // LANES
    assert nchunk == SUBLANES
    m, ring = MOE_RUN, MOE_RING
    nlist = src.shape[0]
    wspec = lambda e, *_: (e, 0, 0)
    grid_spec = pltpu.PrefetchScalarGridSpec(
        num_scalar_prefetch=4,
        grid=(ne,),
        in_specs=[pl.BlockSpec(memory_space=pltpu.VMEM),
                  pl.BlockSpec((1, d, de), wspec),
                  pl.BlockSpec((1, d, de), wspec),
                  pl.BlockSpec((1, de, d), wspec)],
        out_specs=pl.BlockSpec(memory_space=pl.ANY),
        scratch_shapes=[pltpu.VMEM((m * nchunk, LANES), U32), pltpu.VMEM((ring, m * nchunk, LANES), U32),
                        pltpu.SemaphoreType.DMA((ring,)), pltpu.SMEM((3,), I32)],
    )
    return pl.pallas_call(
        _moe_kernel,
        out_shape=jax.ShapeDtypeStruct(((nlist + ring * m) * nchunk, LANES), U32),
        grid_spec=grid_spec,
        compiler_params=pltpu.CompilerParams(dimension_semantics=("arbitrary",), vmem_limit_bytes=VMEM_LIMIT),
        name="moe",
    )(counts, starts, src, dst, h2p, w_gate, w_up, w_down)


def _combine_kernel(alpha, *refs):
    y_refs = refs[:TOP_K]
    w_ref, x1_ref, mod_ref, wsg_ref, wsu_ref, wsd_ref, g_ref, b_ref, o_ref, wbc, routed = refs[TOP_K:]
    tt, d = x1_ref.shape
    half = d // 2
    nchunk = SUBLANES

    x1 = x1_ref[...]
    h2 = (x1 * (1.0 + mod_ref[0, 4:5, :]) + mod_ref[0, 3:4, :]).astype(BF16)
    g = _dot(h2, wsg_ref[...])
    u = _dot(h2, wsu_ref[...])
    shared = _dot((g * _sigmoid(g) * u).astype(BF16), wsd_ref[...])
    for k in range(TOP_K):
        wbc[k] = jnp.broadcast_to(w_ref[:, k:k + 1], (tt, LANES))

    for c in range(nchunk):
        acc_lo = acc_hi = None
        for k in range(TOP_K):
            lo, hi = _unpack_pairs_f32(y_refs[k][pl.ds(c, tt, stride=nchunk), :])
            wk = wbc[k]
            acc_lo = lo * wk if k == 0 else acc_lo + lo * wk
            acc_hi = hi * wk if k == 0 else acc_hi + hi * wk
        routed[:, c * LANES:(c + 1) * LANES] = acc_lo
        routed[:, half + c * LANES: half + (c + 1) * LANES] = acc_hi

    z = alpha * x1 + mod_ref[0, 5:6, :] * (routed[...] + shared)
    o_ref[...] = _layer_norm(z, g_ref[...], b_ref[...])


def _combine(y_ranked, w_tok, x1, mod, wsg_bf, wsu_bf, wsd_bf, ln_g, ln_b, alpha, seq):
    t, d = x1.shape
    ds_ = wsg_bf.shape[1]
    tt = COMBINE_TOKENS
    per_batch = seq // tt
    nsteps = t // tt
    nchunk = d // 2 // LANES
    assert nchunk == SUBLANES
    c2 = lambda i: (0, 0)
    y_specs = [pl.BlockSpec((tt * nchunk, LANES), functools.partial(lambda k, i: (k * nsteps + i, 0), k))
               for k in range(TOP_K)]
    return pl.pallas_call(
        functools.partial(_combine_kernel, alpha),
        out_shape=jax.ShapeDtypeStruct((t, d), F32),
        grid=(nsteps,),
        in_specs=y_specs + [pl.BlockSpec((tt, TOP_K), lambda i: (i, 0)),
                            pl.BlockSpec((tt, d), lambda i: (i, 0)),
                            pl.BlockSpec((1, N_MOD, d), lambda i: (i // per_batch, 0, 0)),
                            pl.BlockSpec((d, ds_), c2),
                            pl.BlockSpec((d, ds_), c2),
                            pl.BlockSpec((ds_, d), c2),
                            pl.BlockSpec((1, d), c2),
                            pl.BlockSpec((1, d), c2)],
        out_specs=pl.BlockSpec((tt, d), lambda i: (i, 0)),
        scratch_shapes=[pltpu.VMEM((TOP_K, tt, LANES), F32), pltpu.VMEM((tt, d), F32)],
        compiler_params=pltpu.CompilerParams(dimension_semantics=("arbitrary",), vmem_limit_bytes=VMEM_LIMIT),
        name="combine",
    )(*([y_ranked] * TOP_K), w_tok, x1, mod, wsg_bf, wsu_bf, wsd_bf, ln_g.reshape(1, d), ln_b.reshape(1, d))


def kernel(x, c, w_ada, b_ada, w_in, rel_bias, attn_gain, ret_gain, w_out, ln1_gain, ln1_bias, w_router, router_bias,
           w_gate, w_up, w_down, ws_gate, ws_up, ws_down, ln2_gain, ln2_bias):
    batch, seq, d = x.shape
    depth = w_ada.shape[0]
    alpha = (2.0 * depth) ** 0.25
    t = batch * seq
    assert t % COMBINE_TOKENS == 0
    xt = x.reshape(t, d)
    c_pad = jnp.zeros((8, d), F32).at[:batch].set(c)
    for l in range(depth):
        mod = _ada(c_pad, w_ada[l], b_ada[l])[:batch].reshape(batch, N_MOD, d)
        proj = _inproj(xt, mod, w_in[l], seq)
        mix = _mixer(proj, rel_bias[l], attn_gain[l], ret_gain[l], batch, seq)
        x1, h2p, idx_t, w_t, rank_t, cnt = _outproj(mix, xt, mod, w_out[l].astype(BF16), ln1_gain[l], ln1_bias[l],
                                                    w_router[l].T.astype(BF16), router_bias[l], alpha, seq)
        counts = cnt[:, 0].astype(I32)
        starts = jnp.cumsum(counts) - counts
        pos = _slots(idx_t, rank_t, starts)
        src, dst = _invert(pos.reshape(-1), t)
        y_ranked = _moe(h2p, src, dst, counts, starts, w_gate[l], w_up[l], w_down[l])
        xt = _combine(y_ranked, w_t.T, x1, mod, ws_gate[l].astype(BF16), ws_up[l].astype(BF16),
                      ws_down[l].astype(BF16), ln2_gain[l], ln2_bias[l], alpha, seq)
    return xt.reshape(batch, seq, d)
```

```python
import functools

import jax
import jax.numpy as jnp
import numpy as np
from jax import lax
from jax.experimental import pallas as pl
from jax.experimental.pallas import tpu as pltpu

F32 = jnp.float32
BF16 = jnp.bfloat16
U32 = jnp.uint32
I32 = jnp.int32

CHUNK = 64
LEFT_CHUNKS = 8
MAX_REL = 256
ATTN_HEADS = 8
HEAD_DIM = 128
RET_HEADS = 8
RET_KEY_DIM = 64
RET_VALUE_DIM = 128
ROPE_BASE = 10000.0
N_GROUPS = 8
TOPK_GROUPS = 4
TOP_K = 8
ROUTED_SCALE = 2.5
EPS = 1e-5
N_MOD = 6
LANES = 128
SUBLANES = 8

SEQ_BLOCK = 256
ROUTE_TOKENS = 512
ROUTE_SUBTILE = 256
MOE_RUN = 256
MOE_SHORT_RUN = 64
MOE_RING = 3
COMBINE_TOKENS = 128
NEG_BIG = -1e30
VMEM_LIMIT = 56 * 1024 * 1024


def _sigmoid(v):
    return 1.0 / (1.0 + jnp.exp(-v))


def _dot(a, b):
    return jnp.dot(a, b, preferred_element_type=F32)


def _dot_t(a, b):
    return lax.dot_general(a, b, (((1,), (1,)), ((), ())), preferred_element_type=F32)


def _ada_kernel(c_ref, w_ref, b_ref, o_ref):
    c = c_ref[...]
    s = (c * _sigmoid(c)).astype(BF16)
    o_ref[...] = _dot(s, w_ref[...].astype(BF16)) + b_ref[...]


def _ada(c_pad, w_ada, b_ada):
    d, n = w_ada.shape
    tn = 1024
    return pl.pallas_call(
        _ada_kernel,
        out_shape=jax.ShapeDtypeStruct((8, n), F32),
        grid=(n // tn,),
        in_specs=[pl.BlockSpec((8, d), lambda j: (0, 0)),
                  pl.BlockSpec((d, tn), lambda j: (0, j)),
                  pl.BlockSpec((1, tn), lambda j: (0, j))],
        out_specs=pl.BlockSpec((8, tn), lambda j: (0, j)),
        compiler_params=pltpu.CompilerParams(dimension_semantics=("arbitrary",), vmem_limit_bytes=VMEM_LIMIT),
        name="ada",
    )(c_pad, w_ada, b_ada.reshape(1, n))


def _inproj_kernel(x_ref, mod_ref, w_ref, o_ref, wbf_ref):
    @pl.when(pl.program_id(1) == 0)
    def _():
        wbf_ref[...] = w_ref[...].astype(BF16)

    shift = mod_ref[0, 0:1, :]
    scale = mod_ref[0, 1:2, :]
    h = (x_ref[...] * (1.0 + scale) + shift).astype(BF16)
    o_ref[...] = _dot(h, wbf_ref[...]).astype(o_ref.dtype)


def _inproj(x2, mod, w_in, seq):
    t, d = x2.shape
    n = w_in.shape[1]
    tm, tn = 1024, 1024
    per_batch = seq // tm
    return pl.pallas_call(
        _inproj_kernel,
        out_shape=jax.ShapeDtypeStruct((t, n), BF16),
        grid=(n // tn, t // tm),
        in_specs=[pl.BlockSpec((tm, d), lambda j, i: (i, 0)),
                  pl.BlockSpec((1, N_MOD, d), lambda j, i: (i // per_batch, 0, 0)),
                  pl.BlockSpec((d, tn), lambda j, i: (0, j))],
        out_specs=pl.BlockSpec((tm, tn), lambda j, i: (i, j)),
        scratch_shapes=[pltpu.VMEM((d, tn), BF16)],
        compiler_params=pltpu.CompilerParams(dimension_semantics=("arbitrary", "arbitrary"),
                                             vmem_limit_bytes=VMEM_LIMIT),
        name="inproj",
    )(x2, mod, w_in)


def _ret_decay_consts(blk):
    h = np.arange(RET_HEADS, dtype=np.float64)
    log_g = np.log(1.0 - 2.0 ** (-5.0 - h))
    n = np.arange(blk, dtype=np.float64)
    diff = n[:, None] - n[None, :]
    same = (n[:, None] // CHUNK) == (n[None, :] // CHUNK)
    later = (n[:, None] // CHUNK) > (n[None, :] // CHUNK)
    expo = np.where(same, np.abs(diff), diff)
    kscale = RET_KEY_DIM ** -0.5
    decay = np.where(same | later, np.exp(log_g[:, None, None] * expo[None]), 0.0) * kscale
    xi = np.exp(log_g[:, None] * (n[None, :] + 1.0))
    zeta = np.exp(log_g[:, None] * (blk - 1.0 - n[None, :])) * kscale
    g_blk = np.exp(log_g * blk)
    xi = np.broadcast_to(xi[:, :, None], (RET_HEADS, blk, LANES))
    zeta = np.broadcast_to(zeta[:, :, None], (RET_HEADS, blk, LANES))
    return (jnp.asarray(decay, F32), jnp.asarray(xi, F32), jnp.asarray(zeta, F32), [float(v) for v in g_blk])


def _mixer_kernel(g_blk, qa, ka0, ka1, ka2, va0, va1, va2, qb, kb, vb, gb, bias, cos, sin, decay, xi, zeta,
                  again, rgain, out, state, oa):
    i = pl.program_id(1)
    blk = qa.shape[0]

    @pl.when(i == 0)
    def _():
        state[...] = jnp.zeros_like(state)

    ok0 = i >= 2
    ok1 = i >= 1
    scale = HEAD_DIM ** -0.5
    ssq = jnp.zeros((blk, 1), F32)

    def raw_scores(h):
        sl = slice(h * HEAD_DIM, (h + 1) * HEAD_DIM)
        q = (qa[:, sl].astype(F32) * scale).astype(BF16)
        return _dot_t(q, ka0[:, sl]), _dot_t(q, ka1[:, sl]), _dot_t(q, ka2[:, sl])

    pending = raw_scores(0)
    for h in range(ATTN_HEADS):
        sl = slice(h * HEAD_DIM, (h + 1) * HEAD_DIM)
        r0, r1, r2 = pending
        if h + 1 < ATTN_HEADS:
            pending = raw_scores(h + 1)
        s0 = jnp.where(ok0, r0 + bias[h, :, 0:blk], NEG_BIG)
        s1 = jnp.where(ok1, r1 + bias[h, :, blk:2 * blk], NEG_BIG)
        s2 = r2 + bias[h, :, 2 * blk:3 * blk]
        m = jnp.max(jnp.maximum(jnp.maximum(s0, s1), s2), axis=-1, keepdims=True)
        p0 = jnp.exp(s0 - m)
        p1 = jnp.exp(s1 - m)
        p2 = jnp.exp(s2 - m)
        denom = jnp.sum(p0 + p1 + p2, axis=-1, keepdims=True)
        o = _dot(p0.astype(BF16), va0[:, sl]) + _dot(p1.astype(BF16), va1[:, sl]) + _dot(p2.astype(BF16), va2[:, sl])
        o = o * (1.0 / denom)
        oa[:, sl] = o
        ssq = ssq + jnp.sum(o * o, axis=-1, keepdims=True)
    width_a = ATTN_HEADS * HEAD_DIM
    inv_rms = lax.rsqrt(ssq * (1.0 / width_a) + EPS)
    out[:, 0:width_a] = (oa[...] * inv_rms * again[...]).astype(out.dtype)

    cosv = cos[...]
    sinv = sin[...]
    lane = lax.broadcasted_iota(jnp.int32, (blk, LANES), 1)
    first_half = (lane % RET_KEY_DIM) < (RET_KEY_DIM // 2)
    low_head = lane < RET_KEY_DIM

    def rope(v):
        rot = jnp.where(first_half, pltpu.roll(v, LANES - RET_KEY_DIM // 2, 1), pltpu.roll(v, RET_KEY_DIM // 2, 1))
        return v * cosv + rot * sinv

    for j in range(RET_HEADS // 2):
        sl2 = slice(j * LANES, (j + 1) * LANES)
        qr = rope(qb[:, sl2].astype(F32))
        k2 = rope(kb[:, sl2].astype(F32)).astype(BF16)
        for hh in range(2):
            h = 2 * j + hh
            slv = slice(h * RET_VALUE_DIM, (h + 1) * RET_VALUE_DIM)
            qm = jnp.where(low_head if hh == 0 else jnp.logical_not(low_head), qr, 0.0).astype(BF16)
            v = vb[:, slv]
            st = state[h]
            scores = _dot_t(qm, k2) * decay[h]
            ret = _dot(scores.astype(BF16), v) + _dot(qm, st.astype(BF16)) * xi[h]
            zv = (v.astype(F32) * zeta[h]).astype(BF16)
            upd = lax.dot_general(k2, zv, (((0,), (0,)), ((), ())), preferred_element_type=F32)
            state[h] = g_blk[h] * st + upd
            mu = jnp.mean(ret, axis=-1, keepdims=True)
            cen = ret - mu
            var = jnp.mean(cen * cen, axis=-1, keepdims=True)
            gate = gb[:, slv].astype(F32)
            yb = gate * _sigmoid(gate) * (cen * lax.rsqrt(var + EPS)) * rgain[:, slv]
            out[:, width_a + h * RET_VALUE_DIM: width_a + (h + 1) * RET_VALUE_DIM] = yb.astype(out.dtype)


def _attn_bias_table(rel_bias, blk):
    nk = 3 * blk
    period = nk + blk
    offs = np.concatenate([np.arange(nk), np.zeros((1,), np.int64), np.arange(-(blk - 1), 0)])
    rel_idx = np.clip(2 * blk - offs, -(CHUNK - 1), MAX_REL) + (CHUNK - 1)
    line = rel_bias[:, jnp.asarray(rel_idx)].astype(F32)
    heads = line.shape[0]
    flat = jnp.broadcast_to(line[:, None, :], (heads, blk, period)).reshape(heads, blk * period)
    tab = flat[:, :blk * (period - 1)].reshape(heads, blk, period - 1)[:, :, :nk]
    r = np.arange(blk)[:, None]
    c = np.arange(nk)[None, :]
    qc = r // CHUNK + (2 * blk) // CHUNK
    kc = c // CHUNK
    in_band = (kc <= qc) & (kc >= qc - LEFT_CHUNKS)
    return jnp.where(jnp.asarray(in_band)[None], tab, NEG_BIG)


def _rope_tables(seq):
    half = RET_KEY_DIM // 2
    inv = ROPE_BASE ** (-jnp.arange(half, dtype=F32) / half)
    ang = jnp.arange(seq, dtype=F32)[:, None] * inv[None, :]
    cos, sin = jnp.cos(ang), jnp.sin(ang)
    reps = LANES // RET_KEY_DIM
    cos_t = jnp.tile(jnp.concatenate([cos, cos], axis=-1), (1, reps))
    sin_t = jnp.tile(jnp.concatenate([-sin, sin], axis=-1), (1, reps))
    return cos_t, sin_t


def _mixer(proj, rel_bias, attn_gain, ret_gain, batch, seq):
    t = proj.shape[0]
    blk = SEQ_BLOCK
    assert 2 * blk == LEFT_CHUNKS * CHUNK and seq % blk == 0
    nb = seq // blk
    wa = ATTN_HEADS * HEAD_DIM
    wqk = RET_HEADS * RET_KEY_DIM
    wv = RET_HEADS * RET_VALUE_DIM
    assert wa == wv == 2 * wqk
    bias = _attn_bias_table(rel_bias, blk)
    cos_t, sin_t = _rope_tables(seq)
    decay, xi, zeta, g_blk = _ret_decay_consts(blk)

    def row(b, i):
        return b * nb + i

    def kspec(back, col):
        return pl.BlockSpec((blk, wa), lambda b, i: (row(b, jnp.maximum(i - back, 0)), col))

    const3 = lambda b, i: (0, 0, 0)
    in_specs = [
        pl.BlockSpec((blk, wa), lambda b, i: (row(b, i), 0)),
        kspec(2, 1), kspec(1, 1), kspec(0, 1),
        kspec(2, 2), kspec(1, 2), kspec(0, 2),
        pl.BlockSpec((blk, wqk), lambda b, i: (row(b, i), 3 * wa // wqk)),
        pl.BlockSpec((blk, wqk), lambda b, i: (row(b, i), 3 * wa // wqk + 1)),
        pl.BlockSpec((blk, wv), lambda b, i: (row(b, i), (3 * wa + 2 * wqk) // wv)),
        pl.BlockSpec((blk, wv), lambda b, i: (row(b, i), (3 * wa + 2 * wqk) // wv + 1)),
        pl.BlockSpec((ATTN_HEADS, blk, 3 * blk), const3),
        pl.BlockSpec((blk, LANES), lambda b, i: (i, 0)),
        pl.BlockSpec((blk, LANES), lambda b, i: (i, 0)),
        pl.BlockSpec((RET_HEADS, blk, blk), const3),
        pl.BlockSpec((RET_HEADS, blk, LANES), const3),
        pl.BlockSpec((RET_HEADS, blk, LANES), const3),
        pl.BlockSpec((1, wa), lambda b, i: (0, 0)),
        pl.BlockSpec((1, wv), lambda b, i: (0, 0)),
    ]
    return pl.pallas_call(
        functools.partial(_mixer_kernel, g_blk),
        out_shape=jax.ShapeDtypeStruct((t, wa + wv), BF16),
        grid=(batch, nb),
        in_specs=in_specs,
        out_specs=pl.BlockSpec((blk, wa + wv), lambda b, i: (row(b, i), 0)),
        scratch_shapes=[pltpu.VMEM((RET_HEADS, LANES, RET_VALUE_DIM), F32), pltpu.VMEM((blk, wa), F32)],
        compiler_params=pltpu.CompilerParams(dimension_semantics=("arbitrary", "arbitrary"),
                                             vmem_limit_bytes=VMEM_LIMIT),
        name="mixer",
    )(proj, proj, proj, proj, proj, proj, proj, proj, proj, proj, proj, bias, cos_t, sin_t, decay, xi, zeta,
      attn_gain.reshape(1, wa), ret_gain.reshape(1, wv))


def _layer_norm(z, gain, bias):
    mu = jnp.mean(z, axis=-1, keepdims=True)
    cen = z - mu
    var = jnp.mean(cen * cen, axis=-1, keepdims=True)
    return cen * lax.rsqrt(var + EPS) * gain + bias


def _pack_bf16_pairs(lo, hi):
    return pltpu.pack_elementwise([lo, hi], packed_dtype=BF16)


def _unpack_pairs_f32(words):
    lo = pltpu.unpack_elementwise(words, index=0, packed_dtype=BF16, unpacked_dtype=F32)
    hi = pltpu.unpack_elementwise(words, index=1, packed_dtype=BF16, unpacked_dtype=F32)
    return lo, hi


def _store_packed_rows(ref, val):
    tm, d = val.shape
    half = d // 2
    nchunk = half // LANES
    for c in range(nchunk):
        words = _pack_bf16_pairs(val[:, c * LANES:(c + 1) * LANES], val[:, half + c * LANES: half + (c + 1) * LANES])
        ref[pl.ds(c, tm, stride=nchunk), :] = words


def _route_tile(scores_t, bias_col):
    ne, tm = scores_t.shape
    per_group = ne // N_GROUPS
    biased = scores_t + bias_col
    neg_inf = -jnp.inf
    iota_g = lax.broadcasted_iota(I32, (per_group, tm), 0)
    gscore = []
    for g in range(N_GROUPS):
        b = biased[g * per_group:(g + 1) * per_group, :]
        m1 = jnp.max(b, axis=0, keepdims=True)
        first = jnp.min(jnp.where(b == m1, iota_g, per_group), axis=0, keepdims=True)
        m2 = jnp.max(jnp.where(iota_g == first, neg_inf, b), axis=0, keepdims=True)
        gscore.append(m1 + m2)
    masked_parts = []
    for g in range(N_GROUPS):
        beaten = jnp.zeros((1, tm), I32)
        for o in range(N_GROUPS):
            if o == g:
                continue
            wins = (gscore[o] >= gscore[g]) if o < g else (gscore[o] > gscore[g])
            beaten = beaten + wins.astype(I32)
        keep = beaten < TOPK_GROUPS
        masked_parts.append(jnp.where(keep, biased[g * per_group:(g + 1) * per_group, :], neg_inf))
    masked = jnp.concatenate(masked_parts, axis=0)
    eiota = lax.broadcasted_iota(I32, (ne, tm), 0)
    ids, vals = [], []
    for _ in range(TOP_K):
        m = jnp.max(masked, axis=0, keepdims=True)
        first = jnp.min(jnp.where(masked == m, eiota, ne), axis=0, keepdims=True)
        hit = eiota == first
        vals.append(jnp.sum(jnp.where(hit, scores_t, 0.0), axis=0, keepdims=True))
        ids.append(first)
        masked = jnp.where(hit, neg_inf, masked)
    return ids, vals


def _outproj_kernel(alpha, mix_ref, x_ref, mod_ref, wout_ref, g_ref, b_ref, wrt_ref, rb_ref, tri_ref, ones_ref,
                    x1_ref, h2p_ref, idx_ref, w_ref, rank_ref, cnt_ref, carry):
    i = pl.program_id(0)
    tm, d = x_ref.shape
    nchunk = d // 2 // LANES
    ne = wrt_ref.shape[0]
    sub = ROUTE_SUBTILE

    @pl.when(i == 0)
    def _():
        carry[...] = jnp.zeros_like(carry)

    gate1 = mod_ref[0, 2:3, :]
    scale2 = 1.0 + mod_ref[0, 4:5, :]
    shift2 = mod_ref[0, 3:4, :]
    eiota = lax.broadcasted_iota(I32, (ne, LANES), 0)
    counted = carry[...]
    for part in range(tm // sub):
        rs = slice(part * sub, (part + 1) * sub)
        y = _dot(mix_ref[rs, :], wout_ref[...])
        x1 = _layer_norm(alpha * x_ref[rs, :] + gate1 * y, g_ref[...], b_ref[...])
        x1_ref[rs, :] = x1
        h2 = x1 * scale2 + shift2
        _store_packed_rows(h2p_ref.at[pl.ds(part * sub * nchunk, sub * nchunk), :], h2)
        logits_t = _dot_t(wrt_ref[...], h2.astype(BF16))
        for q in range(sub // LANES):
            ls = slice(part * sub + q * LANES, part * sub + (q + 1) * LANES)
            scores_t = _sigmoid(logits_t[:, q * LANES:(q + 1) * LANES])
            ids, vals = _route_tile(scores_t, rb_ref[:, 0:1])
            chosen = jnp.zeros((ne, LANES), F32)
            for k in range(TOP_K):
                chosen = chosen + (eiota == ids[k]).astype(F32)
            chosen_bf = chosen.astype(BF16)
            before = _dot(chosen_bf, tri_ref[...]) - chosen + counted
            ranks = [jnp.sum(jnp.where(eiota == ids[k], before, 0.0), axis=0, keepdims=True)
                     for k in range(TOP_K)]
            counted = counted + _dot(chosen_bf, ones_ref[...])
            wsel = jnp.concatenate(vals, axis=0)
            w_ref[:, ls] = wsel / jnp.sum(wsel, axis=0, keepdims=True) * ROUTED_SCALE
            idx_ref[:, ls] = jnp.concatenate(ids, axis=0)
            rank_ref[:, ls] = jnp.concatenate(ranks, axis=0).astype(I32)
    carry[...] = counted
    cnt_ref[...] = counted


def _outproj(mix, x2, mod, w_out_bf, ln_g, ln_b, w_router_t_bf, router_bias, alpha, seq):
    t, d = x2.shape
    ne = w_router_t_bf.shape[0]
    tm = ROUTE_TOKENS
    per_batch = seq // tm
    nchunk = d // 2 // LANES
    tri = jnp.asarray(np.triu(np.ones((LANES, LANES), np.float32)), BF16)
    ones = jnp.ones((LANES, LANES), BF16)
    rb = jnp.broadcast_to(router_bias.astype(F32)[:, None], (ne, LANES))
    c2 = lambda i: (0, 0)
    return pl.pallas_call(
        functools.partial(_outproj_kernel, alpha),
        out_shape=(jax.ShapeDtypeStruct((t, d), F32),
                   jax.ShapeDtypeStruct((t * nchunk, LANES), U32),
                   jax.ShapeDtypeStruct((TOP_K, t), I32),
                   jax.ShapeDtypeStruct((TOP_K, t), F32),
                   jax.ShapeDtypeStruct((TOP_K, t), I32),
                   jax.ShapeDtypeStruct((ne, LANES), F32)),
        grid=(t // tm,),
        in_specs=[pl.BlockSpec((tm, d), lambda i: (i, 0)),
                  pl.BlockSpec((tm, d), lambda i: (i, 0)),
                  pl.BlockSpec((1, N_MOD, d), lambda i: (i // per_batch, 0, 0)),
                  pl.BlockSpec((d, d), c2),
                  pl.BlockSpec((1, d), c2),
                  pl.BlockSpec((1, d), c2),
                  pl.BlockSpec((ne, d), c2),
                  pl.BlockSpec((ne, LANES), c2),
                  pl.BlockSpec((LANES, LANES), c2),
                  pl.BlockSpec((LANES, LANES), c2)],
        out_specs=(pl.BlockSpec((tm, d), lambda i: (i, 0)),
                   pl.BlockSpec((tm * nchunk, LANES), lambda i: (i, 0)),
                   pl.BlockSpec((TOP_K, tm), lambda i: (0, i)),
                   pl.BlockSpec((TOP_K, tm), lambda i: (0, i)),
                   pl.BlockSpec((TOP_K, tm), lambda i: (0, i)),
                   pl.BlockSpec((ne, LANES), c2)),
        scratch_shapes=[pltpu.VMEM((ne, LANES), F32)],
        compiler_params=pltpu.CompilerParams(dimension_semantics=("arbitrary",), vmem_limit_bytes=VMEM_LIMIT),
        name="outproj",
    )(mix, x2, mod, w_out_bf, ln_g.reshape(1, d), ln_b.reshape(1, d), w_router_t_bf, rb, tri, ones)


def _slots_kernel(idx_ref, rank_ref, start_ref, pos_ref):
    ne = start_ref.shape[0]
    tm = idx_ref.shape[1]
    eiota = lax.broadcasted_iota(I32, (ne, tm), 0)
    start = jnp.concatenate([start_ref[...]] * (tm // LANES), axis=-1)
    rows = []
    for k in range(TOP_K):
        rows.append(jnp.sum(jnp.where(eiota == idx_ref[k:k + 1, :], start, 0), axis=0, keepdims=True))
    pos_ref[...] = jnp.concatenate(rows, axis=0) + rank_ref[...]


def _slots(idx_t, rank_t, start):
    k, t = idx_t.shape
    ne = start.shape[0]
    tm = 512
    spec = pl.BlockSpec((k, tm), lambda i: (0, i))
    return pl.pallas_call(
        _slots_kernel,
        out_shape=jax.ShapeDtypeStruct((k, t), I32),
        grid=(t // tm,),
        in_specs=[spec, spec, pl.BlockSpec((ne, LANES), lambda i: (0, 0))],
        out_specs=spec,
        compiler_params=pltpu.CompilerParams(dimension_semantics=("arbitrary",)),
        name="slots",
    )(idx_t, rank_t, jnp.broadcast_to(start[:, None], (ne, LANES)))


def _invert_kernel(pad, pos_ref, lst_ref):
    ntok = pos_ref.shape[0]
    k = pl.program_id(0)
    n = lst_ref.shape[0] - pad
    unroll = 8

    @pl.when(k == 0)
    def _():
        for p in range(pad):
            lst_ref[n + p] = 0

    step = (1 << 16) + SUBLANES
    first = (k * ntok) << 16

    def body(j, carry):
        word = first + j * (unroll * step)
        for u in range(unroll):
            lst_ref[pos_ref[j * unroll + u]] = word + u * step
        return carry

    lax.fori_loop(0, ntok // unroll, body, 0)


def _invert(pos_flat, ntok, pad):
    n = pos_flat.shape[0]
    assert n <= 1 << 16 and ntok * SUBLANES <= 1 << 16
    return pl.pallas_call(
        functools.partial(_invert_kernel, pad),
        out_shape=jax.ShapeDtypeStruct((n + pad,), I32),
        grid=(n // ntok,),
        in_specs=[pl.BlockSpec((ntok,), lambda k: (k,), memory_space=pltpu.SMEM)],
        out_specs=pl.BlockSpec(memory_space=pltpu.SMEM),
        compiler_params=pltpu.CompilerParams(dimension_semantics=("arbitrary",)),
        name="invert",
    )(pos_flat)


def _moe_kernel(cnt_ref, start_ref, lst_ref, h2p_ref, wg_ref, wu_ref, wd_ref, out_hbm,
                xbuf, stage, stage_s, sem, sem_s, state):
    e = pl.program_id(0)
    nchunk = SUBLANES
    ring, ring_s = stage.shape[0], stage_s.shape[0]
    m, ms = stage.shape[1] // nchunk, stage_s.shape[1] // nchunk
    nlist = lst_ref.shape[0] - m
    spare_s = nlist + ring * m
    n = cnt_ref[e]

    def wait_slot(buf, sems, slot):
        pltpu.make_async_copy(buf.at[slot], out_hbm.at[pl.ds(0, buf.shape[1]), :], sems.at[slot]).wait()

    def send_rows(buf, sems, slot, base, occupied, spare):
        rows = buf.shape[1] // nchunk
        for r in range(rows):
            row = jnp.where(r < occupied, lax.shift_right_logical(lst_ref[base + r], 16), spare + slot * rows + r)
            pltpu.make_async_copy(buf.at[slot, pl.ds(r * nchunk, nchunk), :],
                                  out_hbm.at[pl.ds(pl.multiple_of(row * nchunk, nchunk), nchunk), :],
                                  sems.at[slot]).start()

    def swiglu_rows(base, rows, into):
        for r in range(rows):
            src = pl.multiple_of(lst_ref[base + r] & 0xFFFF, nchunk)
            xbuf[r * nchunk:(r + 1) * nchunk, :] = h2p_ref[pl.ds(src, nchunk), :]
        los, his = [], []
        for c in range(nchunk):
            lo, hi = _unpack_pairs_f32(xbuf[pl.ds(c, rows, stride=nchunk), :])
            los.append(lo.astype(BF16))
            his.append(hi.astype(BF16))
        xrows = jnp.concatenate(los + his, axis=-1)
        g = _dot(xrows, wg_ref[0].astype(BF16))
        u = _dot(xrows, wu_ref[0].astype(BF16))
        act = (g * _sigmoid(g) * u).astype(BF16)
        _store_packed_rows(into, _dot(act, wd_ref[0].astype(BF16)))

    @pl.when(e == 0)
    def _():
        for buf, sems, spare in ((stage, sem, nlist), (stage_s, sem_s, spare_s)):
            rows = buf.shape[1] // nchunk
            zeros = jnp.zeros((rows, LANES), F32)
            packed = jnp.concatenate([_pack_bf16_pairs(zeros, zeros)] * nchunk, axis=0)
            for s in range(buf.shape[0]):
                buf[s] = packed
                fill = pltpu.make_async_copy(buf.at[s], out_hbm.at[pl.ds((spare + s * rows) * nchunk, rows * nchunk), :],
                                             sems.at[s])
                fill.start()
                fill.wait()
        state[0] = 1
        state[1] = 0
        state[2] = 0
        state[3] = 0

    nlong = (n + m // 2 - 1) // m
    tail = jnp.maximum(n - nlong * m, 0)

    def long_run(j, carry):
        done = state[0]
        slot = lax.rem(done, ring)
        base = start_ref[e] + j * m

        @pl.when(done >= ring)
        def _():
            wait_slot(stage, sem, slot)

        send_rows(stage, sem, lax.rem(done + ring - 1, ring), state[1], state[2], nlist)
        swiglu_rows(base, m, stage.at[slot])
        state[0] = done + 1
        state[1] = base
        state[2] = jnp.minimum(n - j * m, m)
        return carry

    def short_run(i, carry):
        done = state[3]
        slot = lax.rem(done, ring_s)
        base = start_ref[e] + nlong * m + i * ms

        @pl.when(done >= ring_s)
        def _():
            wait_slot(stage_s, sem_s, slot)

        swiglu_rows(base, ms, stage_s.at[slot])
        send_rows(stage_s, sem_s, slot, base, jnp.minimum(tail - i * ms, ms), spare_s)
        state[3] = done + 1
        return carry

    lax.fori_loop(0, nlong, long_run, 0)
    lax.fori_loop(0, (tail + ms - 1) // ms, short_run, 0)

    @pl.when(e == pl.num_programs(0) - 1)
    def _():
        done = state[0]
        send_rows(stage, sem, lax.rem(done + ring - 1, ring), state[1], state[2], nlist)
        for s in range(ring):
            @pl.when(done > s)
            def _():
                wait_slot(stage, sem, s)
        for s in range(ring_s):
            @pl.when(state[3] > s)
            def _():
                wait_slot(stage_s, sem_s, s)


def _moe(h2p, lst, counts, starts, w_gate, w_up, w_down):
    ne, d, de = w_gate.shape
    nchunk = d // 2 // LANES
    assert nchunk == SUBLANES
    m, ms, ring, ring_s = MOE_RUN, MOE_SHORT_RUN, MOE_RING, 2
    nlist = lst.shape[0] - m
    wspec = lambda e, *_: (e, 0, 0)
    grid_spec = pltpu.PrefetchScalarGridSpec(
        num_scalar_prefetch=3,
        grid=(ne,),
        in_specs=[pl.BlockSpec(memory_space=pltpu.VMEM),
                  pl.BlockSpec((1, d, de), wspec),
                  pl.BlockSpec((1, d, de), wspec),
                  pl.BlockSpec((1, de, d), wspec)],
        out_specs=pl.BlockSpec(memory_space=pl.ANY),
        scratch_shapes=[pltpu.VMEM((m * nchunk, LANES), U32), pltpu.VMEM((ring, m * nchunk, LANES), U32),
                        pltpu.VMEM((ring_s, ms * nchunk, LANES), U32),
                        pltpu.SemaphoreType.DMA((ring,)), pltpu.SemaphoreType.DMA((ring_s,)), pltpu.SMEM((4,), I32)],
    )
    return pl.pallas_call(
        _moe_kernel,
        out_shape=jax.ShapeDtypeStruct(((nlist + ring * m + ring_s * ms) * nchunk, LANES), U32),
        grid_spec=grid_spec,
        compiler_params=pltpu.CompilerParams(dimension_semantics=("arbitrary",), vmem_limit_bytes=VMEM_LIMIT),
        name="moe",
    )(counts, starts, lst, h2p, w_gate, w_up, w_down)


def _combine_kernel(alpha, *refs):
    y_refs = refs[:TOP_K]
    w_ref, x1_ref, mod_ref, wsg_ref, wsu_ref, wsd_ref, g_ref, b_ref, o_ref, wbc, routed = refs[TOP_K:]
    tt, d = x1_ref.shape
    half = d // 2
    nchunk = SUBLANES

    x1 = x1_ref[...]
    h2 = (x1 * (1.0 + mod_ref[0, 4:5, :]) + mod_ref[0, 3:4, :]).astype(BF16)
    g = _dot(h2, wsg_ref[...])
    u = _dot(h2, wsu_ref[...])
    shared = _dot((g * _sigmoid(g) * u).astype(BF16), wsd_ref[...])
    for k in range(TOP_K):
        wbc[k] = jnp.broadcast_to(w_ref[:, k:k + 1], (tt, LANES))

    for c in range(nchunk):
        acc_lo = acc_hi = None
        for k in range(TOP_K):
            lo, hi = _unpack_pairs_f32(y_refs[k][pl.ds(c, tt, stride=nchunk), :])
            wk = wbc[k]
            acc_lo = lo * wk if k == 0 else acc_lo + lo * wk
            acc_hi = hi * wk if k == 0 else acc_hi + hi * wk
        routed[:, c * LANES:(c + 1) * LANES] = acc_lo
        routed[:, half + c * LANES: half + (c + 1) * LANES] = acc_hi

    z = alpha * x1 + mod_ref[0, 5:6, :] * (routed[...] + shared)
    o_ref[...] = _layer_norm(z, g_ref[...], b_ref[...])


def _combine(y_ranked, w_tok, x1, mod, wsg_bf, wsu_bf, wsd_bf, ln_g, ln_b, alpha, seq):
    t, d = x1.shape
    ds_ = wsg_bf.shape[1]
    tt = COMBINE_TOKENS
    per_batch = seq // tt
    nsteps = t // tt
    nchunk = d // 2 // LANES
    assert nchunk == SUBLANES
    c2 = lambda i: (0, 0)
    y_specs = [pl.BlockSpec((tt * nchunk, LANES), functools.partial(lambda k, i: (k * nsteps + i, 0), k))
               for k in range(TOP_K)]
    return pl.pallas_call(
        functools.partial(_combine_kernel, alpha),
        out_shape=jax.ShapeDtypeStruct((t, d), F32),
        grid=(nsteps,),
        in_specs=y_specs + [pl.BlockSpec((tt, TOP_K), lambda i: (i, 0)),
                            pl.BlockSpec((tt, d), lambda i: (i, 0)),
                            pl.BlockSpec((1, N_MOD, d), lambda i: (i // per_batch, 0, 0)),
                            pl.BlockSpec((d, ds_), c2),
                            pl.BlockSpec((d, ds_), c2),
                            pl.BlockSpec((ds_, d), c2),
                            pl.BlockSpec((1, d), c2),
                            pl.BlockSpec((1, d), c2)],
        out_specs=pl.BlockSpec((tt, d), lambda i: (i, 0)),
        scratch_shapes=[pltpu.VMEM((TOP_K, tt, LANES), F32), pltpu.VMEM((tt, d), F32)],
        compiler_params=pltpu.CompilerParams(dimension_semantics=("arbitrary",), vmem_limit_bytes=VMEM_LIMIT),
        name="combine",
    )(*([y_ranked] * TOP_K), w_tok, x1, mod, wsg_bf, wsu_bf, wsd_bf, ln_g.reshape(1, d), ln_b.reshape(1, d))


def kernel(x, c, w_ada, b_ada, w_in, rel_bias, attn_gain, ret_gain, w_out, ln1_gain, ln1_bias, w_router, router_bias,
           w_gate, w_up, w_down, ws_gate, ws_up, ws_down, ln2_gain, ln2_bias):
    batch, seq, d = x.shape
    depth = w_ada.shape[0]
    alpha = (2.0 * depth) ** 0.25
    t = batch * seq
    assert t % COMBINE_TOKENS == 0
    xt = x.reshape(t, d)
    c_pad = jnp.zeros((8, d), F32).at[:batch].set(c)
    for l in range(depth):
        mod = _ada(c_pad, w_ada[l], b_ada[l])[:batch].reshape(batch, N_MOD, d)
        proj = _inproj(xt, mod, w_in[l], seq)
        mix = _mixer(proj, rel_bias[l], attn_gain[l], ret_gain[l], batch, seq)
        x1, h2p, idx_t, w_t, rank_t, cnt = _outproj(mix, xt, mod, w_out[l].astype(BF16), ln1_gain[l], ln1_bias[l],
                                                    w_router[l].T.astype(BF16), router_bias[l], alpha, seq)
        counts = cnt[:, 0].astype(I32)
        starts = jnp.cumsum(counts) - counts
        pos = _slots(idx_t, rank_t, starts)
        lst = _invert(pos.reshape(-1), t, MOE_RUN)
        y_ranked = _moe(h2p, lst, counts, starts, w_gate[l], w_up[l], w_down[l])
        xt = _combine(y_ranked, w_t.T, x1, mod, ws_gate[l].astype(BF16), ws_up[l].astype(BF16),
                      ws_down[l].astype(BF16), ln2_gain[l], ln2_bias[l], alpha, seq)
    return xt.reshape(batch, seq, d)
```

```python
import functools

import jax
import jax.numpy as jnp
import numpy as np
from jax import lax
from jax.experimental import pallas as pl
from jax.experimental.pallas import tpu as pltpu

F32 = jnp.float32
BF16 = jnp.bfloat16
U32 = jnp.uint32
I32 = jnp.int32

CHUNK = 64
LEFT_CHUNKS = 8
MAX_REL = 256
ATTN_HEADS = 8
HEAD_DIM = 128
RET_HEADS = 8
RET_KEY_DIM = 64
RET_VALUE_DIM = 128
ROPE_BASE = 10000.0
N_GROUPS = 8
TOPK_GROUPS = 4
TOP_K = 8
ROUTED_SCALE = 2.5
EPS = 1e-5
N_MOD = 6
LANES = 128
SUBLANES = 8

SEQ_BLOCK = 256
ROUTE_TOKENS = 512
ROUTE_SUBTILE = 256
MOE_RUN = 128
MOE_RING = 3
COMBINE_TOKENS = 128
NEG_BIG = -1e30
VMEM_LIMIT = 56 * 1024 * 1024


def _sigmoid(v):
    return 1.0 / (1.0 + jnp.exp(-v))


def _dot(a, b):
    return jnp.dot(a, b, preferred_element_type=F32)


def _dot_t(a, b):
    return lax.dot_general(a, b, (((1,), (1,)), ((), ())), preferred_element_type=F32)


def _ada_kernel(c_ref, w_ref, b_ref, o_ref):
    c = c_ref[...]
    s = (c * _sigmoid(c)).astype(BF16)
    o_ref[...] = _dot(s, w_ref[...].astype(BF16)) + b_ref[...]


def _ada(c_pad, w_ada, b_ada):
    d, n = w_ada.shape
    tn = 1024
    return pl.pallas_call(
        _ada_kernel,
        out_shape=jax.ShapeDtypeStruct((8, n), F32),
        grid=(n // tn,),
        in_specs=[pl.BlockSpec((8, d), lambda j: (0, 0)),
                  pl.BlockSpec((d, tn), lambda j: (0, j)),
                  pl.BlockSpec((1, tn), lambda j: (0, j))],
        out_specs=pl.BlockSpec((8, tn), lambda j: (0, j)),
        compiler_params=pltpu.CompilerParams(dimension_semantics=("arbitrary",), vmem_limit_bytes=VMEM_LIMIT),
        name="ada",
    )(c_pad, w_ada, b_ada.reshape(1, n))


def _inproj_kernel(x_ref, mod_ref, w_ref, o_ref, wbf_ref):
    @pl.when(pl.program_id(1) == 0)
    def _():
        wbf_ref[...] = w_ref[...].astype(BF16)

    shift = mod_ref[0, 0:1, :]
    scale = mod_ref[0, 1:2, :]
    h = (x_ref[...] * (1.0 + scale) + shift).astype(BF16)
    o_ref[...] = _dot(h, wbf_ref[...]).astype(o_ref.dtype)


def _inproj(x2, mod, w_in, seq):
    t, d = x2.shape
    n = w_in.shape[1]
    tm, tn = 1024, 1024
    per_batch = seq // tm
    return pl.pallas_call(
        _inproj_kernel,
        out_shape=jax.ShapeDtypeStruct((t, n), BF16),
        grid=(n // tn, t // tm),
        in_specs=[pl.BlockSpec((tm, d), lambda j, i: (i, 0)),
                  pl.BlockSpec((1, N_MOD, d), lambda j, i: (i // per_batch, 0, 0)),
                  pl.BlockSpec((d, tn), lambda j, i: (0, j))],
        out_specs=pl.BlockSpec((tm, tn), lambda j, i: (i, j)),
        scratch_shapes=[pltpu.VMEM((d, tn), BF16)],
        compiler_params=pltpu.CompilerParams(dimension_semantics=("arbitrary", "arbitrary"),
                                             vmem_limit_bytes=VMEM_LIMIT),
        name="inproj",
    )(x2, mod, w_in)


def _ret_decay_consts(blk):
    h = np.arange(RET_HEADS, dtype=np.float64)
    log_g = np.log(1.0 - 2.0 ** (-5.0 - h))
    n = np.arange(blk, dtype=np.float64)
    diff = n[:, None] - n[None, :]
    same = (n[:, None] // CHUNK) == (n[None, :] // CHUNK)
    later = (n[:, None] // CHUNK) > (n[None, :] // CHUNK)
    expo = np.where(same, np.abs(diff), diff)
    kscale = RET_KEY_DIM ** -0.5
    decay = np.where(same | later, np.exp(log_g[:, None, None] * expo[None]), 0.0) * kscale
    xi = np.exp(log_g[:, None] * (n[None, :] + 1.0))
    zeta = np.exp(log_g[:, None] * (blk - 1.0 - n[None, :])) * kscale
    g_blk = np.exp(log_g * blk)
    xi = np.broadcast_to(xi[:, :, None], (RET_HEADS, blk, LANES))
    zeta = np.broadcast_to(zeta[:, :, None], (RET_HEADS, blk, LANES))
    return (jnp.asarray(decay, F32), jnp.asarray(xi, F32), jnp.asarray(zeta, F32), [float(v) for v in g_blk])


def _mixer_kernel(g_blk, qa, ka0, ka1, ka2, va0, va1, va2, qb, kb, vb, gb, bias, cos, sin, decay, xi, zeta,
                  again, rgain, out, state, oa):
    i = pl.program_id(1)
    blk = qa.shape[0]

    @pl.when(i == 0)
    def _():
        state[...] = jnp.zeros_like(state)

    ok0 = i >= 2
    ok1 = i >= 1
    scale = HEAD_DIM ** -0.5
    ssq = jnp.zeros((blk, 1), F32)

    def raw_scores(h):
        sl = slice(h * HEAD_DIM, (h + 1) * HEAD_DIM)
        q = (qa[:, sl].astype(F32) * scale).astype(BF16)
        return _dot_t(q, ka0[:, sl]), _dot_t(q, ka1[:, sl]), _dot_t(q, ka2[:, sl])

    pending = raw_scores(0)
    for h in range(ATTN_HEADS):
        sl = slice(h * HEAD_DIM, (h + 1) * HEAD_DIM)
        r0, r1, r2 = pending
        if h + 1 < ATTN_HEADS:
            pending = raw_scores(h + 1)
        s0 = jnp.where(ok0, r0 + bias[h, :, 0:blk], NEG_BIG)
        s1 = jnp.where(ok1, r1 + bias[h, :, blk:2 * blk], NEG_BIG)
        s2 = r2 + bias[h, :, 2 * blk:3 * blk]
        m = jnp.max(jnp.maximum(jnp.maximum(s0, s1), s2), axis=-1, keepdims=True)
        p0 = jnp.exp(s0 - m)
        p1 = jnp.exp(s1 - m)
        p2 = jnp.exp(s2 - m)
        denom = jnp.sum(p0 + p1 + p2, axis=-1, keepdims=True)
        o = _dot(p0.astype(BF16), va0[:, sl]) + _dot(p1.astype(BF16), va1[:, sl]) + _dot(p2.astype(BF16), va2[:, sl])
        o = o * (1.0 / denom)
        oa[:, sl] = o
        ssq = ssq + jnp.sum(o * o, axis=-1, keepdims=True)
    width_a = ATTN_HEADS * HEAD_DIM
    inv_rms = lax.rsqrt(ssq * (1.0 / width_a) + EPS)
    out[:, 0:width_a] = (oa[...] * inv_rms * again[...]).astype(out.dtype)

    cosv = cos[...]
    sinv = sin[...]
    lane = lax.broadcasted_iota(jnp.int32, (blk, LANES), 1)
    first_half = (lane % RET_KEY_DIM) < (RET_KEY_DIM // 2)
    low_head = lane < RET_KEY_DIM

    def rope(v):
        rot = jnp.where(first_half, pltpu.roll(v, LANES - RET_KEY_DIM // 2, 1), pltpu.roll(v, RET_KEY_DIM // 2, 1))
        return v * cosv + rot * sinv

    for j in range(RET_HEADS // 2):
        sl2 = slice(j * LANES, (j + 1) * LANES)
        qr = rope(qb[:, sl2].astype(F32))
        k2 = rope(kb[:, sl2].astype(F32)).astype(BF16)
        for hh in range(2):
            h = 2 * j + hh
            slv = slice(h * RET_VALUE_DIM, (h + 1) * RET_VALUE_DIM)
            qm = jnp.where(low_head if hh == 0 else jnp.logical_not(low_head), qr, 0.0).astype(BF16)
            v = vb[:, slv]
            st = state[h]
            scores = _dot_t(qm, k2) * decay[h]
            ret = _dot(scores.astype(BF16), v) + _dot(qm, st.astype(BF16)) * xi[h]
            zv = (v.astype(F32) * zeta[h]).astype(BF16)
            upd = lax.dot_general(k2, zv, (((0,), (0,)), ((), ())), preferred_element_type=F32)
            state[h] = g_blk[h] * st + upd
            mu = jnp.mean(ret, axis=-1, keepdims=True)
            cen = ret - mu
            var = jnp.mean(cen * cen, axis=-1, keepdims=True)
            gate = gb[:, slv].astype(F32)
            yb = gate * _sigmoid(gate) * (cen * lax.rsqrt(var + EPS)) * rgain[:, slv]
            out[:, width_a + h * RET_VALUE_DIM: width_a + (h + 1) * RET_VALUE_DIM] = yb.astype(out.dtype)


def _attn_bias_table(rel_bias, blk):
    nk = 3 * blk
    period = nk + blk
    offs = np.concatenate([np.arange(nk), np.zeros((1,), np.int64), np.arange(-(blk - 1), 0)])
    rel_idx = np.clip(2 * blk - offs, -(CHUNK - 1), MAX_REL) + (CHUNK - 1)
    line = rel_bias[:, jnp.asarray(rel_idx)].astype(F32)
    heads = line.shape[0]
    flat = jnp.broadcast_to(line[:, None, :], (heads, blk, period)).reshape(heads, blk * period)
    tab = flat[:, :blk * (period - 1)].reshape(heads, blk, period - 1)[:, :, :nk]
    r = np.arange(blk)[:, None]
    c = np.arange(nk)[None, :]
    qc = r // CHUNK + (2 * blk) // CHUNK
    kc = c // CHUNK
    in_band = (kc <= qc) & (kc >= qc - LEFT_CHUNKS)
    return jnp.where(jnp.asarray(in_band)[None], tab, NEG_BIG)


def _rope_tables(seq):
    half = RET_KEY_DIM // 2
    inv = ROPE_BASE ** (-jnp.arange(half, dtype=F32) / half)
    ang = jnp.arange(seq, dtype=F32)[:, None] * inv[None, :]
    cos, sin = jnp.cos(ang), jnp.sin(ang)
    reps = LANES // RET_KEY_DIM
    cos_t = jnp.tile(jnp.concatenate([cos, cos], axis=-1), (1, reps))
    sin_t = jnp.tile(jnp.concatenate([-sin, sin], axis=-1), (1, reps))
    return cos_t, sin_t


def _mixer(proj, rel_bias, attn_gain, ret_gain, batch, seq):
    t = proj.shape[0]
    blk = SEQ_BLOCK
    assert 2 * blk == LEFT_CHUNKS * CHUNK and seq % blk == 0
    nb = seq // blk
    wa = ATTN_HEADS * HEAD_DIM
    wqk = RET_HEADS * RET_KEY_DIM
    wv = RET_HEADS * RET_VALUE_DIM
    assert wa == wv == 2 * wqk
    bias = _attn_bias_table(rel_bias, blk)
    cos_t, sin_t = _rope_tables(seq)
    decay, xi, zeta, g_blk = _ret_decay_consts(blk)

    def row(b, i):
        return b * nb + i

    def kspec(back, col):
        return pl.BlockSpec((blk, wa), lambda b, i: (row(b, jnp.maximum(i - back, 0)), col))

    const3 = lambda b, i: (0, 0, 0)
    in_specs = [
        pl.BlockSpec((blk, wa), lambda b, i: (row(b, i), 0)),
        kspec(2, 1), kspec(1, 1), kspec(0, 1),
        kspec(2, 2), kspec(1, 2), kspec(0, 2),
        pl.BlockSpec((blk, wqk), lambda b, i: (row(b, i), 3 * wa // wqk)),
        pl.BlockSpec((blk, wqk), lambda b, i: (row(b, i), 3 * wa // wqk + 1)),
        pl.BlockSpec((blk, wv), lambda b, i: (row(b, i), (3 * wa + 2 * wqk) // wv)),
        pl.BlockSpec((blk, wv), lambda b, i: (row(b, i), (3 * wa + 2 * wqk) // wv + 1)),
        pl.BlockSpec((ATTN_HEADS, blk, 3 * blk), const3),
        pl.BlockSpec((blk, LANES), lambda b, i: (i, 0)),
        pl.BlockSpec((blk, LANES), lambda b, i: (i, 0)),
        pl.BlockSpec((RET_HEADS, blk, blk), const3),
        pl.BlockSpec((RET_HEADS, blk, LANES), const3),
        pl.BlockSpec((RET_HEADS, blk, LANES), const3),
        pl.BlockSpec((1, wa), lambda b, i: (0, 0)),
        pl.BlockSpec((1, wv), lambda b, i: (0, 0)),
    ]
    return pl.pallas_call(
        functools.partial(_mixer_kernel, g_blk),
        out_shape=jax.ShapeDtypeStruct((t, wa + wv), BF16),
        grid=(batch, nb),
        in_specs=in_specs,
        out_specs=pl.BlockSpec((blk, wa + wv), lambda b, i: (row(b, i), 0)),
        scratch_shapes=[pltpu.VMEM((RET_HEADS, LANES, RET_VALUE_DIM), F32), pltpu.VMEM((blk, wa), F32)],
        compiler_params=pltpu.CompilerParams(dimension_semantics=("arbitrary", "arbitrary"),
                                             vmem_limit_bytes=VMEM_LIMIT),
        name="mixer",
    )(proj, proj, proj, proj, proj, proj, proj, proj, proj, proj, proj, bias, cos_t, sin_t, decay, xi, zeta,
      attn_gain.reshape(1, wa), ret_gain.reshape(1, wv))


def _layer_norm(z, gain, bias):
    mu = jnp.mean(z, axis=-1, keepdims=True)
    cen = z - mu
    var = jnp.mean(cen * cen, axis=-1, keepdims=True)
    return cen * lax.rsqrt(var + EPS) * gain + bias


def _pack_bf16_pairs(lo, hi):
    return pltpu.pack_elementwise([lo, hi], packed_dtype=BF16)


def _unpack_pairs_f32(words):
    lo = pltpu.unpack_elementwise(words, index=0, packed_dtype=BF16, unpacked_dtype=F32)
    hi = pltpu.unpack_elementwise(words, index=1, packed_dtype=BF16, unpacked_dtype=F32)
    return lo, hi


def _store_packed_rows(ref, val):
    tm, d = val.shape
    half = d // 2
    nchunk = half // LANES
    for c in range(nchunk):
        words = _pack_bf16_pairs(val[:, c * LANES:(c + 1) * LANES], val[:, half + c * LANES: half + (c + 1) * LANES])
        ref[pl.ds(c, tm, stride=nchunk), :] = words


def _route_tile(scores_t, bias_col):
    ne, tm = scores_t.shape
    per_group = ne // N_GROUPS
    biased = scores_t + bias_col
    neg_inf = -jnp.inf
    iota_g = lax.broadcasted_iota(I32, (per_group, tm), 0)
    gscore = []
    for g in range(N_GROUPS):
        b = biased[g * per_group:(g + 1) * per_group, :]
        m1 = jnp.max(b, axis=0, keepdims=True)
        first = jnp.min(jnp.where(b == m1, iota_g, per_group), axis=0, keepdims=True)
        m2 = jnp.max(jnp.where(iota_g == first, neg_inf, b), axis=0, keepdims=True)
        gscore.append(m1 + m2)
    masked_parts = []
    for g in range(N_GROUPS):
        beaten = jnp.zeros((1, tm), I32)
        for o in range(N_GROUPS):
            if o == g:
                continue
            wins = (gscore[o] >= gscore[g]) if o < g else (gscore[o] > gscore[g])
            beaten = beaten + wins.astype(I32)
        keep = beaten < TOPK_GROUPS
        masked_parts.append(jnp.where(keep, biased[g * per_group:(g + 1) * per_group, :], neg_inf))
    masked = jnp.concatenate(masked_parts, axis=0)
    eiota = lax.broadcasted_iota(I32, (ne, tm), 0)
    ids, vals = [], []
    for _ in range(TOP_K):
        m = jnp.max(masked, axis=0, keepdims=True)
        first = jnp.min(jnp.where(masked == m, eiota, ne), axis=0, keepdims=True)
        hit = eiota == first
        vals.append(jnp.sum(jnp.where(hit, scores_t, 0.0), axis=0, keepdims=True))
        ids.append(first)
        masked = jnp.where(hit, neg_inf, masked)
    return ids, vals


def _outproj_kernel(alpha, mix_ref, x_ref, mod_ref, wout_ref, g_ref, b_ref, wrt_ref, rb_ref, tri_ref, ones_ref,
                    x1_ref, h2p_ref, idx_ref, w_ref, rank_ref, cnt_ref, carry):
    i = pl.program_id(0)
    tm, d = x_ref.shape
    nchunk = d // 2 // LANES
    ne = wrt_ref.shape[0]
    sub = ROUTE_SUBTILE

    @pl.when(i == 0)
    def _():
        carry[...] = jnp.zeros_like(carry)

    gate1 = mod_ref[0, 2:3, :]
    scale2 = 1.0 + mod_ref[0, 4:5, :]
    shift2 = mod_ref[0, 3:4, :]
    eiota = lax.broadcasted_iota(I32, (ne, LANES), 0)
    counted = carry[...]
    for part in range(tm // sub):
        rs = slice(part * sub, (part + 1) * sub)
        y = _dot(mix_ref[rs, :], wout_ref[...])
        x1 = _layer_norm(alpha * x_ref[rs, :] + gate1 * y, g_ref[...], b_ref[...])
        x1_ref[rs, :] = x1
        h2 = x1 * scale2 + shift2
        _store_packed_rows(h2p_ref.at[pl.ds(part * sub * nchunk, sub * nchunk), :], h2)
        logits_t = _dot_t(wrt_ref[...], h2.astype(BF16))
        for q in range(sub // LANES):
            ls = slice(part * sub + q * LANES, part * sub + (q + 1) * LANES)
            scores_t = _sigmoid(logits_t[:, q * LANES:(q + 1) * LANES])
            ids, vals = _route_tile(scores_t, rb_ref[:, 0:1])
            chosen = jnp.zeros((ne, LANES), F32)
            for k in range(TOP_K):
                chosen = chosen + (eiota == ids[k]).astype(F32)
            chosen_bf = chosen.astype(BF16)
            before = _dot(chosen_bf, tri_ref[...]) - chosen + counted
            ranks = [jnp.sum(jnp.where(eiota == ids[k], before, 0.0), axis=0, keepdims=True)
                     for k in range(TOP_K)]
            counted = counted + _dot(chosen_bf, ones_ref[...])
            wsel = jnp.concatenate(vals, axis=0)
            w_ref[:, ls] = wsel / jnp.sum(wsel, axis=0, keepdims=True) * ROUTED_SCALE
            idx_ref[:, ls] = jnp.concatenate(ids, axis=0)
            rank_ref[:, ls] = jnp.concatenate(ranks, axis=0).astype(I32)
    carry[...] = counted
    cnt_ref[...] = counted


def _outproj(mix, x2, mod, w_out_bf, ln_g, ln_b, w_router_t_bf, router_bias, alpha, seq):
    t, d = x2.shape
    ne = w_router_t_bf.shape[0]
    tm = ROUTE_TOKENS
    per_batch = seq // tm
    nchunk = d // 2 // LANES
    tri = jnp.asarray(np.triu(np.ones((LANES, LANES), np.float32)), BF16)
    ones = jnp.ones((LANES, LANES), BF16)
    rb = jnp.broadcast_to(router_bias.astype(F32)[:, None], (ne, LANES))
    c2 = lambda i: (0, 0)
    return pl.pallas_call(
        functools.partial(_outproj_kernel, alpha),
        out_shape=(jax.ShapeDtypeStruct((t, d), F32),
                   jax.ShapeDtypeStruct((t * nchunk, LANES), U32),
                   jax.ShapeDtypeStruct((TOP_K, t), I32),
                   jax.ShapeDtypeStruct((TOP_K, t), F32),
                   jax.ShapeDtypeStruct((TOP_K, t), I32),
                   jax.ShapeDtypeStruct((ne, LANES), F32)),
        grid=(t // tm,),
        in_specs=[pl.BlockSpec((tm, d), lambda i: (i, 0)),
                  pl.BlockSpec((tm, d), lambda i: (i, 0)),
                  pl.BlockSpec((1, N_MOD, d), lambda i: (i // per_batch, 0, 0)),
                  pl.BlockSpec((d, d), c2),
                  pl.BlockSpec((1, d), c2),
                  pl.BlockSpec((1, d), c2),
                  pl.BlockSpec((ne, d), c2),
                  pl.BlockSpec((ne, LANES), c2),
                  pl.BlockSpec((LANES, LANES), c2),
                  pl.BlockSpec((LANES, LANES), c2)],
        out_specs=(pl.BlockSpec((tm, d), lambda i: (i, 0)),
                   pl.BlockSpec((tm * nchunk, LANES), lambda i: (i, 0)),
                   pl.BlockSpec((TOP_K, tm), lambda i: (0, i)),
                   pl.BlockSpec((TOP_K, tm), lambda i: (0, i)),
                   pl.BlockSpec((TOP_K, tm), lambda i: (0, i)),
                   pl.BlockSpec((ne, LANES), c2)),
        scratch_shapes=[pltpu.VMEM((ne, LANES), F32)],
        compiler_params=pltpu.CompilerParams(dimension_semantics=("arbitrary",), vmem_limit_bytes=VMEM_LIMIT),
        name="outproj",
    )(mix, x2, mod, w_out_bf, ln_g.reshape(1, d), ln_b.reshape(1, d), w_router_t_bf, rb, tri, ones)


def _slots_kernel(idx_ref, rank_ref, start_ref, pos_ref):
    ne = start_ref.shape[0]
    tm = idx_ref.shape[1]
    eiota = lax.broadcasted_iota(I32, (ne, tm), 0)
    start = jnp.concatenate([start_ref[...]] * (tm // LANES), axis=-1)
    rows = []
    for k in range(TOP_K):
        rows.append(jnp.sum(jnp.where(eiota == idx_ref[k:k + 1, :], start, 0), axis=0, keepdims=True))
    pos_ref[...] = jnp.concatenate(rows, axis=0) + rank_ref[...]


def _slots(idx_t, rank_t, start):
    k, t = idx_t.shape
    ne = start.shape[0]
    tm = 512
    spec = pl.BlockSpec((k, tm), lambda i: (0, i))
    return pl.pallas_call(
        _slots_kernel,
        out_shape=jax.ShapeDtypeStruct((k, t), I32),
        grid=(t // tm,),
        in_specs=[spec, spec, pl.BlockSpec((ne, LANES), lambda i: (0, 0))],
        out_specs=spec,
        compiler_params=pltpu.CompilerParams(dimension_semantics=("arbitrary",)),
        name="slots",
    )(idx_t, rank_t, jnp.broadcast_to(start[:, None], (ne, LANES)))


def _invert_kernel(pad, pos_ref, lst_ref):
    ntok = pos_ref.shape[0]
    k = pl.program_id(0)
    n = lst_ref.shape[0] - pad
    unroll = 8

    @pl.when(k == 0)
    def _():
        for p in range(pad):
            lst_ref[n + p] = 0

    step = (1 << 16) + SUBLANES
    first = (k * ntok) << 16

    def body(j, carry):
        word = first + j * (unroll * step)
        for u in range(unroll):
            lst_ref[pos_ref[j * unroll + u]] = word + u * step
        return carry

    lax.fori_loop(0, ntok // unroll, body, 0)


def _invert(pos_flat, ntok, pad):
    n = pos_flat.shape[0]
    assert n <= 1 << 16 and ntok * SUBLANES <= 1 << 16
    return pl.pallas_call(
        functools.partial(_invert_kernel, pad),
        out_shape=jax.ShapeDtypeStruct((n + pad,), I32),
        grid=(n // ntok,),
        in_specs=[pl.BlockSpec((ntok,), lambda k: (k,), memory_space=pltpu.SMEM)],
        out_specs=pl.BlockSpec(memory_space=pltpu.SMEM),
        compiler_params=pltpu.CompilerParams(dimension_semantics=("arbitrary",)),
        name="invert",
    )(pos_flat)


def _moe_kernel(cnt_ref, start_ref, lst_ref, h2p_ref, wg_ref, wu_ref, wd_ref, out_hbm,
                xbuf, stage, sem, state):
    e = pl.program_id(0)
    nchunk = SUBLANES
    ring = stage.shape[0]
    m = stage.shape[1] // nchunk
    nlist = lst_ref.shape[0] - m
    n = cnt_ref[e]

    def wait_slot(slot):
        pltpu.make_async_copy(stage.at[slot], out_hbm.at[pl.ds(0, m * nchunk), :], sem.at[slot]).wait()

    def send_rows(slot, base, occupied):
        for r in range(m):
            row = jnp.where(r < occupied, lax.shift_right_logical(lst_ref[base + r], 16), nlist + slot * m + r)
            pltpu.make_async_copy(stage.at[slot, pl.ds(r * nchunk, nchunk), :],
                                  out_hbm.at[pl.ds(pl.multiple_of(row * nchunk, nchunk), nchunk), :],
                                  sem.at[slot]).start()

    @pl.when(e == 0)
    def _():
        zeros = jnp.zeros((m, LANES), F32)
        packed = jnp.concatenate([_pack_bf16_pairs(zeros, zeros)] * nchunk, axis=0)
        for s in range(ring):
            stage[s] = packed
        for s in range(ring):
            fill = pltpu.make_async_copy(stage.at[s], out_hbm.at[pl.ds((nlist + s * m) * nchunk, m * nchunk), :],
                                         sem.at[s])
            fill.start()
            fill.wait()
        state[0] = 1
        state[1] = 0
        state[2] = 0

    def run(j, carry):
        done = state[0]
        slot = lax.rem(done, ring)
        base = start_ref[e] + j * m

        @pl.when(done >= ring)
        def _():
            wait_slot(slot)

        send_rows(lax.rem(done + ring - 1, ring), state[1], state[2])
        for r in range(m):
            src = pl.multiple_of(lst_ref[base + r] & 0xFFFF, nchunk)
            xbuf[r * nchunk:(r + 1) * nchunk, :] = h2p_ref[pl.ds(src, nchunk), :]
        los, his = [], []
        for c in range(nchunk):
            lo, hi = _unpack_pairs_f32(xbuf[pl.ds(c, m, stride=nchunk), :])
            los.append(lo.astype(BF16))
            his.append(hi.astype(BF16))
        xrows = jnp.concatenate(los + his, axis=-1)
        g = _dot(xrows, wg_ref[0].astype(BF16))
        u = _dot(xrows, wu_ref[0].astype(BF16))
        act = (g * _sigmoid(g) * u).astype(BF16)
        _store_packed_rows(stage.at[slot], _dot(act, wd_ref[0].astype(BF16)))
        state[0] = done + 1
        state[1] = base
        state[2] = jnp.minimum(n - j * m, m)
        return carry

    lax.fori_loop(0, (n + m - 1) // m, run, 0)

    @pl.when(e == pl.num_programs(0) - 1)
    def _():
        done = state[0]
        send_rows(lax.rem(done + ring - 1, ring), state[1], state[2])
        for s in range(ring):
            @pl.when(done > s)
            def _():
                wait_slot(s)


def _moe(h2p, lst, counts, starts, w_gate, w_up, w_down):
    ne, d, de = w_gate.shape
    nchunk = d // 2 // LANES
    assert nchunk == SUBLANES
    m, ring = MOE_RUN, MOE_RING
    nlist = lst.shape[0] - m
    wspec = lambda e, *_: (e, 0, 0)
    grid_spec = pltpu.PrefetchScalarGridSpec(
        num_scalar_prefetch=3,
        grid=(ne,),
        in_specs=[pl.BlockSpec(memory_space=pltpu.VMEM),
                  pl.BlockSpec((1, d, de), wspec),
                  pl.BlockSpec((1, d, de), wspec),
                  pl.BlockSpec((1, de, d), wspec)],
        out_specs=pl.BlockSpec(memory_space=pl.ANY),
        scratch_shapes=[pltpu.VMEM((m * nchunk, LANES), U32), pltpu.VMEM((ring, m * nchunk, LANES), U32),
                        pltpu.SemaphoreType.DMA((ring,)), pltpu.SMEM((3,), I32)],
    )
    return pl.pallas_call(
        _moe_kernel,
        out_shape=jax.ShapeDtypeStruct(((nlist + ring * m) * nchunk, LANES), U32),
        grid_spec=grid_spec,
        compiler_params=pltpu.CompilerParams(dimension_semantics=("arbitrary",), vmem_limit_bytes=VMEM_LIMIT),
        name="moe",
    )(counts, starts, lst, h2p, w_gate, w_up, w_down)


def _combine_kernel(alpha, *refs):
    y_refs = refs[:TOP_K]
    w_ref, x1_ref, mod_ref, wsg_ref, wsu_ref, wsd_ref, g_ref, b_ref, o_ref, wbc, routed = refs[TOP_K:]
    tt, d = x1_ref.shape
    half = d // 2
    nchunk = SUBLANES

    x1 = x1_ref[...]
    h2 = (x1 * (1.0 + mod_ref[0, 4:5, :]) + mod_ref[0, 3:4, :]).astype(BF16)
    g = _dot(h2, wsg_ref[...])
    u = _dot(h2, wsu_ref[...])
    shared = _dot((g * _sigmoid(g) * u).astype(BF16), wsd_ref[...])
    for k in range(TOP_K):
        wbc[k] = jnp.broadcast_to(w_ref[:, k:k + 1], (tt, LANES))

    for c in range(nchunk):
        acc_lo = acc_hi = None
        for k in range(TOP_K):
            lo, hi = _unpack_pairs_f32(y_refs[k][pl.ds(c, tt, stride=nchunk), :])
            wk = wbc[k]
            acc_lo = lo * wk if k == 0 else acc_lo + lo * wk
            acc_hi = hi * wk if k == 0 else acc_hi + hi * wk
        routed[:, c * LANES:(c + 1) * LANES] = acc_lo
        routed[:, half + c * LANES: half + (c + 1) * LANES] = acc_hi

    z = alpha * x1 + mod_ref[0, 5:6, :] * (routed[...] + shared)
    o_ref[...] = _layer_norm(z, g_ref[...], b_ref[...])


def _combine(y_ranked, w_tok, x1, mod, wsg_bf, wsu_bf, wsd_bf, ln_g, ln_b, alpha, seq):
    t, d = x1.shape
    ds_ = wsg_bf.shape[1]
    tt = COMBINE_TOKENS
    per_batch = seq // tt
    nsteps = t // tt
    nchunk = d // 2 // LANES
    assert nchunk == SUBLANES
    c2 = lambda i: (0, 0)
    y_specs = [pl.BlockSpec((tt * nchunk, LANES), functools.partial(lambda k, i: (k * nsteps + i, 0), k))
               for k in range(TOP_K)]
    return pl.pallas_call(
        functools.partial(_combine_kernel, alpha),
        out_shape=jax.ShapeDtypeStruct((t, d), F32),
        grid=(nsteps,),
        in_specs=y_specs + [pl.BlockSpec((tt, TOP_K), lambda i: (i, 0)),
                            pl.BlockSpec((tt, d), lambda i: (i, 0)),
                            pl.BlockSpec((1, N_MOD, d), lambda i: (i // per_batch, 0, 0)),
                            pl.BlockSpec((d, ds_), c2),
                            pl.BlockSpec((d, ds_), c2),
                            pl.BlockSpec((ds_, d), c2),
                            pl.BlockSpec((1, d), c2),
                            pl.BlockSpec((1, d), c2)],
        out_specs=pl.BlockSpec((tt, d), lambda i: (i, 0)),
        scratch_shapes=[pltpu.VMEM((TOP_K, tt, LANES), F32), pltpu.VMEM((tt, d), F32)],
        compiler_params=pltpu.CompilerParams(dimension_semantics=("arbitrary",), vmem_limit_bytes=VMEM_LIMIT),
        name="combine",
    )(*([y_ranked] * TOP_K), w_tok, x1, mod, wsg_bf, wsu_bf, wsd_bf, ln_g.reshape(1, d), ln_b.reshape(1, d))


def kernel(x, c, w_ada, b_ada, w_in, rel_bias, attn_gain, ret_gain, w_out, ln1_gain, ln1_bias, w_router, router_bias,
           w_gate, w_up, w_down, ws_gate, ws_up, ws_down, ln2_gain, ln2_bias):
    batch, seq, d = x.shape
    depth = w_ada.shape[0]
    alpha = (2.0 * depth) ** 0.25
    t = batch * seq
    assert t % COMBINE_TOKENS == 0
    xt = x.reshape(t, d)
    c_pad = jnp.zeros((8, d), F32).at[:batch].set(c)
    for l in range(depth):
        mod = _ada(c_pad, w_ada[l], b_ada[l])[:batch].reshape(batch, N_MOD, d)
        proj = _inproj(xt, mod, w_in[l], seq)
        mix = _mixer(proj, rel_bias[l], attn_gain[l], ret_gain[l], batch, seq)
        x1, h2p, idx_t, w_t, rank_t, cnt = _outproj(mix, xt, mod, w_out[l].astype(BF16), ln1_gain[l], ln1_bias[l],
                                                    w_router[l].T.astype(BF16), router_bias[l], alpha, seq)
        counts = cnt[:, 0].astype(I32)
        starts = jnp.cumsum(counts) - counts
        pos = _slots(idx_t, rank_t, starts)
        lst = _invert(pos.reshape(-1), t, MOE_RUN)
        y_ranked = _moe(h2p, lst, counts, starts, w_gate[l], w_up[l], w_down[l])
        xt = _combine(y_ranked, w_t.T, x1, mod, ws_gate[l].astype(BF16), ws_up[l].astype(BF16),
                      ws_down[l].astype(BF16), ln2_gain[l], ln2_bias[l], alpha, seq)
    return xt.reshape(batch, seq, d)
```

```python
import functools

import jax
import jax.numpy as jnp
import numpy as np
from jax import lax
from jax.experimental import pallas as pl
from jax.experimental.pallas import tpu as pltpu

F32 = jnp.float32
BF16 = jnp.bfloat16
U32 = jnp.uint32
I32 = jnp.int32

CHUNK = 64
LEFT_CHUNKS = 8
MAX_REL = 256
ATTN_HEADS = 8
HEAD_DIM = 128
RET_HEADS = 8
RET_KEY_DIM = 64
RET_VALUE_DIM = 128
ROPE_BASE = 10000.0
N_GROUPS = 8
TOPK_GROUPS = 4
TOP_K = 8
ROUTED_SCALE = 2.5
EPS = 1e-5
N_MOD = 6
LANES = 128
SUBLANES = 8

SEQ_BLOCK = 256
ROUTE_TOKENS = 512
ROUTE_SUBTILE = 256
MOE_RUN = 128
MOE_RING = 3
COMBINE_TOKENS = 256
NEG_BIG = -1e30
VMEM_LIMIT = 56 * 1024 * 1024


def _sigmoid(v):
    return 1.0 / (1.0 + jnp.exp(-v))


def _dot(a, b):
    return jnp.dot(a, b, preferred_element_type=F32)


def _dot_t(a, b):
    return lax.dot_general(a, b, (((1,), (1,)), ((), ())), preferred_element_type=F32)


def _ada_kernel(c_ref, w_ref, b_ref, o_ref):
    c = c_ref[...]
    s = (c * _sigmoid(c)).astype(BF16)
    o_ref[...] = _dot(s, w_ref[...].astype(BF16)) + b_ref[...]


def _ada(c_pad, w_ada, b_ada):
    d, n = w_ada.shape
    tn = 1024
    return pl.pallas_call(
        _ada_kernel,
        out_shape=jax.ShapeDtypeStruct((8, n), F32),
        grid=(n // tn,),
        in_specs=[pl.BlockSpec((8, d), lambda j: (0, 0)),
                  pl.BlockSpec((d, tn), lambda j: (0, j)),
                  pl.BlockSpec((1, tn), lambda j: (0, j))],
        out_specs=pl.BlockSpec((8, tn), lambda j: (0, j)),
        compiler_params=pltpu.CompilerParams(dimension_semantics=("arbitrary",), vmem_limit_bytes=VMEM_LIMIT),
        name="ada",
    )(c_pad, w_ada, b_ada.reshape(1, n))


def _inproj_kernel(x_ref, mod_ref, w_ref, o_ref, wbf_ref):
    @pl.when(pl.program_id(1) == 0)
    def _():
        wbf_ref[...] = w_ref[...].astype(BF16)

    shift = mod_ref[0, 0:1, :]
    scale = mod_ref[0, 1:2, :]
    h = (x_ref[...] * (1.0 + scale) + shift).astype(BF16)
    o_ref[...] = _dot(h, wbf_ref[...]).astype(o_ref.dtype)


def _inproj(x2, mod, w_in, seq):
    t, d = x2.shape
    n = w_in.shape[1]
    tm, tn = 1024, 1024
    per_batch = seq // tm
    return pl.pallas_call(
        _inproj_kernel,
        out_shape=jax.ShapeDtypeStruct((t, n), BF16),
        grid=(n // tn, t // tm),
        in_specs=[pl.BlockSpec((tm, d), lambda j, i: (i, 0)),
                  pl.BlockSpec((1, N_MOD, d), lambda j, i: (i // per_batch, 0, 0)),
                  pl.BlockSpec((d, tn), lambda j, i: (0, j))],
        out_specs=pl.BlockSpec((tm, tn), lambda j, i: (i, j)),
        scratch_shapes=[pltpu.VMEM((d, tn), BF16)],
        compiler_params=pltpu.CompilerParams(dimension_semantics=("arbitrary", "arbitrary"),
                                             vmem_limit_bytes=VMEM_LIMIT),
        name="inproj",
    )(x2, mod, w_in)


def _ret_decay_consts(blk):
    h = np.arange(RET_HEADS, dtype=np.float64)
    log_g = np.log(1.0 - 2.0 ** (-5.0 - h))
    n = np.arange(blk, dtype=np.float64)
    diff = n[:, None] - n[None, :]
    same = (n[:, None] // CHUNK) == (n[None, :] // CHUNK)
    later = (n[:, None] // CHUNK) > (n[None, :] // CHUNK)
    expo = np.where(same, np.abs(diff), diff)
    kscale = RET_KEY_DIM ** -0.5
    decay = np.where(same | later, np.exp(log_g[:, None, None] * expo[None]), 0.0) * kscale
    xi = np.exp(log_g[:, None] * (n[None, :] + 1.0))
    zeta = np.exp(log_g[:, None] * (blk - 1.0 - n[None, :])) * kscale
    g_blk = np.exp(log_g * blk)
    xi = np.broadcast_to(xi[:, :, None], (RET_HEADS, blk, LANES))
    zeta = np.broadcast_to(zeta[:, :, None], (RET_HEADS, blk, LANES))
    return (jnp.asarray(decay, F32), jnp.asarray(xi, F32), jnp.asarray(zeta, F32), [float(v) for v in g_blk])


def _mixer_kernel(g_blk, qa, ka0, ka1, ka2, va0, va1, va2, qb, kb, vb, gb, bias, cos, sin, decay, xi, zeta,
                  again, rgain, out, state, oa):
    i = pl.program_id(1)
    blk = qa.shape[0]

    @pl.when(i == 0)
    def _():
        state[...] = jnp.zeros_like(state)

    ok0 = i >= 2
    ok1 = i >= 1
    scale = HEAD_DIM ** -0.5
    ssq = jnp.zeros((blk, 1), F32)

    def raw_scores(h):
        sl = slice(h * HEAD_DIM, (h + 1) * HEAD_DIM)
        q = (qa[:, sl].astype(F32) * scale).astype(BF16)
        return _dot_t(q, ka0[:, sl]), _dot_t(q, ka1[:, sl]), _dot_t(q, ka2[:, sl])

    pending = raw_scores(0)
    for h in range(ATTN_HEADS):
        sl = slice(h * HEAD_DIM, (h + 1) * HEAD_DIM)
        r0, r1, r2 = pending
        if h + 1 < ATTN_HEADS:
            pending = raw_scores(h + 1)
        s0 = jnp.where(ok0, r0 + bias[h, :, 0:blk], NEG_BIG)
        s1 = jnp.where(ok1, r1 + bias[h, :, blk:2 * blk], NEG_BIG)
        s2 = r2 + bias[h, :, 2 * blk:3 * blk]
        m = jnp.max(jnp.maximum(jnp.maximum(s0, s1), s2), axis=-1, keepdims=True)
        p0 = jnp.exp(s0 - m)
        p1 = jnp.exp(s1 - m)
        p2 = jnp.exp(s2 - m)
        denom = jnp.sum(p0 + p1 + p2, axis=-1, keepdims=True)
        o = _dot(p0.astype(BF16), va0[:, sl]) + _dot(p1.astype(BF16), va1[:, sl]) + _dot(p2.astype(BF16), va2[:, sl])
        o = o * (1.0 / denom)
        oa[:, sl] = o
        ssq = ssq + jnp.sum(o * o, axis=-1, keepdims=True)
    width_a = ATTN_HEADS * HEAD_DIM
    inv_rms = lax.rsqrt(ssq * (1.0 / width_a) + EPS)
    out[:, 0:width_a] = (oa[...] * inv_rms * again[...]).astype(out.dtype)

    cosv = cos[...]
    sinv = sin[...]
    lane = lax.broadcasted_iota(jnp.int32, (blk, LANES), 1)
    first_half = (lane % RET_KEY_DIM) < (RET_KEY_DIM // 2)
    low_head = lane < RET_KEY_DIM

    def rope(v):
        rot = jnp.where(first_half, pltpu.roll(v, LANES - RET_KEY_DIM // 2, 1), pltpu.roll(v, RET_KEY_DIM // 2, 1))
        return v * cosv + rot * sinv

    for j in range(RET_HEADS // 2):
        sl2 = slice(j * LANES, (j + 1) * LANES)
        qr = rope(qb[:, sl2].astype(F32))
        k2 = rope(kb[:, sl2].astype(F32)).astype(BF16)
        for hh in range(2):
            h = 2 * j + hh
            slv = slice(h * RET_VALUE_DIM, (h + 1) * RET_VALUE_DIM)
            qm = jnp.where(low_head if hh == 0 else jnp.logical_not(low_head), qr, 0.0).astype(BF16)
            v = vb[:, slv]
            st = state[h]
            scores = _dot_t(qm, k2) * decay[h]
            ret = _dot(scores.astype(BF16), v) + _dot(qm, st.astype(BF16)) * xi[h]
            zv = (v.astype(F32) * zeta[h]).astype(BF16)
            upd = lax.dot_general(k2, zv, (((0,), (0,)), ((), ())), preferred_element_type=F32)
            state[h] = g_blk[h] * st + upd
            mu = jnp.mean(ret, axis=-1, keepdims=True)
            cen = ret - mu
            var = jnp.mean(cen * cen, axis=-1, keepdims=True)
            gate = gb[:, slv].astype(F32)
            yb = gate * _sigmoid(gate) * (cen * lax.rsqrt(var + EPS)) * rgain[:, slv]
            out[:, width_a + h * RET_VALUE_DIM: width_a + (h + 1) * RET_VALUE_DIM] = yb.astype(out.dtype)


def _attn_bias_table(rel_bias, blk):
    nk = 3 * blk
    period = nk + blk
    offs = np.concatenate([np.arange(nk), np.zeros((1,), np.int64), np.arange(-(blk - 1), 0)])
    rel_idx = np.clip(2 * blk - offs, -(CHUNK - 1), MAX_REL) + (CHUNK - 1)
    line = rel_bias[:, jnp.asarray(rel_idx)].astype(F32)
    heads = line.shape[0]
    flat = jnp.broadcast_to(line[:, None, :], (heads, blk, period)).reshape(heads, blk * period)
    tab = flat[:, :blk * (period - 1)].reshape(heads, blk, period - 1)[:, :, :nk]
    r = np.arange(blk)[:, None]
    c = np.arange(nk)[None, :]
    qc = r // CHUNK + (2 * blk) // CHUNK
    kc = c // CHUNK
    in_band = (kc <= qc) & (kc >= qc - LEFT_CHUNKS)
    return jnp.where(jnp.asarray(in_band)[None], tab, NEG_BIG)


def _rope_tables(seq):
    half = RET_KEY_DIM // 2
    inv = ROPE_BASE ** (-jnp.arange(half, dtype=F32) / half)
    ang = jnp.arange(seq, dtype=F32)[:, None] * inv[None, :]
    cos, sin = jnp.cos(ang), jnp.sin(ang)
    reps = LANES // RET_KEY_DIM
    cos_t = jnp.tile(jnp.concatenate([cos, cos], axis=-1), (1, reps))
    sin_t = jnp.tile(jnp.concatenate([-sin, sin], axis=-1), (1, reps))
    return cos_t, sin_t


def _mixer(proj, rel_bias, attn_gain, ret_gain, batch, seq):
    t = proj.shape[0]
    blk = SEQ_BLOCK
    assert 2 * blk == LEFT_CHUNKS * CHUNK and seq % blk == 0
    nb = seq // blk
    wa = ATTN_HEADS * HEAD_DIM
    wqk = RET_HEADS * RET_KEY_DIM
    wv = RET_HEADS * RET_VALUE_DIM
    assert wa == wv == 2 * wqk
    bias = _attn_bias_table(rel_bias, blk)
    cos_t, sin_t = _rope_tables(seq)
    decay, xi, zeta, g_blk = _ret_decay_consts(blk)

    def row(b, i):
        return b * nb + i

    def kspec(back, col):
        return pl.BlockSpec((blk, wa), lambda b, i: (row(b, jnp.maximum(i - back, 0)), col))

    const3 = lambda b, i: (0, 0, 0)
    in_specs = [
        pl.BlockSpec((blk, wa), lambda b, i: (row(b, i), 0)),
        kspec(2, 1), kspec(1, 1), kspec(0, 1),
        kspec(2, 2), kspec(1, 2), kspec(0, 2),
        pl.BlockSpec((blk, wqk), lambda b, i: (row(b, i), 3 * wa // wqk)),
        pl.BlockSpec((blk, wqk), lambda b, i: (row(b, i), 3 * wa // wqk + 1)),
        pl.BlockSpec((blk, wv), lambda b, i: (row(b, i), (3 * wa + 2 * wqk) // wv)),
        pl.BlockSpec((blk, wv), lambda b, i: (row(b, i), (3 * wa + 2 * wqk) // wv + 1)),
        pl.BlockSpec((ATTN_HEADS, blk, 3 * blk), const3),
        pl.BlockSpec((blk, LANES), lambda b, i: (i, 0)),
        pl.BlockSpec((blk, LANES), lambda b, i: (i, 0)),
        pl.BlockSpec((RET_HEADS, blk, blk), const3),
        pl.BlockSpec((RET_HEADS, blk, LANES), const3),
        pl.BlockSpec((RET_HEADS, blk, LANES), const3),
        pl.BlockSpec((1, wa), lambda b, i: (0, 0)),
        pl.BlockSpec((1, wv), lambda b, i: (0, 0)),
    ]
    return pl.pallas_call(
        functools.partial(_mixer_kernel, g_blk),
        out_shape=jax.ShapeDtypeStruct((t, wa + wv), BF16),
        grid=(batch, nb),
        in_specs=in_specs,
        out_specs=pl.BlockSpec((blk, wa + wv), lambda b, i: (row(b, i), 0)),
        scratch_shapes=[pltpu.VMEM((RET_HEADS, LANES, RET_VALUE_DIM), F32), pltpu.VMEM((blk, wa), F32)],
        compiler_params=pltpu.CompilerParams(dimension_semantics=("arbitrary", "arbitrary"),
                                             vmem_limit_bytes=VMEM_LIMIT),
        name="mixer",
    )(proj, proj, proj, proj, proj, proj, proj, proj, proj, proj, proj, bias, cos_t, sin_t, decay, xi, zeta,
      attn_gain.reshape(1, wa), ret_gain.reshape(1, wv))


def _layer_norm(z, gain, bias):
    mu = jnp.mean(z, axis=-1, keepdims=True)
    cen = z - mu
    var = jnp.mean(cen * cen, axis=-1, keepdims=True)
    return cen * lax.rsqrt(var + EPS) * gain + bias


def _pack_bf16_pairs(lo, hi):
    return pltpu.pack_elementwise([lo, hi], packed_dtype=BF16)


def _unpack_pairs_f32(words):
    lo = pltpu.unpack_elementwise(words, index=0, packed_dtype=BF16, unpacked_dtype=F32)
    hi = pltpu.unpack_elementwise(words, index=1, packed_dtype=BF16, unpacked_dtype=F32)
    return lo, hi


def _store_packed_rows(ref, val):
    tm, d = val.shape
    half = d // 2
    nchunk = half // LANES
    for c in range(nchunk):
        words = _pack_bf16_pairs(val[:, c * LANES:(c + 1) * LANES], val[:, half + c * LANES: half + (c + 1) * LANES])
        ref[pl.ds(c, tm, stride=nchunk), :] = words


def _route_tile(scores_t, bias_col):
    ne, tm = scores_t.shape
    per_group = ne // N_GROUPS
    biased = scores_t + bias_col
    neg_inf = -jnp.inf
    iota_g = lax.broadcasted_iota(I32, (per_group, tm), 0)
    gscore = []
    for g in range(N_GROUPS):
        b = biased[g * per_group:(g + 1) * per_group, :]
        m1 = jnp.max(b, axis=0, keepdims=True)
        first = jnp.min(jnp.where(b == m1, iota_g, per_group), axis=0, keepdims=True)
        m2 = jnp.max(jnp.where(iota_g == first, neg_inf, b), axis=0, keepdims=True)
        gscore.append(m1 + m2)
    masked_parts = []
    for g in range(N_GROUPS):
        beaten = jnp.zeros((1, tm), I32)
        for o in range(N_GROUPS):
            if o == g:
                continue
            wins = (gscore[o] >= gscore[g]) if o < g else (gscore[o] > gscore[g])
            beaten = beaten + wins.astype(I32)
        keep = beaten < TOPK_GROUPS
        masked_parts.append(jnp.where(keep, biased[g * per_group:(g + 1) * per_group, :], neg_inf))
    masked = jnp.concatenate(masked_parts, axis=0)
    eiota = lax.broadcasted_iota(I32, (ne, tm), 0)
    ids, vals = [], []
    for _ in range(TOP_K):
        m = jnp.max(masked, axis=0, keepdims=True)
        first = jnp.min(jnp.where(masked == m, eiota, ne), axis=0, keepdims=True)
        hit = eiota == first
        vals.append(jnp.sum(jnp.where(hit, scores_t, 0.0), axis=0, keepdims=True))
        ids.append(first)
        masked = jnp.where(hit, neg_inf, masked)
    return ids, vals


def _outproj_kernel(alpha, mix_ref, x_ref, mod_ref, wout_ref, g_ref, b_ref, wrt_ref, rb_ref, tri_ref, ones_ref,
                    x1_ref, h2p_ref, idx_ref, w_ref, rank_ref, cnt_ref, carry):
    i = pl.program_id(0)
    tm, d = x_ref.shape
    nchunk = d // 2 // LANES
    ne = wrt_ref.shape[0]
    sub = ROUTE_SUBTILE

    @pl.when(i == 0)
    def _():
        carry[...] = jnp.zeros_like(carry)

    gate1 = mod_ref[0, 2:3, :]
    scale2 = 1.0 + mod_ref[0, 4:5, :]
    shift2 = mod_ref[0, 3:4, :]
    eiota = lax.broadcasted_iota(I32, (ne, LANES), 0)
    counted = carry[...]
    for part in range(tm // sub):
        rs = slice(part * sub, (part + 1) * sub)
        y = _dot(mix_ref[rs, :], wout_ref[...])
        x1 = _layer_norm(alpha * x_ref[rs, :] + gate1 * y, g_ref[...], b_ref[...])
        x1_ref[rs, :] = x1
        h2 = x1 * scale2 + shift2
        _store_packed_rows(h2p_ref.at[pl.ds(part * sub * nchunk, sub * nchunk), :], h2)
        logits_t = _dot_t(wrt_ref[...], h2.astype(BF16))
        for q in range(sub // LANES):
            ls = slice(part * sub + q * LANES, part * sub + (q + 1) * LANES)
            scores_t = _sigmoid(logits_t[:, q * LANES:(q + 1) * LANES])
            ids, vals = _route_tile(scores_t, rb_ref[:, 0:1])
            chosen = jnp.zeros((ne, LANES), F32)
            for k in range(TOP_K):
                chosen = chosen + (eiota == ids[k]).astype(F32)
            chosen_bf = chosen.astype(BF16)
            before = _dot(chosen_bf, tri_ref[...]) - chosen + counted
            ranks = [jnp.sum(jnp.where(eiota == ids[k], before, 0.0), axis=0, keepdims=True)
                     for k in range(TOP_K)]
            counted = counted + _dot(chosen_bf, ones_ref[...])
            wsel = jnp.concatenate(vals, axis=0)
            w_ref[:, ls] = wsel / jnp.sum(wsel, axis=0, keepdims=True) * ROUTED_SCALE
            idx_ref[:, ls] = jnp.concatenate(ids, axis=0)
            rank_ref[:, ls] = jnp.concatenate(ranks, axis=0).astype(I32)
    carry[...] = counted
    cnt_ref[...] = counted


def _outproj(mix, x2, mod, w_out_bf, ln_g, ln_b, w_router_t_bf, router_bias, alpha, seq):
    t, d = x2.shape
    ne = w_router_t_bf.shape[0]
    tm = ROUTE_TOKENS
    per_batch = seq // tm
    nchunk = d // 2 // LANES
    tri = jnp.asarray(np.triu(np.ones((LANES, LANES), np.float32)), BF16)
    ones = jnp.ones((LANES, LANES), BF16)
    rb = jnp.broadcast_to(router_bias.astype(F32)[:, None], (ne, LANES))
    c2 = lambda i: (0, 0)
    return pl.pallas_call(
        functools.partial(_outproj_kernel, alpha),
        out_shape=(jax.ShapeDtypeStruct((t, d), F32),
                   jax.ShapeDtypeStruct((t * nchunk, LANES), U32),
                   jax.ShapeDtypeStruct((TOP_K, t), I32),
                   jax.ShapeDtypeStruct((TOP_K, t), F32),
                   jax.ShapeDtypeStruct((TOP_K, t), I32),
                   jax.ShapeDtypeStruct((ne, LANES), F32)),
        grid=(t // tm,),
        in_specs=[pl.BlockSpec((tm, d), lambda i: (i, 0)),
                  pl.BlockSpec((tm, d), lambda i: (i, 0)),
                  pl.BlockSpec((1, N_MOD, d), lambda i: (i // per_batch, 0, 0)),
                  pl.BlockSpec((d, d), c2),
                  pl.BlockSpec((1, d), c2),
                  pl.BlockSpec((1, d), c2),
                  pl.BlockSpec((ne, d), c2),
                  pl.BlockSpec((ne, LANES), c2),
                  pl.BlockSpec((LANES, LANES), c2),
                  pl.BlockSpec((LANES, LANES), c2)],
        out_specs=(pl.BlockSpec((tm, d), lambda i: (i, 0)),
                   pl.BlockSpec((tm * nchunk, LANES), lambda i: (i, 0)),
                   pl.BlockSpec((TOP_K, tm), lambda i: (0, i)),
                   pl.BlockSpec((TOP_K, tm), lambda i: (0, i)),
                   pl.BlockSpec((TOP_K, tm), lambda i: (0, i)),
                   pl.BlockSpec((ne, LANES), c2)),
        scratch_shapes=[pltpu.VMEM((ne, LANES), F32)],
        compiler_params=pltpu.CompilerParams(dimension_semantics=("arbitrary",), vmem_limit_bytes=VMEM_LIMIT),
        name="outproj",
    )(mix, x2, mod, w_out_bf, ln_g.reshape(1, d), ln_b.reshape(1, d), w_router_t_bf, rb, tri, ones)


def _slots_kernel(idx_ref, rank_ref, start_ref, pos_ref):
    ne = start_ref.shape[0]
    tm = idx_ref.shape[1]
    eiota = lax.broadcasted_iota(I32, (ne, tm), 0)
    start = jnp.concatenate([start_ref[...]] * (tm // LANES), axis=-1)
    rows = []
    for k in range(TOP_K):
        rows.append(jnp.sum(jnp.where(eiota == idx_ref[k:k + 1, :], start, 0), axis=0, keepdims=True))
    pos_ref[...] = jnp.concatenate(rows, axis=0) + rank_ref[...]


def _slots(idx_t, rank_t, start):
    k, t = idx_t.shape
    ne = start.shape[0]
    tm = 512
    spec = pl.BlockSpec((k, tm), lambda i: (0, i))
    return pl.pallas_call(
        _slots_kernel,
        out_shape=jax.ShapeDtypeStruct((k, t), I32),
        grid=(t // tm,),
        in_specs=[spec, spec, pl.BlockSpec((ne, LANES), lambda i: (0, 0))],
        out_specs=spec,
        compiler_params=pltpu.CompilerParams(dimension_semantics=("arbitrary",)),
        name="slots",
    )(idx_t, rank_t, jnp.broadcast_to(start[:, None], (ne, LANES)))


def _invert_kernel(pad, pos_ref, lst_ref):
    ntok = pos_ref.shape[0]
    k = pl.program_id(0)
    n = lst_ref.shape[0] - pad
    unroll = 8

    @pl.when(k == 0)
    def _():
        for p in range(pad):
            lst_ref[n + p] = 0

    step = (1 << 16) + SUBLANES
    first = (k * ntok) << 16

    def body(j, carry):
        word = first + j * (unroll * step)
        for u in range(unroll):
            lst_ref[pos_ref[j * unroll + u]] = word + u * step
        return carry

    lax.fori_loop(0, ntok // unroll, body, 0)


def _invert(pos_flat, ntok, pad):
    n = pos_flat.shape[0]
    assert n <= 1 << 16 and ntok * SUBLANES <= 1 << 16
    return pl.pallas_call(
        functools.partial(_invert_kernel, pad),
        out_shape=jax.ShapeDtypeStruct((n + pad,), I32),
        grid=(n // ntok,),
        in_specs=[pl.BlockSpec((ntok,), lambda k: (k,), memory_space=pltpu.SMEM)],
        out_specs=pl.BlockSpec(memory_space=pltpu.SMEM),
        compiler_params=pltpu.CompilerParams(dimension_semantics=("arbitrary",)),
        name="invert",
    )(pos_flat)


def _moe_kernel(cnt_ref, start_ref, lst_ref, h2p_ref, wg_ref, wu_ref, wd_ref, out_hbm,
                xbuf, stage, sem, state):
    e = pl.program_id(0)
    nchunk = SUBLANES
    ring = stage.shape[0]
    m = stage.shape[1] // nchunk
    nlist = lst_ref.shape[0] - m
    n = cnt_ref[e]

    def wait_slot(slot):
        pltpu.make_async_copy(stage.at[slot], out_hbm.at[pl.ds(0, m * nchunk), :], sem.at[slot]).wait()

    def send_rows(slot, base, occupied):
        for r in range(m):
            row = jnp.where(r < occupied, lax.shift_right_logical(lst_ref[base + r], 16), nlist + slot * m + r)
            pltpu.make_async_copy(stage.at[slot, pl.ds(r * nchunk, nchunk), :],
                                  out_hbm.at[pl.ds(pl.multiple_of(row * nchunk, nchunk), nchunk), :],
                                  sem.at[slot]).start()

    @pl.when(e == 0)
    def _():
        zeros = jnp.zeros((m, LANES), F32)
        packed = jnp.concatenate([_pack_bf16_pairs(zeros, zeros)] * nchunk, axis=0)
        for s in range(ring):
            stage[s] = packed
        for s in range(ring):
            fill = pltpu.make_async_copy(stage.at[s], out_hbm.at[pl.ds((nlist + s * m) * nchunk, m * nchunk), :],
                                         sem.at[s])
            fill.start()
            fill.wait()
        state[0] = 1
        state[1] = 0
        state[2] = 0

    def run(j, carry):
        done = state[0]
        slot = lax.rem(done, ring)
        base = start_ref[e] + j * m

        @pl.when(done >= ring)
        def _():
            wait_slot(slot)

        send_rows(lax.rem(done + ring - 1, ring), state[1], state[2])
        for r in range(m):
            src = pl.multiple_of(lst_ref[base + r] & 0xFFFF, nchunk)
            xbuf[r * nchunk:(r + 1) * nchunk, :] = h2p_ref[pl.ds(src, nchunk), :]
        los, his = [], []
        for c in range(nchunk):
            lo, hi = _unpack_pairs_f32(xbuf[pl.ds(c, m, stride=nchunk), :])
            los.append(lo.astype(BF16))
            his.append(hi.astype(BF16))
        xrows = jnp.concatenate(los + his, axis=-1)
        g = _dot(xrows, wg_ref[0].astype(BF16))
        u = _dot(xrows, wu_ref[0].astype(BF16))
        act = (g * _sigmoid(g) * u).astype(BF16)
        _store_packed_rows(stage.at[slot], _dot(act, wd_ref[0].astype(BF16)))
        state[0] = done + 1
        state[1] = base
        state[2] = jnp.minimum(n - j * m, m)
        return carry

    lax.fori_loop(0, (n + m - 1) // m, run, 0)

    @pl.when(e == pl.num_programs(0) - 1)
    def _():
        done = state[0]
        send_rows(lax.rem(done + ring - 1, ring), state[1], state[2])
        for s in range(ring):
            @pl.when(done > s)
            def _():
                wait_slot(s)


def _moe(h2p, lst, counts, starts, w_gate, w_up, w_down):
    ne, d, de = w_gate.shape
    nchunk = d // 2 // LANES
    assert nchunk == SUBLANES
    m, ring = MOE_RUN, MOE_RING
    nlist = lst.shape[0] - m
    wspec = lambda e, *_: (e, 0, 0)
    grid_spec = pltpu.PrefetchScalarGridSpec(
        num_scalar_prefetch=3,
        grid=(ne,),
        in_specs=[pl.BlockSpec(memory_space=pltpu.VMEM),
                  pl.BlockSpec((1, d, de), wspec),
                  pl.BlockSpec((1, d, de), wspec),
                  pl.BlockSpec((1, de, d), wspec)],
        out_specs=pl.BlockSpec(memory_space=pl.ANY),
        scratch_shapes=[pltpu.VMEM((m * nchunk, LANES), U32), pltpu.VMEM((ring, m * nchunk, LANES), U32),
                        pltpu.SemaphoreType.DMA((ring,)), pltpu.SMEM((3,), I32)],
    )
    return pl.pallas_call(
        _moe_kernel,
        out_shape=jax.ShapeDtypeStruct(((nlist + ring * m) * nchunk, LANES), U32),
        grid_spec=grid_spec,
        compiler_params=pltpu.CompilerParams(dimension_semantics=("arbitrary",), vmem_limit_bytes=VMEM_LIMIT),
        name="moe",
    )(counts, starts, lst, h2p, w_gate, w_up, w_down)


def _combine_kernel(alpha, *refs):
    y_refs = refs[:TOP_K]
    w_ref, x1_ref, mod_ref, wsg_ref, wsu_ref, wsd_ref, g_ref, b_ref, o_ref, wbc, routed = refs[TOP_K:]
    tt, d = x1_ref.shape
    half = d // 2
    nchunk = SUBLANES

    x1 = x1_ref[...]
    h2 = (x1 * (1.0 + mod_ref[0, 4:5, :]) + mod_ref[0, 3:4, :]).astype(BF16)
    g = _dot(h2, wsg_ref[...])
    u = _dot(h2, wsu_ref[...])
    shared = _dot((g * _sigmoid(g) * u).astype(BF16), wsd_ref[...])
    for k in range(TOP_K):
        wbc[k] = jnp.broadcast_to(w_ref[:, k:k + 1], (tt, LANES))

    for c in range(nchunk):
        acc_lo = acc_hi = None
        for k in range(TOP_K):
            lo, hi = _unpack_pairs_f32(y_refs[k][pl.ds(c, tt, stride=nchunk), :])
            wk = wbc[k]
            acc_lo = lo * wk if k == 0 else acc_lo + lo * wk
            acc_hi = hi * wk if k == 0 else acc_hi + hi * wk
        routed[:, c * LANES:(c + 1) * LANES] = acc_lo
        routed[:, half + c * LANES: half + (c + 1) * LANES] = acc_hi

    z = alpha * x1 + mod_ref[0, 5:6, :] * (routed[...] + shared)
    o_ref[...] = _layer_norm(z, g_ref[...], b_ref[...])


def _combine(y_ranked, w_tok, x1, mod, wsg_bf, wsu_bf, wsd_bf, ln_g, ln_b, alpha, seq):
    t, d = x1.shape
    ds_ = wsg_bf.shape[1]
    tt = COMBINE_TOKENS
    per_batch = seq // tt
    nsteps = t // tt
    nchunk = d // 2 // LANES
    assert nchunk == SUBLANES
    c2 = lambda i: (0, 0)
    y_specs = [pl.BlockSpec((tt * nchunk, LANES), functools.partial(lambda k, i: (k * nsteps + i, 0), k))
               for k in range(TOP_K)]
    return pl.pallas_call(
        functools.partial(_combine_kernel, alpha),
        out_shape=jax.ShapeDtypeStruct((t, d), F32),
        grid=(nsteps,),
        in_specs=y_specs + [pl.BlockSpec((tt, TOP_K), lambda i: (i, 0)),
                            pl.BlockSpec((tt, d), lambda i: (i, 0)),
                            pl.BlockSpec((1, N_MOD, d), lambda i: (i // per_batch, 0, 0)),
                            pl.BlockSpec((d, ds_), c2),
                            pl.BlockSpec((d, ds_), c2),
                            pl.BlockSpec((ds_, d), c2),
                            pl.BlockSpec((1, d), c2),
                            pl.BlockSpec((1, d), c2)],
        out_specs=pl.BlockSpec((tt, d), lambda i: (i, 0)),
        scratch_shapes=[pltpu.VMEM((TOP_K, tt, LANES), F32), pltpu.VMEM((tt, d), F32)],
        compiler_params=pltpu.CompilerParams(dimension_semantics=("arbitrary",), vmem_limit_bytes=VMEM_LIMIT),
        name="combine",
    )(*([y_ranked] * TOP_K), w_tok, x1, mod, wsg_bf, wsu_bf, wsd_bf, ln_g.reshape(1, d), ln_b.reshape(1, d))


def kernel(x, c, w_ada, b_ada, w_in, rel_bias, attn_gain, ret_gain, w_out, ln1_gain, ln1_bias, w_router, router_bias,
           w_gate, w_up, w_down, ws_gate, ws_up, ws_down, ln2_gain, ln2_bias):
    batch, seq, d = x.shape
    depth = w_ada.shape[0]
    alpha = (2.0 * depth) ** 0.25
    t = batch * seq
    assert t % COMBINE_TOKENS == 0
    xt = x.reshape(t, d)
    c_pad = jnp.zeros((8, d), F32).at[:batch].set(c)
    for l in range(depth):
        mod = _ada(c_pad, w_ada[l], b_ada[l])[:batch].reshape(batch, N_MOD, d)
        proj = _inproj(xt, mod, w_in[l], seq)
        mix = _mixer(proj, rel_bias[l], attn_gain[l], ret_gain[l], batch, seq)
        x1, h2p, idx_t, w_t, rank_t, cnt = _outproj(mix, xt, mod, w_out[l].astype(BF16), ln1_gain[l], ln1_bias[l],
                                                    w_router[l].T.astype(BF16), router_bias[l], alpha, seq)
        counts = cnt[:, 0].astype(I32)
        starts = jnp.cumsum(counts) - counts
        pos = _slots(idx_t, rank_t, starts)
        lst = _invert(pos.reshape(-1), t, MOE_RUN)
        y_ranked = _moe(h2p, lst, counts, starts, w_gate[l], w_up[l], w_down[l])
        xt = _combine(y_ranked, w_t.T, x1, mod, ws_gate[l].astype(BF16), ws_up[l].astype(BF16),
                      ws_down[l].astype(BF16), ln2_gain[l], ln2_bias[l], alpha, seq)
    return xt.reshape(batch, seq, d)
```

```python
import functools

import jax
import jax.numpy as jnp
import numpy as np
from jax import lax
from jax.experimental import pallas as pl
from jax.experimental.pallas import tpu as pltpu

F32 = jnp.float32
BF16 = jnp.bfloat16
U32 = jnp.uint32
I32 = jnp.int32

CHUNK = 64
LEFT_CHUNKS = 8
MAX_REL = 256
ATTN_HEADS = 8
HEAD_DIM = 128
RET_HEADS = 8
RET_KEY_DIM = 64
RET_VALUE_DIM = 128
ROPE_BASE = 10000.0
N_GROUPS = 8
TOPK_GROUPS = 4
TOP_K = 8
ROUTED_SCALE = 2.5
EPS = 1e-5
N_MOD = 6
LANES = 128
SUBLANES = 8

SEQ_BLOCK = 256
ROUTE_TOKENS = 512
ROUTE_SUBTILE = 256
MOE_RUN = 128
MOE_RING = 3
COMBINE_TOKENS = 256
NEG_BIG = -1e30
VMEM_LIMIT = 56 * 1024 * 1024


def _sigmoid(v):
    return 1.0 / (1.0 + jnp.exp(-v))


def _dot(a, b):
    return jnp.dot(a, b, preferred_element_type=F32)


def _dot_t(a, b):
    return lax.dot_general(a, b, (((1,), (1,)), ((), ())), preferred_element_type=F32)


def _ada_kernel(c_ref, w_ref, b_ref, o_ref):
    c = c_ref[...]
    s = (c * _sigmoid(c)).astype(BF16)
    o_ref[...] = _dot(s, w_ref[...].astype(BF16)) + b_ref[...]


def _ada(c_pad, w_ada, b_ada):
    d, n = w_ada.shape
    tn = 1024
    return pl.pallas_call(
        _ada_kernel,
        out_shape=jax.ShapeDtypeStruct((8, n), F32),
        grid=(n // tn,),
        in_specs=[pl.BlockSpec((8, d), lambda j: (0, 0)),
                  pl.BlockSpec((d, tn), lambda j: (0, j)),
                  pl.BlockSpec((1, tn), lambda j: (0, j))],
        out_specs=pl.BlockSpec((8, tn), lambda j: (0, j)),
        compiler_params=pltpu.CompilerParams(dimension_semantics=("arbitrary",), vmem_limit_bytes=VMEM_LIMIT),
        name="ada",
    )(c_pad, w_ada, b_ada.reshape(1, n))


def _inproj_kernel(x_ref, mod_ref, w_ref, o_ref, wbf_ref):
    @pl.when(pl.program_id(1) == 0)
    def _():
        wbf_ref[...] = w_ref[...].astype(BF16)

    shift = mod_ref[0, 0:1, :]
    scale = mod_ref[0, 1:2, :]
    h = (x_ref[...] * (1.0 + scale) + shift).astype(BF16)
    o_ref[...] = _dot(h, wbf_ref[...]).astype(o_ref.dtype)


def _inproj(x2, mod, w_in, seq):
    t, d = x2.shape
    n = w_in.shape[1]
    tm, tn = 1024, 1024
    per_batch = seq // tm
    return pl.pallas_call(
        _inproj_kernel,
        out_shape=jax.ShapeDtypeStruct((t, n), BF16),
        grid=(n // tn, t // tm),
        in_specs=[pl.BlockSpec((tm, d), lambda j, i: (i, 0)),
                  pl.BlockSpec((1, N_MOD, d), lambda j, i: (i // per_batch, 0, 0)),
                  pl.BlockSpec((d, tn), lambda j, i: (0, j))],
        out_specs=pl.BlockSpec((tm, tn), lambda j, i: (i, j)),
        scratch_shapes=[pltpu.VMEM((d, tn), BF16)],
        compiler_params=pltpu.CompilerParams(dimension_semantics=("arbitrary", "arbitrary"),
                                             vmem_limit_bytes=VMEM_LIMIT),
        name="inproj",
    )(x2, mod, w_in)


def _ret_decay_consts(blk):
    h = np.arange(RET_HEADS, dtype=np.float64)
    log_g = np.log(1.0 - 2.0 ** (-5.0 - h))
    n = np.arange(blk, dtype=np.float64)
    diff = n[:, None] - n[None, :]
    same = (n[:, None] // CHUNK) == (n[None, :] // CHUNK)
    later = (n[:, None] // CHUNK) > (n[None, :] // CHUNK)
    expo = np.where(same, np.abs(diff), diff)
    kscale = RET_KEY_DIM ** -0.5
    decay = np.where(same | later, np.exp(log_g[:, None, None] * expo[None]), 0.0) * kscale
    xi = np.exp(log_g[:, None] * (n[None, :] + 1.0))
    zeta = np.exp(log_g[:, None] * (blk - 1.0 - n[None, :])) * kscale
    g_blk = np.exp(log_g * blk)
    xi = np.broadcast_to(xi[:, :, None], (RET_HEADS, blk, LANES))
    zeta = np.broadcast_to(zeta[:, :, None], (RET_HEADS, blk, LANES))
    return (jnp.asarray(decay, F32), jnp.asarray(xi, F32), jnp.asarray(zeta, F32), [float(v) for v in g_blk])


def _mixer_kernel(g_blk, qa, ka0, ka1, ka2, va0, va1, va2, qb, kb, vb, gb, bias, cos, sin, decay, xi, zeta,
                  again, rgain, out, state, oa):
    i = pl.program_id(1)
    blk = qa.shape[0]

    @pl.when(i == 0)
    def _():
        state[...] = jnp.zeros_like(state)

    ok0 = i >= 2
    ok1 = i >= 1
    scale = HEAD_DIM ** -0.5
    ssq = jnp.zeros((blk, 1), F32)

    def raw_scores(h):
        sl = slice(h * HEAD_DIM, (h + 1) * HEAD_DIM)
        q = (qa[:, sl].astype(F32) * scale).astype(BF16)
        return _dot_t(q, ka0[:, sl]), _dot_t(q, ka1[:, sl]), _dot_t(q, ka2[:, sl])

    pending = raw_scores(0)
    for h in range(ATTN_HEADS):
        sl = slice(h * HEAD_DIM, (h + 1) * HEAD_DIM)
        r0, r1, r2 = pending
        if h + 1 < ATTN_HEADS:
            pending = raw_scores(h + 1)
        s0 = jnp.where(ok0, r0 + bias[h, :, 0:blk], NEG_BIG)
        s1 = jnp.where(ok1, r1 + bias[h, :, blk:2 * blk], NEG_BIG)
        s2 = r2 + bias[h, :, 2 * blk:3 * blk]
        m = jnp.max(jnp.maximum(jnp.maximum(s0, s1), s2), axis=-1, keepdims=True)
        p0 = jnp.exp(s0 - m)
        p1 = jnp.exp(s1 - m)
        p2 = jnp.exp(s2 - m)
        denom = jnp.sum(p0 + p1 + p2, axis=-1, keepdims=True)
        o = _dot(p0.astype(BF16), va0[:, sl]) + _dot(p1.astype(BF16), va1[:, sl]) + _dot(p2.astype(BF16), va2[:, sl])
        o = o * (1.0 / denom)
        oa[:, sl] = o
        ssq = ssq + jnp.sum(o * o, axis=-1, keepdims=True)
    width_a = ATTN_HEADS * HEAD_DIM
    inv_rms = lax.rsqrt(ssq * (1.0 / width_a) + EPS)
    out[:, 0:width_a] = (oa[...] * inv_rms * again[...]).astype(out.dtype)

    cosv = cos[...]
    sinv = sin[...]
    lane = lax.broadcasted_iota(jnp.int32, (blk, LANES), 1)
    first_half = (lane % RET_KEY_DIM) < (RET_KEY_DIM // 2)
    low_head = lane < RET_KEY_DIM

    def rope(v):
        rot = jnp.where(first_half, pltpu.roll(v, LANES - RET_KEY_DIM // 2, 1), pltpu.roll(v, RET_KEY_DIM // 2, 1))
        return v * cosv + rot * sinv

    for j in range(RET_HEADS // 2):
        sl2 = slice(j * LANES, (j + 1) * LANES)
        qr = rope(qb[:, sl2].astype(F32))
        k2 = rope(kb[:, sl2].astype(F32)).astype(BF16)
        for hh in range(2):
            h = 2 * j + hh
            slv = slice(h * RET_VALUE_DIM, (h + 1) * RET_VALUE_DIM)
            qm = jnp.where(low_head if hh == 0 else jnp.logical_not(low_head), qr, 0.0).astype(BF16)
            v = vb[:, slv]
            st = state[h]
            scores = _dot_t(qm, k2) * decay[h]
            ret = _dot(scores.astype(BF16), v) + _dot(qm, st.astype(BF16)) * xi[h]
            zv = (v.astype(F32) * zeta[h]).astype(BF16)
            upd = lax.dot_general(k2, zv, (((0,), (0,)), ((), ())), preferred_element_type=F32)
            state[h] = g_blk[h] * st + upd
            mu = jnp.mean(ret, axis=-1, keepdims=True)
            cen = ret - mu
            var = jnp.mean(cen * cen, axis=-1, keepdims=True)
            gate = gb[:, slv].astype(F32)
            yb = gate * _sigmoid(gate) * (cen * lax.rsqrt(var + EPS)) * rgain[:, slv]
            out[:, width_a + h * RET_VALUE_DIM: width_a + (h + 1) * RET_VALUE_DIM] = yb.astype(out.dtype)


def _attn_bias_table(rel_bias, blk):
    nk = 3 * blk
    period = nk + blk
    offs = np.concatenate([np.arange(nk), np.zeros((1,), np.int64), np.arange(-(blk - 1), 0)])
    rel_idx = np.clip(2 * blk - offs, -(CHUNK - 1), MAX_REL) + (CHUNK - 1)
    line = rel_bias[:, jnp.asarray(rel_idx)].astype(F32)
    heads = line.shape[0]
    flat = jnp.broadcast_to(line[:, None, :], (heads, blk, period)).reshape(heads, blk * period)
    tab = flat[:, :blk * (period - 1)].reshape(heads, blk, period - 1)[:, :, :nk]
    r = np.arange(blk)[:, None]
    c = np.arange(nk)[None, :]
    qc = r // CHUNK + (2 * blk) // CHUNK
    kc = c // CHUNK
    in_band = (kc <= qc) & (kc >= qc - LEFT_CHUNKS)
    return jnp.where(jnp.asarray(in_band)[None], tab, NEG_BIG)


def _rope_tables(seq):
    half = RET_KEY_DIM // 2
    inv = ROPE_BASE ** (-jnp.arange(half, dtype=F32) / half)
    ang = jnp.arange(seq, dtype=F32)[:, None] * inv[None, :]
    cos, sin = jnp.cos(ang), jnp.sin(ang)
    reps = LANES // RET_KEY_DIM
    cos_t = jnp.tile(jnp.concatenate([cos, cos], axis=-1), (1, reps))
    sin_t = jnp.tile(jnp.concatenate([-sin, sin], axis=-1), (1, reps))
    return cos_t, sin_t


def _mixer(proj, rel_bias, attn_gain, ret_gain, batch, seq):
    t = proj.shape[0]
    blk = SEQ_BLOCK
    assert 2 * blk == LEFT_CHUNKS * CHUNK and seq % blk == 0
    nb = seq // blk
    wa = ATTN_HEADS * HEAD_DIM
    wqk = RET_HEADS * RET_KEY_DIM
    wv = RET_HEADS * RET_VALUE_DIM
    assert wa == wv == 2 * wqk
    bias = _attn_bias_table(rel_bias, blk)
    cos_t, sin_t = _rope_tables(seq)
    decay, xi, zeta, g_blk = _ret_decay_consts(blk)

    def row(b, i):
        return b * nb + i

    def kspec(back, col):
        return pl.BlockSpec((blk, wa), lambda b, i: (row(b, jnp.maximum(i - back, 0)), col))

    const3 = lambda b, i: (0, 0, 0)
    in_specs = [
        pl.BlockSpec((blk, wa), lambda b, i: (row(b, i), 0)),
        kspec(2, 1), kspec(1, 1), kspec(0, 1),
        kspec(2, 2), kspec(1, 2), kspec(0, 2),
        pl.BlockSpec((blk, wqk), lambda b, i: (row(b, i), 3 * wa // wqk)),
        pl.BlockSpec((blk, wqk), lambda b, i: (row(b, i), 3 * wa // wqk + 1)),
        pl.BlockSpec((blk, wv), lambda b, i: (row(b, i), (3 * wa + 2 * wqk) // wv)),
        pl.BlockSpec((blk, wv), lambda b, i: (row(b, i), (3 * wa + 2 * wqk) // wv + 1)),
        pl.BlockSpec((ATTN_HEADS, blk, 3 * blk), const3),
        pl.BlockSpec((blk, LANES), lambda b, i: (i, 0)),
        pl.BlockSpec((blk, LANES), lambda b, i: (i, 0)),
        pl.BlockSpec((RET_HEADS, blk, blk), const3),
        pl.BlockSpec((RET_HEADS, blk, LANES), const3),
        pl.BlockSpec((RET_HEADS, blk, LANES), const3),
        pl.BlockSpec((1, wa), lambda b, i: (0, 0)),
        pl.BlockSpec((1, wv), lambda b, i: (0, 0)),
    ]
    return pl.pallas_call(
        functools.partial(_mixer_kernel, g_blk),
        out_shape=jax.ShapeDtypeStruct((t, wa + wv), BF16),
        grid=(batch, nb),
        in_specs=in_specs,
        out_specs=pl.BlockSpec((blk, wa + wv), lambda b, i: (row(b, i), 0)),
        scratch_shapes=[pltpu.VMEM((RET_HEADS, LANES, RET_VALUE_DIM), F32), pltpu.VMEM((blk, wa), F32)],
        compiler_params=pltpu.CompilerParams(dimension_semantics=("arbitrary", "arbitrary"),
                                             vmem_limit_bytes=VMEM_LIMIT),
        name="mixer",
    )(proj, proj, proj, proj, proj, proj, proj, proj, proj, proj, proj, bias, cos_t, sin_t, decay, xi, zeta,
      attn_gain.reshape(1, wa), ret_gain.reshape(1, wv))


def _layer_norm(z, gain, bias):
    mu = jnp.mean(z, axis=-1, keepdims=True)
    cen = z - mu
    var = jnp.mean(cen * cen, axis=-1, keepdims=True)
    return cen * lax.rsqrt(var + EPS) * gain + bias


def _pack_bf16_pairs(lo, hi):
    return pltpu.pack_elementwise([lo, hi], packed_dtype=BF16)


def _unpack_pairs_f32(words):
    lo = pltpu.unpack_elementwise(words, index=0, packed_dtype=BF16, unpacked_dtype=F32)
    hi = pltpu.unpack_elementwise(words, index=1, packed_dtype=BF16, unpacked_dtype=F32)
    return lo, hi


def _store_packed_rows(ref, val):
    tm, d = val.shape
    half = d // 2
    nchunk = half // LANES
    for c in range(nchunk):
        words = _pack_bf16_pairs(val[:, c * LANES:(c + 1) * LANES], val[:, half + c * LANES: half + (c + 1) * LANES])
        ref[pl.ds(c, tm, stride=nchunk), :] = words


def _route_tile(scores_t, bias_col):
    ne, tm = scores_t.shape
    per_group = ne // N_GROUPS
    biased = scores_t + bias_col
    neg_inf = -jnp.inf
    iota_g = lax.broadcasted_iota(I32, (per_group, tm), 0)
    gscore = []
    for g in range(N_GROUPS):
        b = biased[g * per_group:(g + 1) * per_group, :]
        m1 = jnp.max(b, axis=0, keepdims=True)
        first = jnp.min(jnp.where(b == m1, iota_g, per_group), axis=0, keepdims=True)
        m2 = jnp.max(jnp.where(iota_g == first, neg_inf, b), axis=0, keepdims=True)
        gscore.append(m1 + m2)
    masked_parts = []
    for g in range(N_GROUPS):
        beaten = jnp.zeros((1, tm), I32)
        for o in range(N_GROUPS):
            if o == g:
                continue
            wins = (gscore[o] >= gscore[g]) if o < g else (gscore[o] > gscore[g])
            beaten = beaten + wins.astype(I32)
        keep = beaten < TOPK_GROUPS
        masked_parts.append(jnp.where(keep, biased[g * per_group:(g + 1) * per_group, :], neg_inf))
    masked = jnp.concatenate(masked_parts, axis=0)
    eiota = lax.broadcasted_iota(I32, (ne, tm), 0)
    ids, vals = [], []
    for _ in range(TOP_K):
        m = jnp.max(masked, axis=0, keepdims=True)
        first = jnp.min(jnp.where(masked == m, eiota, ne), axis=0, keepdims=True)
        hit = eiota == first
        vals.append(jnp.sum(jnp.where(hit, scores_t, 0.0), axis=0, keepdims=True))
        ids.append(first)
        masked = jnp.where(hit, neg_inf, masked)
    return ids, vals


def _outproj_kernel(alpha, mix_ref, x_ref, mod_ref, wout_ref, g_ref, b_ref, wrt_ref, rb_ref, tri_ref, ones_ref,
                    x1_ref, h2p_ref, idx_ref, w_ref, rank_ref, cnt_ref, carry):
    i = pl.program_id(0)
    tm, d = x_ref.shape
    nchunk = d // 2 // LANES
    ne = wrt_ref.shape[0]
    sub = ROUTE_SUBTILE

    @pl.when(i == 0)
    def _():
        carry[...] = jnp.zeros_like(carry)

    gate1 = mod_ref[0, 2:3, :]
    scale2 = 1.0 + mod_ref[0, 4:5, :]
    shift2 = mod_ref[0, 3:4, :]
    eiota = lax.broadcasted_iota(I32, (ne, LANES), 0)
    counted = carry[...]
    for part in range(tm // sub):
        rs = slice(part * sub, (part + 1) * sub)
        y = _dot(mix_ref[rs, :], wout_ref[...])
        x1 = _layer_norm(alpha * x_ref[rs, :] + gate1 * y, g_ref[...], b_ref[...])
        x1_ref[rs, :] = x1
        h2 = x1 * scale2 + shift2
        _store_packed_rows(h2p_ref.at[pl.ds(part * sub * nchunk, sub * nchunk), :], h2)
        logits_t = _dot_t(wrt_ref[...], h2.astype(BF16))
        for q in range(sub // LANES):
            ls = slice(part * sub + q * LANES, part * sub + (q + 1) * LANES)
            scores_t = _sigmoid(logits_t[:, q * LANES:(q + 1) * LANES])
            ids, vals = _route_tile(scores_t, rb_ref[:, 0:1])
            chosen = jnp.zeros((ne, LANES), F32)
            for k in range(TOP_K):
                chosen = chosen + (eiota == ids[k]).astype(F32)
            chosen_bf = chosen.astype(BF16)
            before = _dot(chosen_bf, tri_ref[...]) - chosen + counted
            ranks = [jnp.sum(jnp.where(eiota == ids[k], before, 0.0), axis=0, keepdims=True)
                     for k in range(TOP_K)]
            counted = counted + _dot(chosen_bf, ones_ref[...])
            wsel = jnp.concatenate(vals, axis=0)
            w_ref[:, ls] = wsel / jnp.sum(wsel, axis=0, keepdims=True) * ROUTED_SCALE
            idx_ref[:, ls] = jnp.concatenate(ids, axis=0)
            rank_ref[:, ls] = jnp.concatenate(ranks, axis=0).astype(I32)
    carry[...] = counted
    cnt_ref[...] = counted


def _outproj(mix, x2, mod, w_out_bf, ln_g, ln_b, w_router_t_bf, router_bias, alpha, seq):
    t, d = x2.shape
    ne = w_router_t_bf.shape[0]
    tm = ROUTE_TOKENS
    per_batch = seq // tm
    nchunk = d // 2 // LANES
    tri = jnp.asarray(np.triu(np.ones((LANES, LANES), np.float32)), BF16)
    ones = jnp.ones((LANES, LANES), BF16)
    rb = jnp.broadcast_to(router_bias.astype(F32)[:, None], (ne, LANES))
    c2 = lambda i: (0, 0)
    return pl.pallas_call(
        functools.partial(_outproj_kernel, alpha),
        out_shape=(jax.ShapeDtypeStruct((t, d), F32),
                   jax.ShapeDtypeStruct((t * nchunk, LANES), U32),
                   jax.ShapeDtypeStruct((TOP_K, t), I32),
                   jax.ShapeDtypeStruct((TOP_K, t), F32),
                   jax.ShapeDtypeStruct((TOP_K, t), I32),
                   jax.ShapeDtypeStruct((ne, LANES), F32)),
        grid=(t // tm,),
        in_specs=[pl.BlockSpec((tm, d), lambda i: (i, 0)),
                  pl.BlockSpec((tm, d), lambda i: (i, 0)),
                  pl.BlockSpec((1, N_MOD, d), lambda i: (i // per_batch, 0, 0)),
                  pl.BlockSpec((d, d), c2),
                  pl.BlockSpec((1, d), c2),
                  pl.BlockSpec((1, d), c2),
                  pl.BlockSpec((ne, d), c2),
                  pl.BlockSpec((ne, LANES), c2),
                  pl.BlockSpec((LANES, LANES), c2),
                  pl.BlockSpec((LANES, LANES), c2)],
        out_specs=(pl.BlockSpec((tm, d), lambda i: (i, 0)),
                   pl.BlockSpec((tm * nchunk, LANES), lambda i: (i, 0)),
                   pl.BlockSpec((TOP_K, tm), lambda i: (0, i)),
                   pl.BlockSpec((TOP_K, tm), lambda i: (0, i)),
                   pl.BlockSpec((TOP_K, tm), lambda i: (0, i)),
                   pl.BlockSpec((ne, LANES), c2)),
        scratch_shapes=[pltpu.VMEM((ne, LANES), F32)],
        compiler_params=pltpu.CompilerParams(dimension_semantics=("arbitrary",), vmem_limit_bytes=VMEM_LIMIT),
        name="outproj",
    )(mix, x2, mod, w_out_bf, ln_g.reshape(1, d), ln_b.reshape(1, d), w_router_t_bf, rb, tri, ones)


def _slots_kernel(idx_ref, rank_ref, start_ref, pos_ref):
    ne = start_ref.shape[0]
    tm = idx_ref.shape[1]
    eiota = lax.broadcasted_iota(I32, (ne, tm), 0)
    start = jnp.concatenate([start_ref[...]] * (tm // LANES), axis=-1)
    rows = []
    for k in range(TOP_K):
        rows.append(jnp.sum(jnp.where(eiota == idx_ref[k:k + 1, :], start, 0), axis=0, keepdims=True))
    pos_ref[...] = jnp.concatenate(rows, axis=0) + rank_ref[...]


def _slots(idx_t, rank_t, start):
    k, t = idx_t.shape
    ne = start.shape[0]
    tm = 512
    spec = pl.BlockSpec((k, tm), lambda i: (0, i))
    return pl.pallas_call(
        _slots_kernel,
        out_shape=jax.ShapeDtypeStruct((k, t), I32),
        grid=(t // tm,),
        in_specs=[spec, spec, pl.BlockSpec((ne, LANES), lambda i: (0, 0))],
        out_specs=spec,
        compiler_params=pltpu.CompilerParams(dimension_semantics=("arbitrary",)),
        name="slots",
    )(idx_t, rank_t, jnp.broadcast_to(start[:, None], (ne, LANES)))


def _invert_kernel(pad, pos_ref, lst_ref):
    ntok = pos_ref.shape[0]
    k = pl.program_id(0)
    n = lst_ref.shape[0] - pad
    unroll = 8

    @pl.when(k == 0)
    def _():
        for p in range(pad):
            lst_ref[n + p] = 0

    step = (1 << 16) + SUBLANES
    first = (k * ntok) << 16

    def body(j, carry):
        word = first + j * (unroll * step)
        for u in range(unroll):
            lst_ref[pos_ref[j * unroll + u]] = word + u * step
        return carry

    lax.fori_loop(0, ntok // unroll, body, 0)


def _invert(pos_flat, ntok, pad):
    n = pos_flat.shape[0]
    assert n <= 1 << 16 and ntok * SUBLANES <= 1 << 16
    return pl.pallas_call(
        functools.partial(_invert_kernel, pad),
        out_shape=jax.ShapeDtypeStruct((n + pad,), I32),
        grid=(n // ntok,),
        in_specs=[pl.BlockSpec((ntok,), lambda k: (k,), memory_space=pltpu.SMEM)],
        out_specs=pl.BlockSpec(memory_space=pltpu.SMEM),
        compiler_params=pltpu.CompilerParams(dimension_semantics=("arbitrary",)),
        name="invert",
    )(pos_flat)


def _moe_kernel(cnt_ref, start_ref, lst_ref, h2p_ref, wg_ref, wu_ref, wd_ref, out_hbm,
                xbuf, stage, sem, state):
    e = pl.program_id(0)
    nchunk = SUBLANES
    ring = stage.shape[0]
    m = stage.shape[1] // nchunk
    nlist = lst_ref.shape[0] - m
    n = cnt_ref[e]

    def wait_slot(slot):
        pltpu.make_async_copy(stage.at[slot], out_hbm.at[pl.ds(0, m * nchunk), :], sem.at[slot]).wait()

    def send_rows(slot, base, occupied):
        for r in range(m):
            row = jnp.where(r < occupied, lax.shift_right_logical(lst_ref[base + r], 16), nlist + slot * m + r)
            pltpu.make_async_copy(stage.at[slot, pl.ds(r * nchunk, nchunk), :],
                                  out_hbm.at[pl.ds(pl.multiple_of(row * nchunk, nchunk), nchunk), :],
                                  sem.at[slot]).start(priority=r % 2)

    @pl.when(e == 0)
    def _():
        zeros = jnp.zeros((m, LANES), F32)
        packed = jnp.concatenate([_pack_bf16_pairs(zeros, zeros)] * nchunk, axis=0)
        for s in range(ring):
            stage[s] = packed
        for s in range(ring):
            fill = pltpu.make_async_copy(stage.at[s], out_hbm.at[pl.ds((nlist + s * m) * nchunk, m * nchunk), :],
                                         sem.at[s])
            fill.start()
            fill.wait()
        state[0] = 1
        state[1] = 0
        state[2] = 0

    def run(j, carry):
        done = state[0]
        slot = lax.rem(done, ring)
        base = start_ref[e] + j * m

        @pl.when(done >= ring)
        def _():
            wait_slot(slot)

        send_rows(lax.rem(done + ring - 1, ring), state[1], state[2])
        for r in range(m):
            src = pl.multiple_of(lst_ref[base + r] & 0xFFFF, nchunk)
            xbuf[r * nchunk:(r + 1) * nchunk, :] = h2p_ref[pl.ds(src, nchunk), :]
        los, his = [], []
        for c in range(nchunk):
            lo, hi = _unpack_pairs_f32(xbuf[pl.ds(c, m, stride=nchunk), :])
            los.append(lo.astype(BF16))
            his.append(hi.astype(BF16))
        xrows = jnp.concatenate(los + his, axis=-1)
        g = _dot(xrows, wg_ref[0].astype(BF16))
        u = _dot(xrows, wu_ref[0].astype(BF16))
        act = (g * _sigmoid(g) * u).astype(BF16)
        _store_packed_rows(stage.at[slot], _dot(act, wd_ref[0].astype(BF16)))
        state[0] = done + 1
        state[1] = base
        state[2] = jnp.minimum(n - j * m, m)
        return carry

    lax.fori_loop(0, (n + m - 1) // m, run, 0)

    @pl.when(e == pl.num_programs(0) - 1)
    def _():
        done = state[0]
        send_rows(lax.rem(done + ring - 1, ring), state[1], state[2])
        for s in range(ring):
            @pl.when(done > s)
            def _():
                wait_slot(s)


def _moe(h2p, lst, counts, starts, w_gate, w_up, w_down):
    ne, d, de = w_gate.shape
    nchunk = d // 2 // LANES
    assert nchunk == SUBLANES
    m, ring = MOE_RUN, MOE_RING
    nlist = lst.shape[0] - m
    wspec = lambda e, *_: (e, 0, 0)
    grid_spec = pltpu.PrefetchScalarGridSpec(
        num_scalar_prefetch=3,
        grid=(ne,),
        in_specs=[pl.BlockSpec(memory_space=pltpu.VMEM),
                  pl.BlockSpec((1, d, de), wspec),
                  pl.BlockSpec((1, d, de), wspec),
                  pl.BlockSpec((1, de, d), wspec)],
        out_specs=pl.BlockSpec(memory_space=pl.ANY),
        scratch_shapes=[pltpu.VMEM((m * nchunk, LANES), U32), pltpu.VMEM((ring, m * nchunk, LANES), U32),
                        pltpu.SemaphoreType.DMA((ring,)), pltpu.SMEM((3,), I32)],
    )
    return pl.pallas_call(
        _moe_kernel,
        out_shape=jax.ShapeDtypeStruct(((nlist + ring * m) * nchunk, LANES), U32),
        grid_spec=grid_spec,
        compiler_params=pltpu.CompilerParams(dimension_semantics=("arbitrary",), vmem_limit_bytes=VMEM_LIMIT),
        name="moe",
    )(counts, starts, lst, h2p, w_gate, w_up, w_down)


def _combine_kernel(alpha, *refs):
    y_refs = refs[:TOP_K]
    w_ref, x1_ref, mod_ref, wsg_ref, wsu_ref, wsd_ref, g_ref, b_ref, o_ref, wbc, routed = refs[TOP_K:]
    tt, d = x1_ref.shape
    half = d // 2
    nchunk = SUBLANES

    x1 = x1_ref[...]
    h2 = (x1 * (1.0 + mod_ref[0, 4:5, :]) + mod_ref[0, 3:4, :]).astype(BF16)
    g = _dot(h2, wsg_ref[...])
    u = _dot(h2, wsu_ref[...])
    shared = _dot((g * _sigmoid(g) * u).astype(BF16), wsd_ref[...])
    for k in range(TOP_K):
        wbc[k] = jnp.broadcast_to(w_ref[:, k:k + 1], (tt, LANES))

    for c in range(nchunk):
        acc_lo = acc_hi = None
        for k in range(TOP_K):
            lo, hi = _unpack_pairs_f32(y_refs[k][pl.ds(c, tt, stride=nchunk), :])
            wk = wbc[k]
            acc_lo = lo * wk if k == 0 else acc_lo + lo * wk
            acc_hi = hi * wk if k == 0 else acc_hi + hi * wk
        routed[:, c * LANES:(c + 1) * LANES] = acc_lo
        routed[:, half + c * LANES: half + (c + 1) * LANES] = acc_hi

    z = alpha * x1 + mod_ref[0, 5:6, :] * (routed[...] + shared)
    o_ref[...] = _layer_norm(z, g_ref[...], b_ref[...])


def _combine(y_ranked, w_tok, x1, mod, wsg_bf, wsu_bf, wsd_bf, ln_g, ln_b, alpha, seq):
    t, d = x1.shape
    ds_ = wsg_bf.shape[1]
    tt = COMBINE_TOKENS
    per_batch = seq // tt
    nsteps = t // tt
    nchunk = d // 2 // LANES
    assert nchunk == SUBLANES
    c2 = lambda i: (0, 0)
    y_specs = [pl.BlockSpec((tt * nchunk, LANES), functools.partial(lambda k, i: (k * nsteps + i, 0), k))
               for k in range(TOP_K)]
    return pl.pallas_call(
        functools.partial(_combine_kernel, alpha),
        out_shape=jax.ShapeDtypeStruct((t, d), F32),
        grid=(nsteps,),
        in_specs=y_specs + [pl.BlockSpec((tt, TOP_K), lambda i: (i, 0)),
                            pl.BlockSpec((tt, d), lambda i: (i, 0)),
                            pl.BlockSpec((1, N_MOD, d), lambda i: (i // per_batch, 0, 0)),
                            pl.BlockSpec((d, ds_), c2),
                            pl.BlockSpec((d, ds_), c2),
                            pl.BlockSpec((ds_, d), c2),
                            pl.BlockSpec((1, d), c2),
                            pl.BlockSpec((1, d), c2)],
        out_specs=pl.BlockSpec((tt, d), lambda i: (i, 0)),
        scratch_shapes=[pltpu.VMEM((TOP_K, tt, LANES), F32), pltpu.VMEM((tt, d), F32)],
        compiler_params=pltpu.CompilerParams(dimension_semantics=("arbitrary",), vmem_limit_bytes=VMEM_LIMIT),
        name="combine",
    )(*([y_ranked] * TOP_K), w_tok, x1, mod, wsg_bf, wsu_bf, wsd_bf, ln_g.reshape(1, d), ln_b.reshape(1, d))


def kernel(x, c, w_ada, b_ada, w_in, rel_bias, attn_gain, ret_gain, w_out, ln1_gain, ln1_bias, w_router, router_bias,
           w_gate, w_up, w_down, ws_gate, ws_up, ws_down, ln2_gain, ln2_bias):
    batch, seq, d = x.shape
    depth = w_ada.shape[0]
    alpha = (2.0 * depth) ** 0.25
    t = batch * seq
    assert t % COMBINE_TOKENS == 0
    xt = x.reshape(t, d)
    c_pad = jnp.zeros((8, d), F32).at[:batch].set(c)
    for l in range(depth):
        mod = _ada(c_pad, w_ada[l], b_ada[l])[:batch].reshape(batch, N_MOD, d)
        proj = _inproj(xt, mod, w_in[l], seq)
        mix = _mixer(proj, rel_bias[l], attn_gain[l], ret_gain[l], batch, seq)
        x1, h2p, idx_t, w_t, rank_t, cnt = _outproj(mix, xt, mod, w_out[l].astype(BF16), ln1_gain[l], ln1_bias[l],
                                                    w_router[l].T.astype(BF16), router_bias[l], alpha, seq)
        counts = cnt[:, 0].astype(I32)
        starts = jnp.cumsum(counts) - counts
        pos = _slots(idx_t, rank_t, starts)
        lst = _invert(pos.reshape(-1), t, MOE_RUN)
        y_ranked = _moe(h2p, lst, counts, starts, w_gate[l], w_up[l], w_down[l])
        xt = _combine(y_ranked, w_t.T, x1, mod, ws_gate[l].astype(BF16), ws_up[l].astype(BF16),
                      ws_down[l].astype(BF16), ln2_gain[l], ln2_bias[l], alpha, seq)
    return xt.reshape(batch, seq, d)
```
